```python
import math
import jax, jax.numpy as jnp
from jax import lax
import numpy as np

D_MODEL = 1024
BATCH = 4
SEQ = 8192
DEPTH = 1

MIX_WIDTH = D_MODEL
POOL_WIDTH = MIX_WIDTH // 2
POOL_WINDOWS = (2, 4, 8, 16)
N_POOL_GROUPS = len(POOL_WINDOWS)
POOL_GROUP_WIDTH = POOL_WIDTH // N_POOL_GROUPS
ATTN_WIDTH = MIX_WIDTH - POOL_WIDTH
SB_HEAD_DIM = 64
SB_HEADS = ATTN_WIDTH // SB_HEAD_DIM
Q_BLOCK = 128
IN_WIDTH = POOL_WIDTH + 3 * ATTN_WIDTH
MEM_LEN = 256
MEM_HEADS = 4
MEM_HEAD_DIM = D_MODEL // MEM_HEADS
N_GROUPS = 4
EXPERTS_PER_GROUP = 4
N_EXPERTS = N_GROUPS * EXPERTS_PER_GROUP
TOP_K_EXPERT = 2
EXPERT_FF = D_MODEL // 2
RMS_EPS = 1e-6

kernel_name = "hybrid_pool_stickbreak_hmoe_layer"


def rms_norm(x, gain):
    xf = x.astype(jnp.float32)
    y = xf * lax.rsqrt(jnp.mean(xf * xf, axis=-1, keepdims=True) + RMS_EPS)
    return (y * gain.astype(jnp.float32)).astype(x.dtype)


def causal_multiscale_pool(u, w_pool, pool_scale):
    S = u.shape[1]
    t = jnp.arange(S)
    groups = jnp.split(u, N_POOL_GROUPS, axis=-1)
    outs = []
    for g, (ug, w) in enumerate(zip(groups, POOL_WINDOWS)):
        uf = ug.astype(jnp.float32)
        prefix = jnp.pad(jnp.cumsum(uf, axis=1), ((0, 0), (1, 0), (0, 0)))
        upper = prefix[:, 1:]
        lower = jnp.pad(prefix, ((0, 0), (w - 1, 0), (0, 0)))[:, :S]
        count = jnp.minimum(t + 1, w).astype(jnp.float32)[None, :, None]
        pooled = (upper - lower) / count - uf
        outs.append(jnp.einsum('bsc,cd->bsd', pooled.astype(u.dtype), w_pool[g]))
    return jnp.concatenate(outs, axis=-1) * pool_scale


def stick_breaking_attention(q, k, v):
    B, H, S, dh = q.shape
    n_blocks = S // Q_BLOCK
    scale = 1.0 / math.sqrt(dh)
    q_blocks = jnp.moveaxis(q.reshape(B, H, n_blocks, Q_BLOCK, dh), 2, 0)
    s_idx = jnp.arange(S)

    def one_block(args):
        qb, blk = args
        z = jnp.einsum('bhqd,bhkd->bhqk', qb, k).astype(jnp.float32) * scale
        t_idx = blk * Q_BLOCK + jnp.arange(Q_BLOCK)
        causal = s_idx[None, :] < t_idx[:, None]
        log_not = jnp.where(causal, jax.nn.log_sigmoid(-z), 0.0)
        suffix = lax.cumsum(log_not, axis=3, reverse=True) - log_not
        weights = jnp.where(causal, jnp.exp(jax.nn.log_sigmoid(z) + suffix), 0.0)
        return jnp.einsum('bhqk,bhkd->bhqd', weights.astype(v.dtype), v)

    out = lax.map(one_block, (q_blocks, jnp.arange(n_blocks)))
    return jnp.moveaxis(out, 0, 2).reshape(B, H, S, dh)


def memory_cross_attention(xn, memn, w_q, w_k, w_v, w_o):
    B, S, D = xn.shape
    q = jnp.einsum('bsd,de->bse', xn, w_q).reshape(B, S, MEM_HEADS, MEM_HEAD_DIM)
    k = jnp.einsum('bmd,de->bme', memn, w_k).reshape(B, -1, MEM_HEADS, MEM_HEAD_DIM)
    v = jnp.einsum('bmd,de->bme', memn, w_v).reshape(B, -1, MEM_HEADS, MEM_HEAD_DIM)
    s = jnp.einsum('bshd,bmhd->bhsm', q, k).astype(jnp.float32) / math.sqrt(MEM_HEAD_DIM)
    p = jax.nn.softmax(s, axis=-1).astype(v.dtype)
    o = jnp.einsum('bhsm,bmhd->bshd', p, v).reshape(B, S, D)
    return jnp.einsum('bse,ed->bsd', o, w_o)


def hierarchical_moe(x, w_group, w_expert, w_gate, w_up, w_down):
    B, S, D = x.shape
    xf = x.reshape(-1, D)
    g_prob = jax.nn.softmax(jnp.einsum('nd,dg->ng', xf, w_group).astype(jnp.float32), axis=-1)
    g_gate, g_idx = lax.top_k(g_prob, 1)
    e_logits = jnp.einsum('nd,de->ne', xf, w_expert).astype(jnp.float32)
    e_logits = e_logits.reshape(-1, N_GROUPS, EXPERTS_PER_GROUP)
    e_logits = jnp.take_along_axis(e_logits, g_idx[:, :, None], axis=1)[:, 0]
    e_prob = jax.nn.softmax(e_logits, axis=-1)
    e_w, e_idx = lax.top_k(e_prob, TOP_K_EXPERT)
    e_w = e_w / jnp.sum(e_w, axis=-1, keepdims=True)
    weights = g_gate * e_w
    expert_ids = g_idx * EXPERTS_PER_GROUP + e_idx
    combine = jnp.einsum('nk,nke->ne', weights,
                         jax.nn.one_hot(expert_ids, N_EXPERTS, dtype=jnp.float32))
    y = jnp.zeros(xf.shape, jnp.float32)
    for e in range(N_EXPERTS):
        h = jax.nn.silu(xf @ w_gate[e]) * (xf @ w_up[e])
        y = y + combine[:, e:e + 1] * (h @ w_down[e]).astype(jnp.float32)
    return y.astype(x.dtype).reshape(B, S, D)


def setup_inputs(seed: int = 0) -> dict:
    key = jax.random.key(seed)
    ks = jax.random.split(key, 20)
    f32 = jnp.float32

    def nrm(k, shape, fan_in):
        return jax.random.normal(k, shape, f32) * (fan_in ** -0.5)

    def gain(k, shape):
        return 1.0 + 0.02 * jax.random.normal(k, shape, f32)

    L = DEPTH
    return {
        "x": jax.random.normal(ks[0], (BATCH, SEQ, D_MODEL), f32),
        "mem": jax.random.normal(ks[1], (BATCH, MEM_LEN, D_MODEL), f32),
        "norm_mix": gain(ks[2], (L, D_MODEL)),
        "w_in": nrm(ks[3], (L, D_MODEL, IN_WIDTH), D_MODEL),
        "w_pool": nrm(ks[4], (L, N_POOL_GROUPS, POOL_GROUP_WIDTH, POOL_GROUP_WIDTH), POOL_GROUP_WIDTH),
        "pool_scale": gain(ks[5], (L, POOL_WIDTH)),
        "w_out": nrm(ks[6], (L, MIX_WIDTH, D_MODEL), MIX_WIDTH),
        "norm_cross": gain(ks[7], (L, D_MODEL)),
        "norm_mem": gain(ks[8], (L, D_MODEL)),
        "w_q_mem": nrm(ks[9], (L, D_MODEL, D_MODEL), D_MODEL),
        "w_k_mem": nrm(ks[10], (L, D_MODEL, D_MODEL), D_MODEL),
        "w_v_mem": nrm(ks[11], (L, D_MODEL, D_MODEL), D_MODEL),
        "w_o_mem": nrm(ks[12], (L, D_MODEL, D_MODEL), D_MODEL),
        "norm_ffn": gain(ks[13], (L, D_MODEL)),
        "w_group": nrm(ks[14], (L, D_MODEL, N_GROUPS), D_MODEL),
        "w_expert": nrm(ks[15], (L, D_MODEL, N_EXPERTS), D_MODEL),
        "w_gate": nrm(ks[16], (L, N_EXPERTS, D_MODEL, EXPERT_FF), D_MODEL),
        "w_up": nrm(ks[17], (L, N_EXPERTS, D_MODEL, EXPERT_FF), D_MODEL),
        "w_down": nrm(ks[18], (L, N_EXPERTS, EXPERT_FF, D_MODEL), EXPERT_FF),
        "norm_final": gain(ks[19], (D_MODEL,)),
    }


def reference(x, mem, norm_mix, w_in, w_pool, pool_scale, w_out, norm_cross, norm_mem,
              w_q_mem, w_k_mem, w_v_mem, w_o_mem, norm_ffn, w_group, w_expert,
              w_gate, w_up, w_down, norm_final):
    B, S, D = x.shape
    h = x
    for layer in range(DEPTH):
        xn = rms_norm(h, norm_mix[layer])
        proj = jnp.einsum('bsd,de->bse', xn, w_in[layer])
        u_pool = proj[..., :POOL_WIDTH]
        q, k, v = jnp.split(proj[..., POOL_WIDTH:], 3, axis=-1)
        to_heads = lambda t: jnp.transpose(t.reshape(B, S, SB_HEADS, SB_HEAD_DIM), (0, 2, 1, 3))
        pool_out = causal_multiscale_pool(u_pool, w_pool[layer], pool_scale[layer])
        attn_out = stick_breaking_attention(to_heads(q), to_heads(k), to_heads(v))
        attn_out = jnp.transpose(attn_out, (0, 2, 1, 3)).reshape(B, S, ATTN_WIDTH)
        mixed = jnp.concatenate([pool_out, attn_out], axis=-1)
        h = h + jnp.einsum('bse,ed->bsd', mixed, w_out[layer])
        h = h + memory_cross_attention(rms_norm(h, norm_cross[layer]), rms_norm(mem, norm_mem[layer]),
                                       w_q_mem[layer], w_k_mem[layer], w_v_mem[layer], w_o_mem[layer])
        h = h + hierarchical_moe(rms_norm(h, norm_ffn[layer]), w_group[layer], w_expert[layer],
                                 w_gate[layer], w_up[layer], w_down[layer])
    return rms_norm(h, norm_final)
```

```python
import functools
import math

import jax
import jax.numpy as jnp
from jax import lax
from jax.experimental import pallas as pl
from jax.experimental.pallas import tpu as pltpu

RMS_EPS = 1e-6
POOL_WINDOWS = (2, 4, 8, 16)
SB_HEAD_DIM = 64
MEM_HEADS = 4
N_GROUPS = 4
EXPERTS_PER_GROUP = 4
N_EXPERTS = N_GROUPS * EXPERTS_PER_GROUP

LANES_V7X = 128
MXU_DIM_V7X = 256
VMEM_LIMIT_BYTES = 56 * 1024 * 1024

POOL_HALO = 16
ROUTER_LANES = LANES_V7X


def _rms(x, gain):
    ms = jnp.mean(x * x, axis=-1, keepdims=True)
    return x * lax.rsqrt(ms + RMS_EPS) * gain


def _dot(a, b):
    return jnp.dot(a, b, preferred_element_type=jnp.float32)


def _dot_nt(a, b):
    return lax.dot_general(a, b, (((1,), (1,)), ((), ())), preferred_element_type=jnp.float32)


def _in_proj_kernel(x_ref, g_ref, w_ref, u_ref, qkv_ref, *, pool_width, attn_width, q_scale):
    xn = _rms(x_ref[...], g_ref[...]).astype(jnp.bfloat16)
    proj = _dot(xn, w_ref[...])
    u_ref[...] = proj[:, :pool_width]
    q = proj[:, pool_width:pool_width + attn_width] * q_scale
    qkv_ref[:, :attn_width] = q.astype(jnp.bfloat16)
    qkv_ref[:, attn_width:] = proj[:, pool_width + attn_width:].astype(jnp.bfloat16)


def _in_proj(x2, gain, w_in_bf16, *, pool_width, attn_width, tm):
    n, d = x2.shape
    in_width = w_in_bf16.shape[1]
    kern = functools.partial(_in_proj_kernel, pool_width=pool_width, attn_width=attn_width,
                             q_scale=1.0 / math.sqrt(SB_HEAD_DIM))
    return pl.pallas_call(
        kern,
        grid=(n // tm,),
        in_specs=[
            pl.BlockSpec((tm, d), lambda i: (i, 0)),
            pl.BlockSpec((1, d), lambda i: (0, 0)),
            pl.BlockSpec((d, in_width), lambda i: (0, 0)),
        ],
        out_specs=[
            pl.BlockSpec((tm, pool_width), lambda i: (i, 0)),
            pl.BlockSpec((tm, 3 * attn_width), lambda i: (i, 0)),
        ],
        out_shape=[
            jax.ShapeDtypeStruct((n, pool_width), jnp.float32),
            jax.ShapeDtypeStruct((n, 3 * attn_width), jnp.bfloat16),
        ],
        compiler_params=pltpu.CompilerParams(
            dimension_semantics=("parallel",), vmem_limit_bytes=VMEM_LIMIT_BYTES),
        name="in_proj",
    )(x2, gain, w_in_bf16)


def _softplus(z):
    return jnp.maximum(z, 0.0) + jnp.log(1.0 + jnp.exp(-jnp.abs(z)))


def _sb_attn_kernel(q_ref, k_ref, v_ref, tri_ref, o_ref, *, tq, tk):
    qi = pl.program_id(2)
    lane = lax.broadcasted_iota(jnp.int32, (tq, LANES_V7X), 1)
    head0 = lane < SB_HEAD_DIM
    q2 = q_ref[...]
    zero = jnp.zeros_like(q2)
    q_heads = (jnp.where(head0, q2, zero), jnp.where(head0, zero, q2))
    tri = tri_ref[...]

    def block(j, carry, masked):
        start = pl.multiple_of(j * tk, tk)
        kb = k_ref[pl.ds(start, tk), :]
        vb = v_ref[pl.ds(start, tk), :]
        if masked:
            row = lax.broadcasted_iota(jnp.int32, (tq, tk), 0)
            col = lax.broadcasted_iota(jnp.int32, (tq, tk), 1)
            causal = col < row
        new = []
        for h in range(2):
            c, acc = carry[h]
            z = _dot_nt(q_heads[h], kb)
            sp = _softplus(z)
            if masked:
                sp = jnp.where(causal, sp, 0.0)
            cum = _dot(sp.astype(jnp.bfloat16), tri)
            w = jnp.exp(z - cum - c)
            if masked:
                w = jnp.where(causal, w, 0.0)
            acc = acc + _dot(w.astype(jnp.bfloat16), vb)
            c = c + jnp.sum(sp, axis=1, keepdims=True)
            new.append((c, acc))
        return tuple(new)

    init = tuple((jnp.zeros((tq, 1), jnp.float32), jnp.zeros((tq, LANES_V7X), jnp.float32))
                 for _ in range(2))
    carry = block(qi, init, True)

    def body(step, carry):
        return block(qi - 1 - step, carry, False)

    carry = lax.fori_loop(0, qi, body, carry)
    o_ref[...] = jnp.where(head0, carry[0][1], carry[1][1]).astype(o_ref.dtype)


def _sb_attention(qkv3, tri, *, attn_width, tq):
    b, s, _ = qkv3.shape
    n_pairs = attn_width // LANES_V7X
    kern = functools.partial(_sb_attn_kernel, tq=tq, tk=tq)
    return pl.pallas_call(
        kern,
        grid=(b, n_pairs, s // tq),
        in_specs=[
            pl.BlockSpec((None, tq, LANES_V7X), lambda bi, hp, qi: (bi, qi, hp)),
            pl.BlockSpec((None, s, LANES_V7X), lambda bi, hp, qi: (bi, 0, n_pairs + hp)),
            pl.BlockSpec((None, s, LANES_V7X), lambda bi, hp, qi: (bi, 0, 2 * n_pairs + hp)),
            pl.BlockSpec((tq, tq), lambda bi, hp, qi: (0, 0)),
        ],
        out_specs=pl.BlockSpec((None, tq, LANES_V7X), lambda bi, hp, qi: (bi, qi, hp)),
        out_shape=jax.ShapeDtypeStruct((b, s, attn_width), jnp.bfloat16),
        compiler_params=pltpu.CompilerParams(
            dimension_semantics=("parallel", "parallel", "parallel"),
            vmem_limit_bytes=VMEM_LIMIT_BYTES),
        name="sb_attention",
    )(qkv3, qkv3, qkv3, tri)


def _mem_kv_kernel(m_ref, g_ref, wk_ref, wv_ref, k_ref, v_ref):
    mn = _rms(m_ref[...], g_ref[...]).astype(jnp.bfloat16)
    k_ref[...] = _dot(mn, wk_ref[...]).astype(jnp.bfloat16)
    v_ref[...] = _dot(mn, wv_ref[...]).astype(jnp.bfloat16)


def _mem_kv(mem2, gain, wk, wv, *, tm):
    n, d = mem2.shape
    row = pl.BlockSpec((tm, d), lambda i: (i, 0))
    full = pl.BlockSpec((d, d), lambda i: (0, 0))
    return pl.pallas_call(
        _mem_kv_kernel,
        grid=(n // tm,),
        in_specs=[row, pl.BlockSpec((1, d), lambda i: (0, 0)), full, full],
        out_specs=[row, row],
        out_shape=[jax.ShapeDtypeStruct((n, d), jnp.bfloat16)] * 2,
        compiler_params=pltpu.CompilerParams(
            dimension_semantics=("parallel",), vmem_limit_bytes=VMEM_LIMIT_BYTES),
        name="mem_kv",
    )(mem2, gain, wk, wv)


def _mix_cross_kernel(x_ref, u_ref, halo_ref, a_ref, wp_ref, ps_ref, wo_ref, gc_ref,
                      wq_ref, km_ref, vm_ref, wom_ref, h_ref, *, tm, tiles_per_seq):
    i = pl.program_id(0)
    pool_width = u_ref.shape[1]
    gw = pool_width // len(POOL_WINDOWS)
    tile_in_seq = i % tiles_per_seq
    first = tile_in_seq == 0
    pos = tile_in_seq * tm + lax.broadcasted_iota(jnp.int32, (tm, 1), 0)

    halo = jnp.where(first, 0.0, halo_ref[...])
    u = u_ref[...]
    mixed = x_ref[...]
    for g, w in enumerate(POOL_WINDOWS):
        ug = u[:, g * gw:(g + 1) * gw]
        ext = jnp.concatenate([halo[:, g * gw:(g + 1) * gw], ug], axis=0)
        shift = 1
        while shift < w:
            ext = ext + pltpu.roll(ext, shift, 0)
            shift *= 2
        win = ext[POOL_HALO:, :]
        inv_count = 1.0 / jnp.minimum(pos + 1, w).astype(jnp.float32)
        pooled = win * inv_count - ug
        pg = _dot(pooled.astype(jnp.bfloat16), wp_ref[g]) * ps_ref[:, g * gw:(g + 1) * gw]
        mixed = mixed + _dot(pg.astype(jnp.bfloat16), wo_ref[g * gw:(g + 1) * gw, :])
    h1 = mixed + _dot(a_ref[...], wo_ref[pool_width:, :])

    hn = _rms(h1, gc_ref[...]).astype(jnp.bfloat16)
    d = h1.shape[1]
    hd = d // MEM_HEADS
    q = (_dot(hn, wq_ref[...]) * (1.0 / math.sqrt(hd))).astype(jnp.bfloat16)
    outs = []
    for hh in range(MEM_HEADS):
        sl = slice(hh * hd, (hh + 1) * hd)
        s = _dot_nt(q[:, sl], km_ref[:, sl])
        e = jnp.exp(s - jnp.max(s, axis=-1, keepdims=True))
        p = e * (1.0 / jnp.sum(e, axis=-1, keepdims=True))
        outs.append(_dot(p.astype(jnp.bfloat16), vm_ref[:, sl]))
    o = jnp.concatenate(outs, axis=-1).astype(jnp.bfloat16)
    h_ref[...] = h1 + _dot(o, wom_ref[...])


def _mix_cross(x2, u, attn, w_pool, pool_scale, w_out, g_cross, wq, kmem, vmem, wom, *, tm, seq, mem_len):
    n, d = x2.shape
    pool_width = u.shape[1]
    attn_width = attn.shape[1]
    tiles_per_seq = seq // tm
    halo_blocks = tm // POOL_HALO
    kern = functools.partial(_mix_cross_kernel, tm=tm, tiles_per_seq=tiles_per_seq)
    const2 = lambda i: (0, 0)
    mem_map = lambda i: (i // tiles_per_seq, 0)
    return pl.pallas_call(
        kern,
        grid=(n // tm,),
        in_specs=[
            pl.BlockSpec((tm, d), lambda i: (i, 0)),
            pl.BlockSpec((tm, pool_width), lambda i: (i, 0)),
            pl.BlockSpec((POOL_HALO, pool_width), lambda i: (jnp.maximum(i * halo_blocks - 1, 0), 0)),
            pl.BlockSpec((tm, attn_width), lambda i: (i, 0)),
            pl.BlockSpec(w_pool.shape, lambda i: (0, 0, 0)),
            pl.BlockSpec((1, pool_width), const2),
            pl.BlockSpec(w_out.shape, const2),
            pl.BlockSpec((1, d), const2),
            pl.BlockSpec((d, d), const2),
            pl.BlockSpec((mem_len, d), mem_map),
            pl.BlockSpec((mem_len, d), mem_map),
            pl.BlockSpec((d, d), const2),
        ],
        out_specs=pl.BlockSpec((tm, d), lambda i: (i, 0)),
        out_shape=jax.ShapeDtypeStruct((n, d), jnp.float32),
        compiler_params=pltpu.CompilerParams(
            dimension_semantics=("parallel",), vmem_limit_bytes=VMEM_LIMIT_BYTES),
        name="mix_cross",
    )(x2, u, u, attn, w_pool, pool_scale, w_out, g_cross, wq, kmem, vmem, wom)


def _route(logits):
    neg = jnp.float32(-jnp.inf)
    big = jnp.float32(ROUTER_LANES)
    lane = lax.broadcasted_iota(jnp.int32, logits.shape, 1).astype(jnp.float32)

    def first_argmax(mask, mx):
        return jnp.min(jnp.where(mask & (logits == mx), lane, big), axis=-1, keepdims=True)

    gmask = lane < N_GROUPS
    gmax = jnp.max(jnp.where(gmask, logits, neg), axis=-1, keepdims=True)
    gsum = jnp.sum(jnp.where(gmask, jnp.exp(logits - gmax), 0.0), axis=-1, keepdims=True)
    g_gate = 1.0 / gsum
    g_idx = first_argmax(gmask, gmax)

    lo = N_GROUPS + EXPERTS_PER_GROUP * g_idx
    emask = (lane >= lo) & (lane < lo + EXPERTS_PER_GROUP)
    m1 = jnp.max(jnp.where(emask, logits, neg), axis=-1, keepdims=True)
    i1 = first_argmax(emask, m1)
    mask2 = emask & (lane != i1)
    m2 = jnp.max(jnp.where(mask2, logits, neg), axis=-1, keepdims=True)
    i2 = first_argmax(mask2, m2)
    esum = jnp.sum(jnp.where(emask, jnp.exp(logits - m1), 0.0), axis=-1, keepdims=True)
    p1 = 1.0 / esum
    p2 = jnp.exp(m2 - m1) / esum
    tot = p1 + p2
    w1 = g_gate * (p1 / tot)
    w2 = g_gate * (p2 / tot)
    return jnp.where(lane == i1, w1, 0.0) + jnp.where(lane == i2, w2, 0.0)


def _moe_kernel(h_ref, gf_ref, wr_hi_ref, wr_lo_ref, wg_ref, wu_ref, wd_ref, gl_ref, o_ref,
                xn_ref, comb_ref, acc_ref):
    e = pl.program_id(1)

    @pl.when(e == 0)
    def _():
        xn = _rms(h_ref[...], gf_ref[...])
        x_hi = xn.astype(jnp.bfloat16)
        x_lo = (xn - x_hi.astype(jnp.float32)).astype(jnp.bfloat16)
        logits = (_dot(x_hi, wr_hi_ref[...]) + _dot(x_hi, wr_lo_ref[...]) + _dot(x_lo, wr_hi_ref[...]))
        xn_ref[...] = x_hi
        comb_ref[...] = _route(logits)
        acc_ref[...] = jnp.zeros_like(acc_ref)

    x = xn_ref[...]
    gate = _dot(x, wg_ref[...])
    up = _dot(x, wu_ref[...])
    hmid = (gate * (1.0 / (1.0 + jnp.exp(-gate)))) * up
    y = _dot(hmid.astype(jnp.bfloat16), wd_ref[...])
    lane = lax.broadcasted_iota(jnp.int32, comb_ref.shape, 1)
    ce = jnp.sum(jnp.where(lane == N_GROUPS + e, comb_ref[...], 0.0), axis=-1, keepdims=True)
    acc_ref[...] += ce * y

    @pl.when(e == N_EXPERTS - 1)
    def _():
        o_ref[...] = _rms(h_ref[...] + acc_ref[...], gl_ref[...])


def _moe_final(h, g_ffn, wr_hi, wr_lo, wg, wu, wd, g_final, *, tm):
    n, d = h.shape
    ff = wg.shape[2]
    const2 = lambda i, e: (0, 0)
    return pl.pallas_call(
        _moe_kernel,
        grid=(n // tm, N_EXPERTS),
        in_specs=[
            pl.BlockSpec((tm, d), lambda i, e: (i, 0)),
            pl.BlockSpec((1, d), const2),
            pl.BlockSpec((d, ROUTER_LANES), const2),
            pl.BlockSpec((d, ROUTER_LANES), const2),
            pl.BlockSpec((None, d, ff), lambda i, e: (e, 0, 0)),
            pl.BlockSpec((None, d, ff), lambda i, e: (e, 0, 0)),
            pl.BlockSpec((None, ff, d), lambda i, e: (e, 0, 0)),
            pl.BlockSpec((1, d), const2),
        ],
        out_specs=pl.BlockSpec((tm, d), lambda i, e: (i, 0)),
        out_shape=jax.ShapeDtypeStruct((n, d), jnp.float32),
        scratch_shapes=[
            pltpu.VMEM((tm, d), jnp.bfloat16),
            pltpu.VMEM((tm, ROUTER_LANES), jnp.float32),
            pltpu.VMEM((tm, d), jnp.float32),
        ],
        compiler_params=pltpu.CompilerParams(
            dimension_semantics=("parallel", "arbitrary"), vmem_limit_bytes=VMEM_LIMIT_BYTES),
        name="moe_final",
    )(h, g_ffn, wr_hi, wr_lo, wg, wu, wd, g_final)


def _tile(n, pref):
    t = min(n, pref)
    assert n % t == 0, (n, t)
    return t


def kernel(x, mem, norm_mix, w_in, w_pool, pool_scale, w_out, norm_cross, norm_mem, w_q_mem, w_k_mem,
           w_v_mem, w_o_mem, norm_ffn, w_group, w_expert, w_gate, w_up, w_down, norm_final):
    b, s, d = x.shape
    mem_len = mem.shape[1]
    depth = norm_mix.shape[0]
    pool_width = pool_scale.shape[1]
    attn_width = w_out.shape[1] - pool_width
    bf = jnp.bfloat16
    n = b * s

    tq = _tile(s, MXU_DIM_V7X)
    tri = (lax.broadcasted_iota(jnp.int32, (tq, tq), 0)
           >= lax.broadcasted_iota(jnp.int32, (tq, tq), 1)).astype(bf)

    assert depth == 1, "single-layer problem: the final RMSNorm is fused into the MoE kernel"
    l = 0
    h = x.reshape(n, d)
    u, qkv = _in_proj(h, norm_mix[l][None], w_in[l].astype(bf),
                      pool_width=pool_width, attn_width=attn_width, tm=_tile(n, 512))
    attn = _sb_attention(qkv.reshape(b, s, 3 * attn_width), tri, attn_width=attn_width, tq=tq)
    kmem, vmem = _mem_kv(mem.reshape(b * mem_len, d), norm_mem[l][None],
                         w_k_mem[l].astype(bf), w_v_mem[l].astype(bf), tm=mem_len)
    h = _mix_cross(h, u, attn.reshape(n, attn_width), w_pool[l].astype(bf), pool_scale[l][None],
                   w_out[l].astype(bf), norm_cross[l][None], w_q_mem[l].astype(bf), kmem, vmem,
                   w_o_mem[l].astype(bf), tm=_tile(s, 512), seq=s, mem_len=mem_len)
    w_router = jnp.concatenate([w_group[l], w_expert[l]], axis=1)
    w_router = jnp.pad(w_router, ((0, 0), (0, ROUTER_LANES - w_router.shape[1])))
    wr_hi = w_router.astype(bf)
    wr_lo = (w_router - wr_hi.astype(jnp.float32)).astype(bf)
    out = _moe_final(h, norm_ffn[l][None], wr_hi, wr_lo, w_gate[l].astype(bf), w_up[l].astype(bf),
                     w_down[l].astype(bf), norm_final[None], tm=_tile(n, 1024))
    return out.reshape(b, s, d)
```

```python
import functools
import math

import jax
import jax.numpy as jnp
from jax import lax
from jax.experimental import pallas as pl
from jax.experimental.pallas import tpu as pltpu

RMS_EPS = 1e-6
POOL_WINDOWS = (2, 4, 8, 16)
SB_HEAD_DIM = 64
MEM_HEADS = 4
N_GROUPS = 4
EXPERTS_PER_GROUP = 4
N_EXPERTS = N_GROUPS * EXPERTS_PER_GROUP

LANES_V7X = 128
MXU_DIM_V7X = 256
VMEM_LIMIT_BYTES = 56 * 1024 * 1024

UNDERFLOW_EXPONENT = 104.0
POOL_HALO = 16
ROUTER_LANES = LANES_V7X


def _rms(x, gain):
    ms = jnp.mean(x * x, axis=-1, keepdims=True)
    return x * lax.rsqrt(ms + RMS_EPS) * gain


def _dot(a, b):
    return jnp.dot(a, b, preferred_element_type=jnp.float32)


def _dot_nt(a, b):
    return lax.dot_general(a, b, (((1,), (1,)), ((), ())), preferred_element_type=jnp.float32)


def _in_proj_kernel(x_ref, g_ref, w_ref, u_ref, qkv_ref, *, pool_width, attn_width, q_scale):
    xn = _rms(x_ref[...], g_ref[...]).astype(jnp.bfloat16)
    proj = _dot(xn, w_ref[...])
    u_ref[...] = proj[:, :pool_width]
    q = proj[:, pool_width:pool_width + attn_width] * q_scale
    qkv_ref[:, :attn_width] = q.astype(jnp.bfloat16)
    qkv_ref[:, attn_width:] = proj[:, pool_width + attn_width:].astype(jnp.bfloat16)


def _in_proj(x2, gain, w_in_bf16, *, pool_width, attn_width, tm):
    n, d = x2.shape
    in_width = w_in_bf16.shape[1]
    kern = functools.partial(_in_proj_kernel, pool_width=pool_width, attn_width=attn_width,
                             q_scale=1.0 / math.sqrt(SB_HEAD_DIM))
    return pl.pallas_call(
        kern,
        grid=(n // tm,),
        in_specs=[
            pl.BlockSpec((tm, d), lambda i: (i, 0)),
            pl.BlockSpec((1, d), lambda i: (0, 0)),
            pl.BlockSpec((d, in_width), lambda i: (0, 0)),
        ],
        out_specs=[
            pl.BlockSpec((tm, pool_width), lambda i: (i, 0)),
            pl.BlockSpec((tm, 3 * attn_width), lambda i: (i, 0)),
        ],
        out_shape=[
            jax.ShapeDtypeStruct((n, pool_width), jnp.float32),
            jax.ShapeDtypeStruct((n, 3 * attn_width), jnp.bfloat16),
        ],
        compiler_params=pltpu.CompilerParams(
            dimension_semantics=("parallel",), vmem_limit_bytes=VMEM_LIMIT_BYTES),
        name="in_proj",
    )(x2, gain, w_in_bf16)


def _softplus(z):
    return jnp.maximum(z, 0.0) + jnp.log(1.0 + jnp.exp(-jnp.abs(z)))


def _sb_attn_kernel(q_ref, k_ref, v_ref, tri_ref, o_ref, *, tq, tk):
    qi = pl.program_id(2)
    lane = lax.broadcasted_iota(jnp.int32, (tq, LANES_V7X), 1)
    head0 = lane < SB_HEAD_DIM
    q2 = q_ref[...]
    zero = jnp.zeros_like(q2)
    q_heads = (jnp.where(head0, q2, zero), jnp.where(head0, zero, q2))
    tri = tri_ref[...]

    def block(j, carry, masked):
        start = pl.multiple_of(j * tk, tk)
        kb = k_ref[pl.ds(start, tk), :]
        vb = v_ref[pl.ds(start, tk), :]
        if masked:
            row = lax.broadcasted_iota(jnp.int32, (tq, tk), 0)
            col = lax.broadcasted_iota(jnp.int32, (tq, tk), 1)
            causal = col < row
        new = []
        for h in range(2):
            c, acc = carry[h]
            z = _dot_nt(q_heads[h], kb)
            sp = _softplus(z)
            if masked:
                sp = jnp.where(causal, sp, 0.0)
            cum = _dot(sp.astype(jnp.bfloat16), tri)
            w = jnp.exp(jnp.minimum(z - cum, 0.0) - c)
            if masked:
                w = jnp.where(causal, w, 0.0)
            acc = acc + _dot(w.astype(jnp.bfloat16), vb)
            c = c + jnp.sum(sp, axis=1, keepdims=True)
            new.append((c, acc))
        return tuple(new)

    init = tuple((jnp.zeros((tq, 1), jnp.float32), jnp.zeros((tq, LANES_V7X), jnp.float32))
                 for _ in range(2))
    carry = block(qi, init, True)

    def live(carry):
        cmin = jnp.min(jnp.minimum(carry[0][0], carry[1][0]), axis=0, keepdims=True)
        return (cmin[0, 0] < UNDERFLOW_EXPONENT).astype(jnp.int32)

    def cond(state):
        step, alive, _ = state
        return jnp.logical_and(step < qi, alive > 0)

    def body(state):
        step, _, carry = state
        carry = block(qi - 1 - step, carry, False)
        return step + 1, live(carry), carry

    _, _, carry = lax.while_loop(cond, body, (jnp.int32(0), live(carry), carry))
    o_ref[...] = jnp.where(head0, carry[0][1], carry[1][1]).astype(o_ref.dtype)


def _sb_attention(qkv3, tri, *, attn_width, tq):
    b, s, _ = qkv3.shape
    n_pairs = attn_width // LANES_V7X
    kern = functools.partial(_sb_attn_kernel, tq=tq, tk=tq)
    return pl.pallas_call(
        kern,
        grid=(b, n_pairs, s // tq),
        in_specs=[
            pl.BlockSpec((None, tq, LANES_V7X), lambda bi, hp, qi: (bi, qi, hp)),
            pl.BlockSpec((None, s, LANES_V7X), lambda bi, hp, qi: (bi, 0, n_pairs + hp)),
            pl.BlockSpec((None, s, LANES_V7X), lambda bi, hp, qi: (bi, 0, 2 * n_pairs + hp)),
            pl.BlockSpec((tq, tq), lambda bi, hp, qi: (0, 0)),
        ],
        out_specs=pl.BlockSpec((None, tq, LANES_V7X), lambda bi, hp, qi: (bi, qi, hp)),
        out_shape=jax.ShapeDtypeStruct((b, s, attn_width), jnp.bfloat16),
        compiler_params=pltpu.CompilerParams(
            dimension_semantics=("parallel", "parallel", "parallel"),
            vmem_limit_bytes=VMEM_LIMIT_BYTES),
        name="sb_attention",
    )(qkv3, qkv3, qkv3, tri)


def _mem_kv_kernel(m_ref, g_ref, wk_ref, wv_ref, k_ref, v_ref):
    mn = _rms(m_ref[...], g_ref[...]).astype(jnp.bfloat16)
    k_ref[...] = _dot(mn, wk_ref[...]).astype(jnp.bfloat16)
    v_ref[...] = _dot(mn, wv_ref[...]).astype(jnp.bfloat16)


def _mem_kv(mem2, gain, wk, wv, *, tm):
    n, d = mem2.shape
    row = pl.BlockSpec((tm, d), lambda i: (i, 0))
    full = pl.BlockSpec((d, d), lambda i: (0, 0))
    return pl.pallas_call(
        _mem_kv_kernel,
        grid=(n // tm,),
        in_specs=[row, pl.BlockSpec((1, d), lambda i: (0, 0)), full, full],
        out_specs=[row, row],
        out_shape=[jax.ShapeDtypeStruct((n, d), jnp.bfloat16)] * 2,
        compiler_params=pltpu.CompilerParams(
            dimension_semantics=("parallel",), vmem_limit_bytes=VMEM_LIMIT_BYTES),
        name="mem_kv",
    )(mem2, gain, wk, wv)


def _mix_cross_kernel(x_ref, u_ref, halo_ref, a_ref, wp_ref, ps_ref, wo_ref, gc_ref,
                      wq_ref, km_ref, vm_ref, wom_ref, h_ref, *, tm, tiles_per_seq):
    i = pl.program_id(0)
    pool_width = u_ref.shape[1]
    gw = pool_width // len(POOL_WINDOWS)
    tile_in_seq = i % tiles_per_seq
    first = tile_in_seq == 0
    pos = tile_in_seq * tm + lax.broadcasted_iota(jnp.int32, (tm, 1), 0)

    halo = jnp.where(first, 0.0, halo_ref[...])
    u = u_ref[...]
    mixed = x_ref[...]
    for g, w in enumerate(POOL_WINDOWS):
        ug = u[:, g * gw:(g + 1) * gw]
        ext = jnp.concatenate([halo[:, g * gw:(g + 1) * gw], ug], axis=0)
        shift = 1
        while shift < w:
            ext = ext + pltpu.roll(ext, shift, 0)
            shift *= 2
        win = ext[POOL_HALO:, :]
        inv_count = 1.0 / jnp.minimum(pos + 1, w).astype(jnp.float32)
        pooled = win * inv_count - ug
        pg = _dot(pooled.astype(jnp.bfloat16), wp_ref[g]) * ps_ref[:, g * gw:(g + 1) * gw]
        mixed = mixed + _dot(pg.astype(jnp.bfloat16), wo_ref[g * gw:(g + 1) * gw, :])
    h1 = mixed + _dot(a_ref[...], wo_ref[pool_width:, :])

    hn = _rms(h1, gc_ref[...]).astype(jnp.bfloat16)
    d = h1.shape[1]
    hd = d // MEM_HEADS
    q = (_dot(hn, wq_ref[...]) * (1.0 / math.sqrt(hd))).astype(jnp.bfloat16)
    outs = []
    for hh in range(MEM_HEADS):
        sl = slice(hh * hd, (hh + 1) * hd)
        s = _dot_nt(q[:, sl], km_ref[:, sl])
        e = jnp.exp(s - jnp.max(s, axis=-1, keepdims=True))
        p = e * (1.0 / jnp.sum(e, axis=-1, keepdims=True))
        outs.append(_dot(p.astype(jnp.bfloat16), vm_ref[:, sl]))
    o = jnp.concatenate(outs, axis=-1).astype(jnp.bfloat16)
    h_ref[...] = h1 + _dot(o, wom_ref[...])


def _mix_cross(x2, u, attn, w_pool, pool_scale, w_out, g_cross, wq, kmem, vmem, wom, *, tm, seq, mem_len):
    n, d = x2.shape
    pool_width = u.shape[1]
    attn_width = attn.shape[1]
    tiles_per_seq = seq // tm
    halo_blocks = tm // POOL_HALO
    kern = functools.partial(_mix_cross_kernel, tm=tm, tiles_per_seq=tiles_per_seq)
    const2 = lambda i: (0, 0)
    mem_map = lambda i: (i // tiles_per_seq, 0)
    return pl.pallas_call(
        kern,
        grid=(n // tm,),
        in_specs=[
            pl.BlockSpec((tm, d), lambda i: (i, 0)),
            pl.BlockSpec((tm, pool_width), lambda i: (i, 0)),
            pl.BlockSpec((POOL_HALO, pool_width), lambda i: (jnp.maximum(i * halo_blocks - 1, 0), 0)),
            pl.BlockSpec((tm, attn_width), lambda i: (i, 0)),
            pl.BlockSpec(w_pool.shape, lambda i: (0, 0, 0)),
            pl.BlockSpec((1, pool_width), const2),
            pl.BlockSpec(w_out.shape, const2),
            pl.BlockSpec((1, d), const2),
            pl.BlockSpec((d, d), const2),
            pl.BlockSpec((mem_len, d), mem_map),
            pl.BlockSpec((mem_len, d), mem_map),
            pl.BlockSpec((d, d), const2),
        ],
        out_specs=pl.BlockSpec((tm, d), lambda i: (i, 0)),
        out_shape=jax.ShapeDtypeStruct((n, d), jnp.float32),
        compiler_params=pltpu.CompilerParams(
            dimension_semantics=("parallel",), vmem_limit_bytes=VMEM_LIMIT_BYTES),
        name="mix_cross",
    )(x2, u, u, attn, w_pool, pool_scale, w_out, g_cross, wq, kmem, vmem, wom)


def _route(logits):
    neg = jnp.float32(-jnp.inf)
    big = jnp.float32(ROUTER_LANES)
    lane = lax.broadcasted_iota(jnp.int32, logits.shape, 1).astype(jnp.float32)

    def first_argmax(mask, mx):
        return jnp.min(jnp.where(mask & (logits == mx), lane, big), axis=-1, keepdims=True)

    gmask = lane < N_GROUPS
    gmax = jnp.max(jnp.where(gmask, logits, neg), axis=-1, keepdims=True)
    gsum = jnp.sum(jnp.where(gmask, jnp.exp(logits - gmax), 0.0), axis=-1, keepdims=True)
    g_gate = 1.0 / gsum
    g_idx = first_argmax(gmask, gmax)

    lo = N_GROUPS + EXPERTS_PER_GROUP * g_idx
    emask = (lane >= lo) & (lane < lo + EXPERTS_PER_GROUP)
    m1 = jnp.max(jnp.where(emask, logits, neg), axis=-1, keepdims=True)
    i1 = first_argmax(emask, m1)
    mask2 = emask & (lane != i1)
    m2 = jnp.max(jnp.where(mask2, logits, neg), axis=-1, keepdims=True)
    i2 = first_argmax(mask2, m2)
    esum = jnp.sum(jnp.where(emask, jnp.exp(logits - m1), 0.0), axis=-1, keepdims=True)
    p1 = 1.0 / esum
    p2 = jnp.exp(m2 - m1) / esum
    tot = p1 + p2
    w1 = g_gate * (p1 / tot)
    w2 = g_gate * (p2 / tot)
    return jnp.where(lane == i1, w1, 0.0) + jnp.where(lane == i2, w2, 0.0)


def _moe_kernel(h_ref, gf_ref, wr_hi_ref, wr_lo_ref, wg_ref, wu_ref, wd_ref, gl_ref, o_ref,
                xn_ref, comb_ref, acc_ref):
    e = pl.program_id(1)

    @pl.when(e == 0)
    def _():
        xn = _rms(h_ref[...], gf_ref[...])
        x_hi = xn.astype(jnp.bfloat16)
        x_lo = (xn - x_hi.astype(jnp.float32)).astype(jnp.bfloat16)
        logits = (_dot(x_hi, wr_hi_ref[...]) + _dot(x_hi, wr_lo_ref[...]) + _dot(x_lo, wr_hi_ref[...]))
        xn_ref[...] = x_hi
        comb_ref[...] = _route(logits)
        acc_ref[...] = jnp.zeros_like(acc_ref)

    x = xn_ref[...]
    gate = _dot(x, wg_ref[...])
    up = _dot(x, wu_ref[...])
    hmid = (gate * (1.0 / (1.0 + jnp.exp(-gate)))) * up
    y = _dot(hmid.astype(jnp.bfloat16), wd_ref[...])
    lane = lax.broadcasted_iota(jnp.int32, comb_ref.shape, 1)
    ce = jnp.sum(jnp.where(lane == N_GROUPS + e, comb_ref[...], 0.0), axis=-1, keepdims=True)
    acc_ref[...] += ce * y

    @pl.when(e == N_EXPERTS - 1)
    def _():
        o_ref[...] = _rms(h_ref[...] + acc_ref[...], gl_ref[...])


def _moe_final(h, g_ffn, wr_hi, wr_lo, wg, wu, wd, g_final, *, tm):
    n, d = h.shape
    ff = wg.shape[2]
    const2 = lambda i, e: (0, 0)
    return pl.pallas_call(
        _moe_kernel,
        grid=(n // tm, N_EXPERTS),
        in_specs=[
            pl.BlockSpec((tm, d), lambda i, e: (i, 0)),
            pl.BlockSpec((1, d), const2),
            pl.BlockSpec((d, ROUTER_LANES), const2),
            pl.BlockSpec((d, ROUTER_LANES), const2),
            pl.BlockSpec((None, d, ff), lambda i, e: (e, 0, 0)),
            pl.BlockSpec((None, d, ff), lambda i, e: (e, 0, 0)),
            pl.BlockSpec((None, ff, d), lambda i, e: (e, 0, 0)),
            pl.BlockSpec((1, d), const2),
        ],
        out_specs=pl.BlockSpec((tm, d), lambda i, e: (i, 0)),
        out_shape=jax.ShapeDtypeStruct((n, d), jnp.float32),
        scratch_shapes=[
            pltpu.VMEM((tm, d), jnp.bfloat16),
            pltpu.VMEM((tm, ROUTER_LANES), jnp.float32),
            pltpu.VMEM((tm, d), jnp.float32),
        ],
        compiler_params=pltpu.CompilerParams(
            dimension_semantics=("parallel", "arbitrary"), vmem_limit_bytes=VMEM_LIMIT_BYTES),
        name="moe_final",
    )(h, g_ffn, wr_hi, wr_lo, wg, wu, wd, g_final)


def _tile(n, pref):
    t = min(n, pref)
    assert n % t == 0, (n, t)
    return t


def kernel(x, mem, norm_mix, w_in, w_pool, pool_scale, w_out, norm_cross, norm_mem, w_q_mem, w_k_mem,
           w_v_mem, w_o_mem, norm_ffn, w_group, w_expert, w_gate, w_up, w_down, norm_final):
    b, s, d = x.shape
    mem_len = mem.shape[1]
    depth = norm_mix.shape[0]
    pool_width = pool_scale.shape[1]
    attn_width = w_out.shape[1] - pool_width
    bf = jnp.bfloat16
    n = b * s

    tq = _tile(s, MXU_DIM_V7X)
    tri = (lax.broadcasted_iota(jnp.int32, (tq, tq), 0)
           >= lax.broadcasted_iota(jnp.int32, (tq, tq), 1)).astype(bf)

    assert depth == 1, "single-layer problem: the final RMSNorm is fused into the MoE kernel"
    l = 0
    h = x.reshape(n, d)
    u, qkv = _in_proj(h, norm_mix[l][None], w_in[l].astype(bf),
                      pool_width=pool_width, attn_width=attn_width, tm=_tile(n, 512))
    attn = _sb_attention(qkv.reshape(b, s, 3 * attn_width), tri, attn_width=attn_width, tq=tq)
    kmem, vmem = _mem_kv(mem.reshape(b * mem_len, d), norm_mem[l][None],
                         w_k_mem[l].astype(bf), w_v_mem[l].astype(bf), tm=mem_len)
    h = _mix_cross(h, u, attn.reshape(n, attn_width), w_pool[l].astype(bf), pool_scale[l][None],
                   w_out[l].astype(bf), norm_cross[l][None], w_q_mem[l].astype(bf), kmem, vmem,
                   w_o_mem[l].astype(bf), tm=_tile(s, 512), seq=s, mem_len=mem_len)
    w_router = jnp.concatenate([w_group[l], w_expert[l]], axis=1)
    w_router = jnp.pad(w_router, ((0, 0), (0, ROUTER_LANES - w_router.shape[1])))
    wr_hi = w_router.astype(bf)
    wr_lo = (w_router - wr_hi.astype(jnp.float32)).astype(bf)
    out = _moe_final(h, norm_ffn[l][None], wr_hi, wr_lo, w_gate[l].astype(bf), w_up[l].astype(bf),
                     w_down[l].astype(bf), norm_final[None], tm=_tile(n, 1024))
    return out.reshape(b, s, d)
```

```python
import functools
import math

import jax
import jax.numpy as jnp
from jax import lax
from jax.experimental import pallas as pl
from jax.experimental.pallas import tpu as pltpu

RMS_EPS = 1e-6
POOL_WINDOWS = (2, 4, 8, 16)
SB_HEAD_DIM = 64
MEM_HEADS = 4
N_GROUPS = 4
EXPERTS_PER_GROUP = 4
N_EXPERTS = N_GROUPS * EXPERTS_PER_GROUP

LANES_V7X = 128
MXU_DIM_V7X = 256
VMEM_LIMIT_BYTES = 56 * 1024 * 1024

UNDERFLOW_EXPONENT = 104.0
POOL_HALO = 16

PAIRS = tuple((a, b) for a in range(EXPERTS_PER_GROUP) for b in range(a + 1, EXPERTS_PER_GROUP))
N_BUCKETS = N_GROUPS * len(PAIRS)
ROUTER_ROWS = 32
PAYLOAD_LANES = LANES_V7X
assert N_GROUPS + N_EXPERTS <= ROUTER_ROWS and N_BUCKETS <= ROUTER_ROWS


def _rms(x, gain):
    ms = jnp.mean(x * x, axis=-1, keepdims=True)
    return x * lax.rsqrt(ms + RMS_EPS) * gain


def _dot(a, b):
    return jnp.dot(a, b, preferred_element_type=jnp.float32)


def _dot_nt(a, b):
    return lax.dot_general(a, b, (((1,), (1,)), ((), ())), preferred_element_type=jnp.float32)


def _in_proj_kernel(x_ref, g_ref, w_ref, u_ref, qkv_ref, *, pool_width, attn_width, q_scale):
    xn = _rms(x_ref[...], g_ref[...]).astype(jnp.bfloat16)
    proj = _dot(xn, w_ref[...])
    u_ref[...] = proj[:, :pool_width]
    q = proj[:, pool_width:pool_width + attn_width] * q_scale
    qkv_ref[:, :attn_width] = q.astype(jnp.bfloat16)
    qkv_ref[:, attn_width:] = proj[:, pool_width + attn_width:].astype(jnp.bfloat16)


def _in_proj(x2, gain, w_in_bf16, *, pool_width, attn_width, tm):
    n, d = x2.shape
    in_width = w_in_bf16.shape[1]
    kern = functools.partial(_in_proj_kernel, pool_width=pool_width, attn_width=attn_width,
                             q_scale=1.0 / math.sqrt(SB_HEAD_DIM))
    return pl.pallas_call(
        kern,
        grid=(n // tm,),
        in_specs=[
            pl.BlockSpec((tm, d), lambda i: (i, 0)),
            pl.BlockSpec((1, d), lambda i: (0, 0)),
            pl.BlockSpec((d, in_width), lambda i: (0, 0)),
        ],
        out_specs=[
            pl.BlockSpec((tm, pool_width), lambda i: (i, 0)),
            pl.BlockSpec((tm, 3 * attn_width), lambda i: (i, 0)),
        ],
        out_shape=[
            jax.ShapeDtypeStruct((n, pool_width), jnp.float32),
            jax.ShapeDtypeStruct((n, 3 * attn_width), jnp.bfloat16),
        ],
        compiler_params=pltpu.CompilerParams(
            dimension_semantics=("parallel",), vmem_limit_bytes=VMEM_LIMIT_BYTES),
        name="in_proj",
    )(x2, gain, w_in_bf16)


def _softplus(z):
    return jnp.maximum(z, 0.0) + jnp.log(1.0 + jnp.exp(-jnp.abs(z)))


def _sb_attn_kernel(q_ref, k_ref, v_ref, tri_ref, o_ref, *, tq, tk):
    qi = pl.program_id(2)
    lane = lax.broadcasted_iota(jnp.int32, (tq, LANES_V7X), 1)
    head0 = lane < SB_HEAD_DIM
    q2 = q_ref[...]
    zero = jnp.zeros_like(q2)
    q_heads = (jnp.where(head0, q2, zero), jnp.where(head0, zero, q2))
    tri = tri_ref[...]

    def block(j, carry, masked):
        start = pl.multiple_of(j * tk, tk)
        kb = k_ref[pl.ds(start, tk), :]
        vb = v_ref[pl.ds(start, tk), :]
        if masked:
            row = lax.broadcasted_iota(jnp.int32, (tq, tk), 0)
            col = lax.broadcasted_iota(jnp.int32, (tq, tk), 1)
            causal = col < row
        new = []
        for h in range(2):
            c, acc = carry[h]
            z = _dot_nt(q_heads[h], kb)
            sp = _softplus(z)
            if masked:
                sp = jnp.where(causal, sp, 0.0)
            cum = _dot(sp.astype(jnp.bfloat16), tri)
            w = jnp.exp(jnp.minimum(z - cum, 0.0) - c)
            if masked:
                w = jnp.where(causal, w, 0.0)
            acc = acc + _dot(w.astype(jnp.bfloat16), vb)
            c = c + jnp.sum(sp, axis=1, keepdims=True)
            new.append((c, acc))
        return tuple(new)

    init = tuple((jnp.zeros((tq, 1), jnp.float32), jnp.zeros((tq, LANES_V7X), jnp.float32))
                 for _ in range(2))
    carry = block(qi, init, True)

    def live(carry):
        cmin = jnp.min(jnp.minimum(carry[0][0], carry[1][0]), axis=0, keepdims=True)
        return (cmin[0, 0] < UNDERFLOW_EXPONENT).astype(jnp.int32)

    def cond(state):
        step, alive, _ = state
        return jnp.logical_and(step < qi, alive > 0)

    def body(state):
        step, _, carry = state
        carry = block(qi - 1 - step, carry, False)
        return step + 1, live(carry), carry

    _, _, carry = lax.while_loop(cond, body, (jnp.int32(0), live(carry), carry))
    o_ref[...] = jnp.where(head0, carry[0][1], carry[1][1]).astype(o_ref.dtype)


def _sb_attention(qkv3, tri, *, attn_width, tq):
    b, s, _ = qkv3.shape
    n_pairs = attn_width // LANES_V7X
    kern = functools.partial(_sb_attn_kernel, tq=tq, tk=tq)
    return pl.pallas_call(
        kern,
        grid=(b, n_pairs, s // tq),
        in_specs=[
            pl.BlockSpec((None, tq, LANES_V7X), lambda bi, hp, qi: (bi, qi, hp)),
            pl.BlockSpec((None, s, LANES_V7X), lambda bi, hp, qi: (bi, 0, n_pairs + hp)),
            pl.BlockSpec((None, s, LANES_V7X), lambda bi, hp, qi: (bi, 0, 2 * n_pairs + hp)),
            pl.BlockSpec((tq, tq), lambda bi, hp, qi: (0, 0)),
        ],
        out_specs=pl.BlockSpec((None, tq, LANES_V7X), lambda bi, hp, qi: (bi, qi, hp)),
        out_shape=jax.ShapeDtypeStruct((b, s, attn_width), jnp.bfloat16),
        compiler_params=pltpu.CompilerParams(
            dimension_semantics=("parallel", "parallel", "parallel"),
            vmem_limit_bytes=VMEM_LIMIT_BYTES),
        name="sb_attention",
    )(qkv3, qkv3, qkv3, tri)


def _mem_kv_kernel(m_ref, g_ref, wk_ref, wv_ref, k_ref, v_ref):
    mn = _rms(m_ref[...], g_ref[...]).astype(jnp.bfloat16)
    k_ref[...] = _dot(mn, wk_ref[...]).astype(jnp.bfloat16)
    v_ref[...] = _dot(mn, wv_ref[...]).astype(jnp.bfloat16)


def _mem_kv(mem2, gain, wk, wv, *, tm):
    n, d = mem2.shape
    row = pl.BlockSpec((tm, d), lambda i: (i, 0))
    full = pl.BlockSpec((d, d), lambda i: (0, 0))
    return pl.pallas_call(
        _mem_kv_kernel,
        grid=(n // tm,),
        in_specs=[row, pl.BlockSpec((1, d), lambda i: (0, 0)), full, full],
        out_specs=[row, row],
        out_shape=[jax.ShapeDtypeStruct((n, d), jnp.bfloat16)] * 2,
        compiler_params=pltpu.CompilerParams(
            dimension_semantics=("parallel",), vmem_limit_bytes=VMEM_LIMIT_BYTES),
        name="mem_kv",
    )(mem2, gain, wk, wv)


def _route_t(logits):
    neg = jnp.float32(-jnp.inf)
    big = jnp.float32(ROUTER_ROWS)
    row = lax.broadcasted_iota(jnp.int32, logits.shape, 0).astype(jnp.float32)

    def col_max(mask):
        return jnp.max(jnp.where(mask, logits, neg), axis=0, keepdims=True)

    def first_argmax(mask, mx):
        return jnp.min(jnp.where(mask & (logits == mx), row, big), axis=0, keepdims=True)

    gmask = row < N_GROUPS
    gmax = col_max(gmask)
    gsum = jnp.sum(jnp.where(gmask, jnp.exp(logits - gmax), 0.0), axis=0, keepdims=True)
    g_gate = 1.0 / gsum
    g_idx = first_argmax(gmask, gmax)

    lo = N_GROUPS + EXPERTS_PER_GROUP * g_idx
    emask = (row >= lo) & (row < lo + EXPERTS_PER_GROUP)
    m1 = col_max(emask)
    i1 = first_argmax(emask, m1)
    mask2 = emask & (row != i1)
    m2 = col_max(mask2)
    i2 = first_argmax(mask2, m2)
    esum = jnp.sum(jnp.where(emask, jnp.exp(logits - m1), 0.0), axis=0, keepdims=True)
    p1 = 1.0 / esum
    p2 = jnp.exp(m2 - m1) / esum
    tot = p1 + p2
    w1 = g_gate * (p1 / tot)
    w2 = g_gate * (p2 / tot)

    first = i1 < i2
    la = jnp.where(first, i1, i2) - lo
    lb = jnp.where(first, i2, i1) - lo
    pair = la * (2 * EXPERTS_PER_GROUP - 1 - la) * 0.5 + (lb - la - 1.0)
    bucket = g_idx * len(PAIRS) + pair
    return bucket, jnp.where(first, w1, w2), jnp.where(first, w2, w1)


def _mix_cross_kernel(x_ref, u_ref, halo_ref, a_ref, wp_ref, ps_ref, wo_ref, gc_ref,
                      wq_ref, km_ref, vm_ref, wom_ref, gf_ref, wr_hi_ref, wr_lo_ref, su_ref,
                      h_ref, xrow_ref, bucket_ref, rank_ref, cnt_ref, run_ref, *, tm, tiles_per_seq):
    i = pl.program_id(0)
    pool_width = u_ref.shape[1]
    gw = pool_width // len(POOL_WINDOWS)
    tile_in_seq = i % tiles_per_seq
    first = tile_in_seq == 0
    pos = tile_in_seq * tm + lax.broadcasted_iota(jnp.int32, (tm, 1), 0)

    halo = jnp.where(first, 0.0, halo_ref[...])
    u = u_ref[...]
    mixed = x_ref[...]
    for g, w in enumerate(POOL_WINDOWS):
        ug = u[:, g * gw:(g + 1) * gw]
        ext = jnp.concatenate([halo[:, g * gw:(g + 1) * gw], ug], axis=0)
        shift = 1
        while shift < w:
            ext = ext + pltpu.roll(ext, shift, 0)
            shift *= 2
        win = ext[POOL_HALO:, :]
        inv_count = 1.0 / jnp.minimum(pos + 1, w).astype(jnp.float32)
        pooled = win * inv_count - ug
        pg = _dot(pooled.astype(jnp.bfloat16), wp_ref[g]) * ps_ref[:, g * gw:(g + 1) * gw]
        mixed = mixed + _dot(pg.astype(jnp.bfloat16), wo_ref[g * gw:(g + 1) * gw, :])
    h1 = mixed + _dot(a_ref[...], wo_ref[pool_width:, :])

    hn = _rms(h1, gc_ref[...]).astype(jnp.bfloat16)
    d = h1.shape[1]
    hd = d // MEM_HEADS
    q = (_dot(hn, wq_ref[...]) * (1.0 / math.sqrt(hd))).astype(jnp.bfloat16)
    outs = []
    for hh in range(MEM_HEADS):
        sl = slice(hh * hd, (hh + 1) * hd)
        s = _dot_nt(q[:, sl], km_ref[:, sl])
        e = jnp.exp(s - jnp.max(s, axis=-1, keepdims=True))
        p = e * (1.0 / jnp.sum(e, axis=-1, keepdims=True))
        outs.append(_dot(p.astype(jnp.bfloat16), vm_ref[:, sl]))
    o = jnp.concatenate(outs, axis=-1).astype(jnp.bfloat16)
    h2 = h1 + _dot(o, wom_ref[...])
    h_ref[...] = h2

    xn = _rms(h2, gf_ref[...])
    x_hi = xn.astype(jnp.bfloat16)
    x_lo = (xn - x_hi.astype(jnp.float32)).astype(jnp.bfloat16)
    wr_hi = wr_hi_ref[...]
    logits = _dot_nt(wr_hi, x_hi) + _dot_nt(wr_lo_ref[...], x_hi) + _dot_nt(wr_hi, x_lo)
    bucket, w_a, w_b = _route_t(logits)

    @pl.when(i == 0)
    def _():
        run_ref[...] = jnp.zeros_like(run_ref)

    brow = lax.broadcasted_iota(jnp.int32, (ROUTER_ROWS, tm), 0).astype(jnp.float32)
    onehot = (brow == bucket).astype(jnp.float32)
    before = _dot(onehot.astype(jnp.bfloat16), su_ref[...])
    run = run_ref[...]
    rank = jnp.sum(onehot * (before + run[:, :1]), axis=0, keepdims=True)
    run = run + jnp.sum(onehot, axis=1, keepdims=True)
    run_ref[...] = run
    cnt_ref[...] = run
    bucket_ref[...] = bucket.astype(jnp.int32)
    rank_ref[...] = rank.astype(jnp.int32)

    prow = lax.broadcasted_iota(jnp.int32, (PAYLOAD_LANES, tm), 0)
    payload_t = jnp.where(prow == 0, w_a, jnp.where(prow == 1, w_b, 0.0))
    xrow_ref[:, :d] = xn
    xrow_ref[:, d:] = payload_t.T


def _mix_cross(x2, u, attn, w_pool, pool_scale, w_out, g_cross, wq, kmem, vmem, wom, g_ffn, wr_hi, wr_lo,
               *, tm, seq, mem_len):
    n, d = x2.shape
    pool_width = u.shape[1]
    attn_width = attn.shape[1]
    tiles_per_seq = seq // tm
    n_tiles = n // tm
    halo_blocks = tm // POOL_HALO
    su = (lax.broadcasted_iota(jnp.int32, (tm, tm), 0)
          < lax.broadcasted_iota(jnp.int32, (tm, tm), 1)).astype(jnp.bfloat16)
    kern = functools.partial(_mix_cross_kernel, tm=tm, tiles_per_seq=tiles_per_seq)
    const2 = lambda i: (0, 0)
    mem_map = lambda i: (i // tiles_per_seq, 0)
    return pl.pallas_call(
        kern,
        grid=(n_tiles,),
        in_specs=[
            pl.BlockSpec((tm, d), lambda i: (i, 0)),
            pl.BlockSpec((tm, pool_width), lambda i: (i, 0)),
            pl.BlockSpec((POOL_HALO, pool_width), lambda i: (jnp.maximum(i * halo_blocks - 1, 0), 0)),
            pl.BlockSpec((tm, attn_width), lambda i: (i, 0)),
            pl.BlockSpec(w_pool.shape, lambda i: (0, 0, 0)),
            pl.BlockSpec((1, pool_width), const2),
            pl.BlockSpec(w_out.shape, const2),
            pl.BlockSpec((1, d), const2),
            pl.BlockSpec((d, d), const2),
            pl.BlockSpec((mem_len, d), mem_map),
            pl.BlockSpec((mem_len, d), mem_map),
            pl.BlockSpec((d, d), const2),
            pl.BlockSpec((1, d), const2),
            pl.BlockSpec((ROUTER_ROWS, d), const2),
            pl.BlockSpec((ROUTER_ROWS, d), const2),
            pl.BlockSpec((tm, tm), const2),
        ],
        out_specs=[
            pl.BlockSpec((tm, d), lambda i: (i, 0)),
            pl.BlockSpec((tm, d + PAYLOAD_LANES), lambda i: (i, 0)),
            pl.BlockSpec((None, 1, tm), lambda i: (i, 0, 0)),
            pl.BlockSpec((None, 1, tm), lambda i: (i, 0, 0)),
            pl.BlockSpec((ROUTER_ROWS, LANES_V7X), const2),
        ],
        out_shape=[
            jax.ShapeDtypeStruct((n, d), jnp.float32),
            jax.ShapeDtypeStruct((n, d + PAYLOAD_LANES), jnp.float32),
            jax.ShapeDtypeStruct((n_tiles, 1, tm), jnp.int32),
            jax.ShapeDtypeStruct((n_tiles, 1, tm), jnp.int32),
            jax.ShapeDtypeStruct((ROUTER_ROWS, LANES_V7X), jnp.float32),
        ],
        scratch_shapes=[pltpu.VMEM((ROUTER_ROWS, LANES_V7X), jnp.float32)],
        compiler_params=pltpu.CompilerParams(
            dimension_semantics=("arbitrary",), vmem_limit_bytes=VMEM_LIMIT_BYTES),
        name="mix_cross",
    )(x2, u, u, attn, w_pool, pool_scale, w_out, g_cross, wq, kmem, vmem, wom, g_ffn, wr_hi, wr_lo, su)


def _dispatch_kernel(pos_ref, x_ref, xs_init_ref, xs_ref, sem, *, tm):
    del xs_init_ref
    base = pl.program_id(0) * tm

    def row_copy(r, p):
        return pltpu.make_async_copy(x_ref.at[pl.ds(r, 1), :], xs_ref.at[pl.ds(p, 1), :], sem)

    def issue(r, carry):
        row_copy(r, pos_ref[base + r]).start()
        return carry

    lax.fori_loop(0, tm, issue, 0)
    pltpu.make_async_copy(x_ref, xs_ref.at[pl.ds(0, tm), :], sem).wait()


def _dispatch(pos, xrow, n_pad, *, tm):
    n, width = xrow.shape
    xs_init = jnp.zeros((n_pad, width), xrow.dtype)
    return pl.pallas_call(
        functools.partial(_dispatch_kernel, tm=tm),
        grid_spec=pltpu.PrefetchScalarGridSpec(
            num_scalar_prefetch=1,
            grid=(n // tm,),
            in_specs=[
                pl.BlockSpec((tm, width), lambda i, pos: (i, 0)),
                pl.BlockSpec(memory_space=pl.ANY),
            ],
            out_specs=pl.BlockSpec(memory_space=pl.ANY),
            scratch_shapes=[pltpu.SemaphoreType.DMA],
        ),
        out_shape=jax.ShapeDtypeStruct((n_pad, width), xrow.dtype),
        input_output_aliases={2: 0},
        compiler_params=pltpu.CompilerParams(
            dimension_semantics=("arbitrary",), vmem_limit_bytes=VMEM_LIMIT_BYTES),
        name="moe_dispatch",
    )(pos, xrow, xs_init)


def _expert_kernel(ea_ref, eb_ref, na_ref, xs_ref, wga_ref, wua_ref, wda_ref, wgb_ref, wub_ref, wdb_ref,
                   y_ref, *, d):
    del ea_ref, eb_ref
    active = pl.program_id(0) < na_ref[0]

    @pl.when(jnp.logical_not(active))
    def _():
        y_ref[...] = jnp.zeros_like(y_ref)

    @pl.when(active)
    def _():
        x = xs_ref[:, :d].astype(jnp.bfloat16)

        def mlp(wg_ref, wu_ref, wd_ref):
            gate = _dot(x, wg_ref[...])
            up = _dot(x, wu_ref[...])
            hmid = (gate * (1.0 / (1.0 + jnp.exp(-gate)))) * up
            return _dot(hmid.astype(jnp.bfloat16), wd_ref[...])

        y_ref[...] = (xs_ref[:, d:d + 1] * mlp(wga_ref, wua_ref, wda_ref)
                      + xs_ref[:, d + 1:d + 2] * mlp(wgb_ref, wub_ref, wdb_ref))


def _experts(tile_ea, tile_eb, n_active, xs, wg, wu, wd, *, tm):
    n_pad, width = xs.shape
    d, ff = wg.shape[1], wg.shape[2]
    row_map = lambda j, ea, eb, na: (jnp.minimum(j, na[0] - 1), 0)
    a_map = lambda j, ea, eb, na: (ea[j], 0, 0)
    b_map = lambda j, ea, eb, na: (eb[j], 0, 0)
    return pl.pallas_call(
        functools.partial(_expert_kernel, d=d),
        grid_spec=pltpu.PrefetchScalarGridSpec(
            num_scalar_prefetch=3,
            grid=(n_pad // tm,),
            in_specs=[
                pl.BlockSpec((tm, width), row_map),
                pl.BlockSpec((None, d, ff), a_map),
                pl.BlockSpec((None, d, ff), a_map),
                pl.BlockSpec((None, ff, d), a_map),
                pl.BlockSpec((None, d, ff), b_map),
                pl.BlockSpec((None, d, ff), b_map),
                pl.BlockSpec((None, ff, d), b_map),
            ],
            out_specs=pl.BlockSpec((tm, d), lambda j, ea, eb, na: (j, 0)),
        ),
        out_shape=jax.ShapeDtypeStruct((n_pad, d), jnp.float32),
        compiler_params=pltpu.CompilerParams(
            dimension_semantics=("arbitrary",), vmem_limit_bytes=VMEM_LIMIT_BYTES),
        name="moe_experts",
    )(tile_ea, tile_eb, n_active, xs, wg, wu, wd, wg, wu, wd)


def _combine_kernel(pos_ref, h_ref, y_ref, g_ref, o_ref, ybuf, sems, *, tm, n_tiles):
    i = pl.program_id(0)

    def row_copy(slot, r, p):
        return pltpu.make_async_copy(y_ref.at[pl.ds(p, 1), :], ybuf.at[slot, pl.ds(r, 1), :], sems.at[slot])

    def gather(tile, slot):
        def issue(r, carry):
            row_copy(slot, r, pos_ref[tile * tm + r]).start()
            return carry
        lax.fori_loop(0, tm, issue, 0)

    @pl.when(i == 0)
    def _():
        gather(0, 0)

    @pl.when(i + 1 < n_tiles)
    def _():
        gather(i + 1, (i + 1) % 2)

    slot = i % 2
    pltpu.make_async_copy(y_ref.at[pl.ds(0, tm), :], ybuf.at[slot], sems.at[slot]).wait()
    o_ref[...] = _rms(h_ref[...] + ybuf[slot], g_ref[...])


def _combine(pos, h, y_sorted, g_final, *, tm):
    n, d = h.shape
    n_tiles = n // tm
    return pl.pallas_call(
        functools.partial(_combine_kernel, tm=tm, n_tiles=n_tiles),
        grid_spec=pltpu.PrefetchScalarGridSpec(
            num_scalar_prefetch=1,
            grid=(n_tiles,),
            in_specs=[
                pl.BlockSpec((tm, d), lambda i, pos: (i, 0)),
                pl.BlockSpec(memory_space=pl.ANY),
                pl.BlockSpec((1, d), lambda i, pos: (0, 0)),
            ],
            out_specs=pl.BlockSpec((tm, d), lambda i, pos: (i, 0)),
            scratch_shapes=[pltpu.VMEM((2, tm, d), jnp.float32), pltpu.SemaphoreType.DMA((2,))],
        ),
        out_shape=jax.ShapeDtypeStruct((n, d), jnp.float32),
        compiler_params=pltpu.CompilerParams(
            dimension_semantics=("arbitrary",), vmem_limit_bytes=VMEM_LIMIT_BYTES),
        name="moe_combine",
    )(pos, h, y_sorted, g_final)


def _tile(n, pref):
    t = min(n, pref)
    assert n % t == 0, (n, t)
    return t


def _sorted_layout(bucket, rank, counts, *, tm_e, n_tiles_e):
    seg_tiles = (counts + tm_e - 1) // tm_e
    seg_end = jnp.cumsum(seg_tiles)
    seg_start = seg_end - seg_tiles
    pos = (seg_start * tm_e)[bucket] + rank
    n_active = seg_end[-1]
    tile = jnp.minimum(jnp.arange(n_tiles_e, dtype=jnp.int32), n_active - 1)
    tile_bucket = jnp.sum((tile[:, None] >= seg_end[None, :]).astype(jnp.int32), axis=1)
    group, pair = tile_bucket // len(PAIRS), tile_bucket % len(PAIRS)
    pair_a = jnp.array([p[0] for p in PAIRS], jnp.int32)
    pair_b = jnp.array([p[1] for p in PAIRS], jnp.int32)
    tile_ea = group * EXPERTS_PER_GROUP + pair_a[pair]
    tile_eb = group * EXPERTS_PER_GROUP + pair_b[pair]
    return pos.astype(jnp.int32), tile_ea.astype(jnp.int32), tile_eb.astype(jnp.int32), n_active.astype(jnp.int32)


def kernel(x, mem, norm_mix, w_in, w_pool, pool_scale, w_out, norm_cross, norm_mem, w_q_mem, w_k_mem,
           w_v_mem, w_o_mem, norm_ffn, w_group, w_expert, w_gate, w_up, w_down, norm_final):
    b, s, d = x.shape
    mem_len = mem.shape[1]
    depth = norm_mix.shape[0]
    pool_width = pool_scale.shape[1]
    attn_width = w_out.shape[1] - pool_width
    bf = jnp.bfloat16
    n = b * s

    tq = _tile(s, MXU_DIM_V7X)
    tri = (lax.broadcasted_iota(jnp.int32, (tq, tq), 0)
           >= lax.broadcasted_iota(jnp.int32, (tq, tq), 1)).astype(bf)

    assert depth == 1, "single-layer problem: the final RMSNorm is fused into the combine kernel"
    l = 0
    h = x.reshape(n, d)
    u, qkv = _in_proj(h, norm_mix[l][None], w_in[l].astype(bf),
                      pool_width=pool_width, attn_width=attn_width, tm=_tile(n, 512))
    attn = _sb_attention(qkv.reshape(b, s, 3 * attn_width), tri, attn_width=attn_width, tq=tq)
    kmem, vmem = _mem_kv(mem.reshape(b * mem_len, d), norm_mem[l][None],
                         w_k_mem[l].astype(bf), w_v_mem[l].astype(bf), tm=mem_len)

    w_router = jnp.concatenate([w_group[l], w_expert[l]], axis=1).T
    w_router = jnp.pad(w_router, ((0, ROUTER_ROWS - w_router.shape[0]), (0, 0)))
    wr_hi = w_router.astype(bf)
    wr_lo = (w_router - wr_hi.astype(jnp.float32)).astype(bf)
    h, xrow, bucket, rank, counts = _mix_cross(
        h, u, attn.reshape(n, attn_width), w_pool[l].astype(bf), pool_scale[l][None], w_out[l].astype(bf),
        norm_cross[l][None], w_q_mem[l].astype(bf), kmem, vmem, w_o_mem[l].astype(bf),
        norm_ffn[l][None], wr_hi, wr_lo, tm=_tile(s, 512), seq=s, mem_len=mem_len)

    tm_e = _tile(n, MXU_DIM_V7X)
    n_tiles_e = -(-(n + N_BUCKETS * (tm_e - 1)) // tm_e)
    pos, tile_ea, tile_eb, n_active = _sorted_layout(
        bucket.reshape(n), rank.reshape(n), counts[:N_BUCKETS, 0].astype(jnp.int32),
        tm_e=tm_e, n_tiles_e=n_tiles_e)
    xs = _dispatch(pos, xrow, n_tiles_e * tm_e, tm=_tile(n, 512))
    y_sorted = _experts(tile_ea, tile_eb, n_active[None], xs, w_gate[l].astype(bf), w_up[l].astype(bf),
                        w_down[l].astype(bf), tm=tm_e)
    out = _combine(pos, h, y_sorted, norm_final[None], tm=_tile(n, 512))
    return out.reshape(b, s, d)
```

```python
import functools
import math

import jax
import jax.numpy as jnp
from jax import lax
from jax.experimental import pallas as pl
from jax.experimental.pallas import tpu as pltpu

RMS_EPS = 1e-6
POOL_WINDOWS = (2, 4, 8, 16)
SB_HEAD_DIM = 64
MEM_HEADS = 4
N_GROUPS = 4
EXPERTS_PER_GROUP = 4
N_EXPERTS = N_GROUPS * EXPERTS_PER_GROUP

LANES_V7X = 128
MXU_DIM_V7X = 256
VMEM_LIMIT_BYTES = 56 * 1024 * 1024

LOG2E = 1.4426950408889634
UNDERFLOW_EXPONENT = 104.0
MASKED_LOGIT = -1e30
POOL_HALO = 16

PAIRS = tuple((a, b) for a in range(EXPERTS_PER_GROUP) for b in range(a + 1, EXPERTS_PER_GROUP))
N_BUCKETS = N_GROUPS * len(PAIRS)
ROUTER_ROWS = 32
PAYLOAD_LANES = LANES_V7X
ROW_CHAINS = 1
assert N_GROUPS + N_EXPERTS <= ROUTER_ROWS and N_BUCKETS <= ROUTER_ROWS


def _rms(x, gain):
    ms = jnp.mean(x * x, axis=-1, keepdims=True)
    return x * lax.rsqrt(ms + RMS_EPS) * gain


def _dot(a, b):
    return jnp.dot(a, b, preferred_element_type=jnp.float32)


def _dot_nt(a, b):
    return lax.dot_general(a, b, (((1,), (1,)), ((), ())), preferred_element_type=jnp.float32)


def _in_proj_kernel(x_ref, g_ref, w_ref, u_ref, qkv_ref, *, pool_width, attn_width, q_scale):
    xn = _rms(x_ref[...], g_ref[...]).astype(jnp.bfloat16)
    proj = _dot(xn, w_ref[...])
    u_ref[...] = proj[:, :pool_width]
    q = proj[:, pool_width:pool_width + attn_width] * q_scale
    qkv_ref[:, :attn_width] = q.astype(jnp.bfloat16)
    qkv_ref[:, attn_width:] = proj[:, pool_width + attn_width:].astype(jnp.bfloat16)


def _in_proj(x2, gain, w_in_bf16, *, pool_width, attn_width, tm):
    n, d = x2.shape
    in_width = w_in_bf16.shape[1]
    kern = functools.partial(_in_proj_kernel, pool_width=pool_width, attn_width=attn_width,
                             q_scale=1.0 / math.sqrt(SB_HEAD_DIM))
    return pl.pallas_call(
        kern,
        grid=(n // tm,),
        in_specs=[
            pl.BlockSpec((tm, d), lambda i: (i, 0)),
            pl.BlockSpec((1, d), lambda i: (0, 0)),
            pl.BlockSpec((d, in_width), lambda i: (0, 0)),
        ],
        out_specs=[
            pl.BlockSpec((tm, pool_width), lambda i: (i, 0)),
            pl.BlockSpec((tm, 3 * attn_width), lambda i: (i, 0)),
        ],
        out_shape=[
            jax.ShapeDtypeStruct((n, pool_width), jnp.float32),
            jax.ShapeDtypeStruct((n, 3 * attn_width), jnp.bfloat16),
        ],
        compiler_params=pltpu.CompilerParams(
            dimension_semantics=("parallel",), vmem_limit_bytes=VMEM_LIMIT_BYTES),
        name="in_proj",
    )(x2, gain, w_in_bf16)


def _sb_tile(qh, kb, vb, tri, c, tail_mask):
    z = _dot_nt(qh, kb)
    if tail_mask is not None:
        m = tail_mask.shape[1]
        tail = jnp.where(tail_mask, z[:, -m:], MASKED_LOGIT)
        z = tail if m == z.shape[1] else jnp.concatenate([z[:, :-m], tail], axis=1)
    sp = (jnp.maximum(z, 0.0) + jnp.log(1.0 + jnp.exp2(jnp.abs(z) * (-LOG2E)))).astype(jnp.bfloat16)
    n = tri.shape[0]
    chunks, total = [], None
    for k0 in range(z.shape[1] - n, -1, -n):
        cum_k = _dot(sp[:, k0:k0 + n], tri)
        if total is not None:
            cum_k = cum_k + total
        total = cum_k[:, :1]
        chunks.insert(0, cum_k)
    cum = chunks[0] if len(chunks) == 1 else jnp.concatenate(chunks, axis=1)
    t = jnp.minimum(z - cum, 0.0)
    w = jnp.exp(t if c is None else t - c)
    return (total if c is None else c + total), _dot(w.astype(jnp.bfloat16), vb)


def _sb_attn_kernel(q_ref, k_ref, v_ref, tri_ref, o_ref, *, tq, sub, win):
    qi = pl.program_id(2)
    n_sub = tq // sub
    lane = lax.broadcasted_iota(jnp.int32, (tq, LANES_V7X), 1)
    head0 = lane < SB_HEAD_DIM
    q2 = q_ref[...]
    zero = jnp.zeros_like(q2)
    q_heads = (jnp.where(head0, q2, zero), jnp.where(head0, zero, q2))
    q_st = jnp.concatenate(q_heads, axis=0)

    def kv_block(j):
        start = pl.multiple_of(j * tq, tq)
        return k_ref[pl.ds(start, tq), :], v_ref[pl.ds(start, tq), :]

    def stacked_iota(rows, cols):
        row = lax.broadcasted_iota(jnp.int32, (2 * rows, cols), 0)
        col = lax.broadcasted_iota(jnp.int32, (2 * rows, cols), 1)
        return jnp.where(row >= rows, row - rows, row), col

    def first_block():
        row, col = stacked_iota(tq, tq)
        kb, vb = kv_block(0)
        return _sb_tile(q_st, kb, vb, tri_ref[...], None, col < row)

    def windows():
        row, col = stacked_iota(sub, sub)
        causal = col < row
        tri = tri_ref[...]
        parts = []
        for hf in range(n_sub):
            start = pl.multiple_of((qi * n_sub + hf + 1) * sub - win, sub)
            kw = k_ref[pl.ds(start, win), :]
            vw = v_ref[pl.ds(start, win), :]
            q_sub = jnp.concatenate([qh[hf * sub:(hf + 1) * sub] for qh in q_heads], axis=0)
            parts.append(_sb_tile(q_sub, kw, vw, tri, None, causal))
        return tuple(jnp.concatenate([p[i][h * sub:(h + 1) * sub] for h in range(2) for p in parts], axis=0)
                     for i in range(2))

    carry = lax.cond(qi == 0, first_block, windows)

    def live(carry):
        return jnp.min(carry[0], axis=0, keepdims=True)[0, 0] < UNDERFLOW_EXPONENT

    def add_block(carry, j, tail_mask):
        kb, vb = kv_block(j)
        c, acc = _sb_tile(q_st, kb, vb, tri_ref[...], carry[0], tail_mask)
        return c, carry[1] + acc

    def partial_block(carry):
        row, col = stacked_iota(tq, tq)
        fresh = col < (row // sub + 1) * sub + (tq - win)
        return add_block(carry, qi - 1, fresh)

    carry = lax.cond(jnp.logical_and(qi > 0, live(carry)), partial_block, lambda cr: cr, carry)

    def cond(state):
        step, alive, _ = state
        return jnp.logical_and(step < qi - 1, alive > 0)

    def body(state):
        step, _, carry = state
        carry = add_block(carry, qi - 2 - step, None)
        return step + 1, live(carry).astype(jnp.int32), carry

    _, _, carry = lax.while_loop(cond, body, (jnp.int32(0), live(carry).astype(jnp.int32), carry))
    acc = carry[1]
    o_ref[...] = jnp.where(head0, acc[:tq], acc[tq:]).astype(o_ref.dtype)


def _sb_attention(qkv3, *, attn_width, tq, sub, win):
    b, s, _ = qkv3.shape
    assert tq % sub == 0 and win % sub == 0 and sub <= win <= tq + sub, (tq, sub, win)
    n_pairs = attn_width // LANES_V7X
    n = math.gcd(math.gcd(tq, win), MXU_DIM_V7X)
    tri = (lax.broadcasted_iota(jnp.int32, (n, n), 0)
           >= lax.broadcasted_iota(jnp.int32, (n, n), 1)).astype(jnp.bfloat16)

    kern = functools.partial(_sb_attn_kernel, tq=tq, sub=sub, win=win)
    const2 = lambda bi, hp, qi: (0, 0)
    return pl.pallas_call(
        kern,
        grid=(b, n_pairs, s // tq),
        in_specs=[
            pl.BlockSpec((None, tq, LANES_V7X), lambda bi, hp, qi: (bi, qi, hp)),
            pl.BlockSpec((None, s, LANES_V7X), lambda bi, hp, qi: (bi, 0, n_pairs + hp)),
            pl.BlockSpec((None, s, LANES_V7X), lambda bi, hp, qi: (bi, 0, 2 * n_pairs + hp)),
            pl.BlockSpec((n, n), const2),
        ],
        out_specs=pl.BlockSpec((None, tq, LANES_V7X), lambda bi, hp, qi: (bi, qi, hp)),
        out_shape=jax.ShapeDtypeStruct((b, s, attn_width), jnp.bfloat16),
        compiler_params=pltpu.CompilerParams(
            dimension_semantics=("parallel", "parallel", "parallel"),
            vmem_limit_bytes=VMEM_LIMIT_BYTES),
        name="sb_attention",
    )(qkv3, qkv3, qkv3, tri)


def _mem_kv_kernel(m_ref, g_ref, wk_ref, wv_ref, k_ref, v_ref):
    mn = _rms(m_ref[...], g_ref[...]).astype(jnp.bfloat16)
    k_ref[...] = _dot(mn, wk_ref[...]).astype(jnp.bfloat16)
    v_ref[...] = _dot(mn, wv_ref[...]).astype(jnp.bfloat16)


def _mem_kv(mem2, gain, wk, wv, *, tm):
    n, d = mem2.shape
    row = pl.BlockSpec((tm, d), lambda i: (i, 0))
    full = pl.BlockSpec((d, d), lambda i: (0, 0))
    return pl.pallas_call(
        _mem_kv_kernel,
        grid=(n // tm,),
        in_specs=[row, pl.BlockSpec((1, d), lambda i: (0, 0)), full, full],
        out_specs=[row, row],
        out_shape=[jax.ShapeDtypeStruct((n, d), jnp.bfloat16)] * 2,
        compiler_params=pltpu.CompilerParams(
            dimension_semantics=("parallel",), vmem_limit_bytes=VMEM_LIMIT_BYTES),
        name="mem_kv",
    )(mem2, gain, wk, wv)


def _route_t(logits):
    neg = jnp.float32(-jnp.inf)
    big = jnp.float32(ROUTER_ROWS)
    row = lax.broadcasted_iota(jnp.int32, logits.shape, 0).astype(jnp.float32)

    def col_max(mask):
        return jnp.max(jnp.where(mask, logits, neg), axis=0, keepdims=True)

    def first_argmax(mask, mx):
        return jnp.min(jnp.where(mask & (logits == mx), row, big), axis=0, keepdims=True)

    gmask = row < N_GROUPS
    gmax = col_max(gmask)
    gsum = jnp.sum(jnp.where(gmask, jnp.exp(logits - gmax), 0.0), axis=0, keepdims=True)
    g_gate = 1.0 / gsum
    g_idx = first_argmax(gmask, gmax)

    lo = N_GROUPS + EXPERTS_PER_GROUP * g_idx
    emask = (row >= lo) & (row < lo + EXPERTS_PER_GROUP)
    m1 = col_max(emask)
    i1 = first_argmax(emask, m1)
    mask2 = emask & (row != i1)
    m2 = col_max(mask2)
    i2 = first_argmax(mask2, m2)
    esum = jnp.sum(jnp.where(emask, jnp.exp(logits - m1), 0.0), axis=0, keepdims=True)
    p1 = 1.0 / esum
    p2 = jnp.exp(m2 - m1) / esum
    tot = p1 + p2
    w1 = g_gate * (p1 / tot)
    w2 = g_gate * (p2 / tot)

    first = i1 < i2
    la = jnp.where(first, i1, i2) - lo
    lb = jnp.where(first, i2, i1) - lo
    pair = la * (2 * EXPERTS_PER_GROUP - 1 - la) * 0.5 + (lb - la - 1.0)
    bucket = g_idx * len(PAIRS) + pair
    return bucket, jnp.where(first, w1, w2), jnp.where(first, w2, w1)


def _mix_cross_kernel(x_ref, u_ref, halo_ref, a_ref, wp_ref, ps_ref, wo_ref, gc_ref,
                      wq_ref, km_ref, vm_ref, wom_ref, gf_ref, wr_hi_ref, wr_lo_ref, su_ref,
                      h_ref, xrow_ref, bucket_ref, rank_ref, cnt_ref, run_ref, *, tm, tiles_per_seq):
    i = pl.program_id(0)
    pool_width = u_ref.shape[1]
    gw = pool_width // len(POOL_WINDOWS)
    tile_in_seq = i % tiles_per_seq
    first = tile_in_seq == 0
    pos = tile_in_seq * tm + lax.broadcasted_iota(jnp.int32, (tm, 1), 0)

    halo = jnp.where(first, 0.0, halo_ref[...])
    u = u_ref[...]
    pooled = []
    for g, w in enumerate(POOL_WINDOWS):
        ug = u[:, g * gw:(g + 1) * gw]
        ext = jnp.concatenate([halo[:, g * gw:(g + 1) * gw], ug], axis=0)
        shift = 1
        while shift < w:
            ext = ext + pltpu.roll(ext, shift, 0)
            shift *= 2
        win = ext[POOL_HALO:, :]
        inv_count = 1.0 / jnp.minimum(pos + 1, w).astype(jnp.float32)
        pooled.append((win * inv_count - ug).astype(jnp.bfloat16))

    d = x_ref.shape[1]
    hd = d // MEM_HEADS

    def rows_chain(r0, r1):
        mixed = x_ref[r0:r1, :]
        for g in range(len(POOL_WINDOWS)):
            pg = _dot(pooled[g][r0:r1], wp_ref[g]) * ps_ref[:, g * gw:(g + 1) * gw]
            mixed = mixed + _dot(pg.astype(jnp.bfloat16), wo_ref[g * gw:(g + 1) * gw, :])
        h1 = mixed + _dot(a_ref[r0:r1, :], wo_ref[pool_width:, :])

        hn = _rms(h1, gc_ref[...]).astype(jnp.bfloat16)
        q = (_dot(hn, wq_ref[...]) * (1.0 / math.sqrt(hd))).astype(jnp.bfloat16)
        outs = []
        for hh in range(MEM_HEADS):
            sl = slice(hh * hd, (hh + 1) * hd)
            s = _dot_nt(q[:, sl], km_ref[:, sl])
            e = jnp.exp(s - jnp.max(s, axis=-1, keepdims=True))
            p = e * (1.0 / jnp.sum(e, axis=-1, keepdims=True))
            outs.append(_dot(p.astype(jnp.bfloat16), vm_ref[:, sl]))
        o = jnp.concatenate(outs, axis=-1).astype(jnp.bfloat16)
        h2 = h1 + _dot(o, wom_ref[...])
        h_ref[r0:r1, :] = h2

        xn = _rms(h2, gf_ref[...])
        xrow_ref[r0:r1, :d] = xn
        x_hi = xn.astype(jnp.bfloat16)
        x_lo = (xn - x_hi.astype(jnp.float32)).astype(jnp.bfloat16)
        wr_hi = wr_hi_ref[...]
        return _dot_nt(wr_hi, x_hi) + _dot_nt(wr_lo_ref[...], x_hi) + _dot_nt(wr_hi, x_lo)

    bounds = [k * tm // ROW_CHAINS for k in range(ROW_CHAINS + 1)]
    logits = jnp.concatenate([rows_chain(bounds[k], bounds[k + 1]) for k in range(ROW_CHAINS)], axis=1)
    bucket, w_a, w_b = _route_t(logits)

    @pl.when(i == 0)
    def _():
        run_ref[...] = jnp.zeros_like(run_ref)

    brow = lax.broadcasted_iota(jnp.int32, (ROUTER_ROWS, tm), 0).astype(jnp.float32)
    onehot = (brow == bucket).astype(jnp.float32)
    before = _dot(onehot.astype(jnp.bfloat16), su_ref[...])
    run = run_ref[...]
    rank = jnp.sum(onehot * (before + run[:, :1]), axis=0, keepdims=True)
    run = run + jnp.sum(onehot, axis=1, keepdims=True)
    run_ref[...] = run
    cnt_ref[...] = run
    bucket_ref[...] = bucket.astype(jnp.int32)
    rank_ref[...] = rank.astype(jnp.int32)

    prow = lax.broadcasted_iota(jnp.int32, (PAYLOAD_LANES, tm), 0)
    payload_t = jnp.where(prow == 0, w_a, jnp.where(prow == 1, w_b, 0.0))
    xrow_ref[:, d:] = payload_t.T


def _mix_cross(x2, u, attn, w_pool, pool_scale, w_out, g_cross, wq, kmem, vmem, wom, g_ffn, wr_hi, wr_lo,
               *, tm, seq, mem_len):
    n, d = x2.shape
    pool_width = u.shape[1]
    attn_width = attn.shape[1]
    tiles_per_seq = seq // tm
    n_tiles = n // tm
    halo_blocks = tm // POOL_HALO
    su = (lax.broadcasted_iota(jnp.int32, (tm, tm), 0)
          < lax.broadcasted_iota(jnp.int32, (tm, tm), 1)).astype(jnp.bfloat16)
    kern = functools.partial(_mix_cross_kernel, tm=tm, tiles_per_seq=tiles_per_seq)
    const2 = lambda i: (0, 0)
    mem_map = lambda i: (i // tiles_per_seq, 0)
    return pl.pallas_call(
        kern,
        grid=(n_tiles,),
        in_specs=[
            pl.BlockSpec((tm, d), lambda i: (i, 0)),
            pl.BlockSpec((tm, pool_width), lambda i: (i, 0)),
            pl.BlockSpec((POOL_HALO, pool_width), lambda i: (jnp.maximum(i * halo_blocks - 1, 0), 0)),
            pl.BlockSpec((tm, attn_width), lambda i: (i, 0)),
            pl.BlockSpec(w_pool.shape, lambda i: (0, 0, 0)),
            pl.BlockSpec((1, pool_width), const2),
            pl.BlockSpec(w_out.shape, const2),
            pl.BlockSpec((1, d), const2),
            pl.BlockSpec((d, d), const2),
            pl.BlockSpec((mem_len, d), mem_map),
            pl.BlockSpec((mem_len, d), mem_map),
            pl.BlockSpec((d, d), const2),
            pl.BlockSpec((1, d), const2),
            pl.BlockSpec((ROUTER_ROWS, d), const2),
            pl.BlockSpec((ROUTER_ROWS, d), const2),
            pl.BlockSpec((tm, tm), const2),
        ],
        out_specs=[
            pl.BlockSpec((tm, d), lambda i: (i, 0)),
            pl.BlockSpec((tm, d + PAYLOAD_LANES), lambda i: (i, 0)),
            pl.BlockSpec((None, 1, tm), lambda i: (i, 0, 0)),
            pl.BlockSpec((None, 1, tm), lambda i: (i, 0, 0)),
            pl.BlockSpec((ROUTER_ROWS, LANES_V7X), const2),
        ],
        out_shape=[
            jax.ShapeDtypeStruct((n, d), jnp.float32),
            jax.ShapeDtypeStruct((n, d + PAYLOAD_LANES), jnp.float32),
            jax.ShapeDtypeStruct((n_tiles, 1, tm), jnp.int32),
            jax.ShapeDtypeStruct((n_tiles, 1, tm), jnp.int32),
            jax.ShapeDtypeStruct((ROUTER_ROWS, LANES_V7X), jnp.float32),
        ],
        scratch_shapes=[pltpu.VMEM((ROUTER_ROWS, LANES_V7X), jnp.float32)],
        compiler_params=pltpu.CompilerParams(
            dimension_semantics=("arbitrary",), vmem_limit_bytes=VMEM_LIMIT_BYTES),
        name="mix_cross",
    )(x2, u, u, attn, w_pool, pool_scale, w_out, g_cross, wq, kmem, vmem, wom, g_ffn, wr_hi, wr_lo, su)


def _dispatch_kernel(pos_ref, x_ref, xs_init_ref, xs_ref, sem, *, tm):
    del xs_init_ref
    base = pl.program_id(0) * tm

    def row_copy(r, p):
        return pltpu.make_async_copy(x_ref.at[pl.ds(r, 1), :], xs_ref.at[pl.ds(p, 1), :], sem)

    for r in range(tm):
        row_copy(r, pos_ref[base + r]).start()
    pltpu.make_async_copy(x_ref, xs_ref.at[pl.ds(0, tm), :], sem).wait()


def _dispatch(pos, xrow, n_pad, *, tm):
    n, width = xrow.shape
    xs_init = jnp.zeros((n_pad, width), xrow.dtype)
    return pl.pallas_call(
        functools.partial(_dispatch_kernel, tm=tm),
        grid_spec=pltpu.PrefetchScalarGridSpec(
            num_scalar_prefetch=1,
            grid=(n // tm,),
            in_specs=[
                pl.BlockSpec((tm, width), lambda i, pos: (i, 0)),
                pl.BlockSpec(memory_space=pl.ANY),
            ],
            out_specs=pl.BlockSpec(memory_space=pl.ANY),
            scratch_shapes=[pltpu.SemaphoreType.DMA],
        ),
        out_shape=jax.ShapeDtypeStruct((n_pad, width), xrow.dtype),
        input_output_aliases={2: 0},
        compiler_params=pltpu.CompilerParams(
            dimension_semantics=("arbitrary",), vmem_limit_bytes=VMEM_LIMIT_BYTES),
        name="moe_dispatch",
    )(pos, xrow, xs_init)


def _expert_kernel(ea_ref, eb_ref, na_ref, xs_ref, wga_ref, wua_ref, wda_ref, wgb_ref, wub_ref, wdb_ref,
                   y_ref, *, d):
    del ea_ref, eb_ref
    active = pl.program_id(0) < na_ref[0]

    @pl.when(jnp.logical_not(active))
    def _():
        y_ref[...] = jnp.zeros_like(y_ref)

    @pl.when(active)
    def _():
        x = xs_ref[:, :d].astype(jnp.bfloat16)

        def mlp(wg_ref, wu_ref, wd_ref):
            gate = _dot(x, wg_ref[...])
            up = _dot(x, wu_ref[...])
            hmid = (gate * (1.0 / (1.0 + jnp.exp(-gate)))) * up
            return _dot(hmid.astype(jnp.bfloat16), wd_ref[...])

        y_ref[...] = (xs_ref[:, d:d + 1] * mlp(wga_ref, wua_ref, wda_ref)
                      + xs_ref[:, d + 1:d + 2] * mlp(wgb_ref, wub_ref, wdb_ref))


def _experts(tile_ea, tile_eb, n_active, xs, wg, wu, wd, *, tm):
    n_pad, width = xs.shape
    d, ff = wg.shape[1], wg.shape[2]
    row_map = lambda j, ea, eb, na: (jnp.minimum(j, na[0] - 1), 0)
    a_map = lambda j, ea, eb, na: (ea[j], 0, 0)
    b_map = lambda j, ea, eb, na: (eb[j], 0, 0)
    return pl.pallas_call(
        functools.partial(_expert_kernel, d=d),
        grid_spec=pltpu.PrefetchScalarGridSpec(
            num_scalar_prefetch=3,
            grid=(n_pad // tm,),
            in_specs=[
                pl.BlockSpec((tm, width), row_map),
                pl.BlockSpec((None, d, ff), a_map),
                pl.BlockSpec((None, d, ff), a_map),
                pl.BlockSpec((None, ff, d), a_map),
                pl.BlockSpec((None, d, ff), b_map),
                pl.BlockSpec((None, d, ff), b_map),
                pl.BlockSpec((None, ff, d), b_map),
            ],
            out_specs=pl.BlockSpec((tm, d), lambda j, ea, eb, na: (j, 0)),
        ),
        out_shape=jax.ShapeDtypeStruct((n_pad, d), jnp.float32),
        compiler_params=pltpu.CompilerParams(
            dimension_semantics=("arbitrary",), vmem_limit_bytes=VMEM_LIMIT_BYTES),
        name="moe_experts",
    )(tile_ea, tile_eb, n_active, xs, wg, wu, wd, wg, wu, wd)


def _combine_kernel(pos_ref, h_ref, y_ref, g_ref, o_ref, ybuf, sems, *, tm, n_tiles):
    i = pl.program_id(0)

    def row_copy(slot, r, p):
        return pltpu.make_async_copy(y_ref.at[pl.ds(p, 1), :], ybuf.at[slot, pl.ds(r, 1), :], sems.at[slot])

    @pl.when(i < n_tiles)
    def _():
        slot = i % 2
        for r in range(tm):
            row_copy(slot, r, pos_ref[i * tm + r]).start()

    @pl.when(i > 0)
    def _():
        slot = (i - 1) % 2
        pltpu.make_async_copy(y_ref.at[pl.ds(0, tm), :], ybuf.at[slot], sems.at[slot]).wait()
        o_ref[...] = _rms(h_ref[...] + ybuf[slot], g_ref[...])


def _combine(pos, h, y_sorted, g_final, *, tm):
    n, d = h.shape
    n_tiles = n // tm
    prev_tile = lambda i, pos: (jnp.maximum(i - 1, 0), 0)
    return pl.pallas_call(
        functools.partial(_combine_kernel, tm=tm, n_tiles=n_tiles),
        grid_spec=pltpu.PrefetchScalarGridSpec(
            num_scalar_prefetch=1,
            grid=(n_tiles + 1,),
            in_specs=[
                pl.BlockSpec((tm, d), prev_tile),
                pl.BlockSpec(memory_space=pl.ANY),
                pl.BlockSpec((1, d), lambda i, pos: (0, 0)),
            ],
            out_specs=pl.BlockSpec((tm, d), prev_tile),
            scratch_shapes=[pltpu.VMEM((2, tm, d), jnp.float32), pltpu.SemaphoreType.DMA((2,))],
        ),
        out_shape=jax.ShapeDtypeStruct((n, d), jnp.float32),
        compiler_params=pltpu.CompilerParams(
            dimension_semantics=("arbitrary",), vmem_limit_bytes=VMEM_LIMIT_BYTES),
        name="moe_combine",
    )(pos, h, y_sorted, g_final)


def _tile(n, pref):
    t = min(n, pref)
    assert n % t == 0, (n, t)
    return t


def _sorted_layout(bucket, rank, counts, *, tm_e, n_tiles_e):
    seg_tiles = (counts + tm_e - 1) // tm_e
    seg_end = jnp.cumsum(seg_tiles)
    seg_start = seg_end - seg_tiles
    pos = (seg_start * tm_e)[bucket] + rank
    n_active = seg_end[-1]
    tile = jnp.minimum(jnp.arange(n_tiles_e, dtype=jnp.int32), n_active - 1)
    tile_bucket = jnp.sum((tile[:, None] >= seg_end[None, :]).astype(jnp.int32), axis=1)
    group, pair = tile_bucket // len(PAIRS), tile_bucket % len(PAIRS)
    pair_a = jnp.array([p[0] for p in PAIRS], jnp.int32)
    pair_b = jnp.array([p[1] for p in PAIRS], jnp.int32)
    tile_ea = group * EXPERTS_PER_GROUP + pair_a[pair]
    tile_eb = group * EXPERTS_PER_GROUP + pair_b[pair]
    return pos.astype(jnp.int32), tile_ea.astype(jnp.int32), tile_eb.astype(jnp.int32), n_active.astype(jnp.int32)


def kernel(x, mem, norm_mix, w_in, w_pool, pool_scale, w_out, norm_cross, norm_mem, w_q_mem, w_k_mem,
           w_v_mem, w_o_mem, norm_ffn, w_group, w_expert, w_gate, w_up, w_down, norm_final):
    b, s, d = x.shape
    mem_len = mem.shape[1]
    depth = norm_mix.shape[0]
    pool_width = pool_scale.shape[1]
    attn_width = w_out.shape[1] - pool_width
    bf = jnp.bfloat16
    n = b * s

    assert depth == 1, "single-layer problem: the final RMSNorm is fused into the combine kernel"
    l = 0
    h = x.reshape(n, d)
    u, qkv = _in_proj(h, norm_mix[l][None], w_in[l].astype(bf),
                      pool_width=pool_width, attn_width=attn_width, tm=_tile(n, 512))
    tq = _tile(s, 2 * MXU_DIM_V7X)
    sub = tq // 2
    attn = _sb_attention(qkv.reshape(b, s, 3 * attn_width), attn_width=attn_width, tq=tq, sub=sub, win=2 * sub)
    kmem, vmem = _mem_kv(mem.reshape(b * mem_len, d), norm_mem[l][None],
                         w_k_mem[l].astype(bf), w_v_mem[l].astype(bf), tm=mem_len)

    w_router = jnp.concatenate([w_group[l], w_expert[l]], axis=1).T
    w_router = jnp.pad(w_router, ((0, ROUTER_ROWS - w_router.shape[0]), (0, 0)))
    wr_hi = w_router.astype(bf)
    wr_lo = (w_router - wr_hi.astype(jnp.float32)).astype(bf)
    h, xrow, bucket, rank, counts = _mix_cross(
        h, u, attn.reshape(n, attn_width), w_pool[l].astype(bf), pool_scale[l][None], w_out[l].astype(bf),
        norm_cross[l][None], w_q_mem[l].astype(bf), kmem, vmem, w_o_mem[l].astype(bf),
        norm_ffn[l][None], wr_hi, wr_lo, tm=_tile(s, 512), seq=s, mem_len=mem_len)

    tm_e = _tile(n, MXU_DIM_V7X)
    n_tiles_e = -(-(n + N_BUCKETS * (tm_e - 1)) // tm_e)
    pos, tile_ea, tile_eb, n_active = _sorted_layout(
        bucket.reshape(n), rank.reshape(n), counts[:N_BUCKETS, 0].astype(jnp.int32),
        tm_e=tm_e, n_tiles_e=n_tiles_e)
    xs = _dispatch(pos, xrow, n_tiles_e * tm_e, tm=_tile(n, 512))
    y_sorted = _experts(tile_ea, tile_eb, n_active[None], xs, w_gate[l].astype(bf), w_up[l].astype(bf),
                        w_down[l].astype(bf), tm=tm_e)
    out = _combine(pos, h, y_sorted, norm_final[None], tm=_tile(n, 512))
    return out.reshape(b, s, d)
```

```python
import functools
import math

import jax
import jax.numpy as jnp
from jax import lax
from jax.experimental import pallas as pl
from jax.experimental.pallas import tpu as pltpu

RMS_EPS = 1e-6
POOL_WINDOWS = (2, 4, 8, 16)
SB_HEAD_DIM = 64
MEM_HEADS = 4
N_GROUPS = 4
EXPERTS_PER_GROUP = 4
N_EXPERTS = N_GROUPS * EXPERTS_PER_GROUP

LANES_V7X = 128
MXU_DIM_V7X = 256
VMEM_LIMIT_BYTES = 56 * 1024 * 1024

LOG2E = 1.4426950408889634
UNDERFLOW_EXPONENT = 104.0
MASKED_LOGIT = -1e30
POOL_HALO = 16

PAIRS = tuple((a, b) for a in range(EXPERTS_PER_GROUP) for b in range(a + 1, EXPERTS_PER_GROUP))
N_BUCKETS = N_GROUPS * len(PAIRS)
ROUTER_ROWS = 32
ROUTER_LANES = LANES_V7X
PAYLOAD_LANES = LANES_V7X
assert N_GROUPS + N_EXPERTS <= ROUTER_ROWS and N_BUCKETS <= ROUTER_ROWS


def _rms(x, gain):
    ms = jnp.mean(x * x, axis=-1, keepdims=True)
    return x * lax.rsqrt(ms + RMS_EPS) * gain


def _dot(a, b):
    return jnp.dot(a, b, preferred_element_type=jnp.float32)


def _dot_nt(a, b):
    return lax.dot_general(a, b, (((1,), (1,)), ((), ())), preferred_element_type=jnp.float32)


def _in_proj_kernel(x_ref, g_ref, w_ref, u_ref, qkv_ref, *, pool_width, attn_width, q_scale):
    xn = _rms(x_ref[...], g_ref[...]).astype(jnp.bfloat16)
    proj = _dot(xn, w_ref[...])
    u_ref[...] = proj[:, :pool_width]
    q = proj[:, pool_width:pool_width + attn_width] * q_scale
    qkv_ref[:, :attn_width] = q.astype(jnp.bfloat16)
    qkv_ref[:, attn_width:] = proj[:, pool_width + attn_width:].astype(jnp.bfloat16)


def _in_proj(x2, gain, w_in_bf16, *, pool_width, attn_width, tm):
    n, d = x2.shape
    in_width = w_in_bf16.shape[1]
    kern = functools.partial(_in_proj_kernel, pool_width=pool_width, attn_width=attn_width,
                             q_scale=1.0 / math.sqrt(SB_HEAD_DIM))
    return pl.pallas_call(
        kern,
        grid=(n // tm,),
        in_specs=[
            pl.BlockSpec((tm, d), lambda i: (i, 0)),
            pl.BlockSpec((1, d), lambda i: (0, 0)),
            pl.BlockSpec((d, in_width), lambda i: (0, 0)),
        ],
        out_specs=[
            pl.BlockSpec((tm, pool_width), lambda i: (i, 0)),
            pl.BlockSpec((tm, 3 * attn_width), lambda i: (i, 0)),
        ],
        out_shape=[
            jax.ShapeDtypeStruct((n, pool_width), jnp.float32),
            jax.ShapeDtypeStruct((n, 3 * attn_width), jnp.bfloat16),
        ],
        compiler_params=pltpu.CompilerParams(
            dimension_semantics=("parallel",), vmem_limit_bytes=VMEM_LIMIT_BYTES),
        name="in_proj",
    )(x2, gain, w_in_bf16)


def _sb_tile(qh, kb, vb, tri, c, tail_mask):
    z = _dot_nt(qh, kb)
    if tail_mask is not None:
        m = tail_mask.shape[1]
        tail = jnp.where(tail_mask, z[:, -m:], MASKED_LOGIT)
        z = tail if m == z.shape[1] else jnp.concatenate([z[:, :-m], tail], axis=1)
    sp = (jnp.maximum(z, 0.0) + jnp.log(1.0 + jnp.exp2(jnp.abs(z) * (-LOG2E)))).astype(jnp.bfloat16)
    n = tri.shape[0]
    chunks, total = [], None
    for k0 in range(z.shape[1] - n, -1, -n):
        cum_k = _dot(sp[:, k0:k0 + n], tri)
        if total is not None:
            cum_k = cum_k + total
        total = cum_k[:, :1]
        chunks.insert(0, cum_k)
    cum = chunks[0] if len(chunks) == 1 else jnp.concatenate(chunks, axis=1)
    t = jnp.minimum(z - cum, 0.0)
    w = jnp.exp(t if c is None else t - c)
    return (total if c is None else c + total), _dot(w.astype(jnp.bfloat16), vb)


def _sb_attn_kernel(q_ref, k_ref, v_ref, tri_w_ref, tri_b_ref, o_ref, *, tq, sub, win):
    qi = pl.program_id(2)
    n_sub = tq // sub
    lane = lax.broadcasted_iota(jnp.int32, (tq, LANES_V7X), 1)
    head0 = lane < SB_HEAD_DIM
    q2 = q_ref[...]
    zero = jnp.zeros_like(q2)
    q_heads = (jnp.where(head0, q2, zero), jnp.where(head0, zero, q2))
    q_st = jnp.concatenate(q_heads, axis=0)

    def kv_block(j):
        start = pl.multiple_of(j * tq, tq)
        return k_ref[pl.ds(start, tq), :], v_ref[pl.ds(start, tq), :]

    def stacked_iota(rows, cols):
        row = lax.broadcasted_iota(jnp.int32, (2 * rows, cols), 0)
        col = lax.broadcasted_iota(jnp.int32, (2 * rows, cols), 1)
        return jnp.where(row >= rows, row - rows, row), col

    def first_block():
        row, col = stacked_iota(tq, tq)
        kb, vb = kv_block(0)
        return _sb_tile(q_st, kb, vb, tri_b_ref[...], None, col < row)

    def windows():
        row, col = stacked_iota(sub, sub)
        causal = col < row
        tri = tri_w_ref[...]
        parts = []
        for hf in range(n_sub):
            start = pl.multiple_of((qi * n_sub + hf + 1) * sub - win, sub)
            kw = k_ref[pl.ds(start, win), :]
            vw = v_ref[pl.ds(start, win), :]
            q_sub = jnp.concatenate([qh[hf * sub:(hf + 1) * sub] for qh in q_heads], axis=0)
            parts.append(_sb_tile(q_sub, kw, vw, tri, None, causal))
        return tuple(jnp.concatenate([p[i][h * sub:(h + 1) * sub] for h in range(2) for p in parts], axis=0)
                     for i in range(2))

    carry = lax.cond(qi == 0, first_block, windows)

    def live(carry):
        return jnp.min(carry[0], axis=0, keepdims=True)[0, 0] < UNDERFLOW_EXPONENT

    def add_block(carry, j, tail_mask):
        kb, vb = kv_block(j)
        c, acc = _sb_tile(q_st, kb, vb, tri_b_ref[...], carry[0], tail_mask)
        return c, carry[1] + acc

    def partial_block(carry):
        row, col = stacked_iota(tq, tq)
        fresh = col < (row // sub + 1) * sub + (tq - win)
        return add_block(carry, qi - 1, fresh)

    carry = lax.cond(jnp.logical_and(qi > 0, live(carry)), partial_block, lambda cr: cr, carry)

    def cond(state):
        step, alive, _ = state
        return jnp.logical_and(step < qi - 1, alive > 0)

    def body(state):
        step, _, carry = state
        carry = add_block(carry, qi - 2 - step, None)
        return step + 1, live(carry).astype(jnp.int32), carry

    _, _, carry = lax.while_loop(cond, body, (jnp.int32(0), live(carry).astype(jnp.int32), carry))
    acc = carry[1]
    o_ref[...] = jnp.where(head0, acc[:tq], acc[tq:]).astype(o_ref.dtype)


def _sb_attention(qkv3, *, attn_width, tq, sub, win):
    b, s, _ = qkv3.shape
    assert tq % sub == 0 and win % sub == 0 and sub <= win <= tq + sub, (tq, sub, win)
    n_pairs = attn_width // LANES_V7X

    def tri(n):
        return (lax.broadcasted_iota(jnp.int32, (n, n), 0)
                >= lax.broadcasted_iota(jnp.int32, (n, n), 1)).astype(jnp.bfloat16)

    kern = functools.partial(_sb_attn_kernel, tq=tq, sub=sub, win=win)
    const2 = lambda bi, hp, qi: (0, 0)
    return pl.pallas_call(
        kern,
        grid=(b, n_pairs, s // tq),
        in_specs=[
            pl.BlockSpec((None, tq, LANES_V7X), lambda bi, hp, qi: (bi, qi, hp)),
            pl.BlockSpec((None, s, LANES_V7X), lambda bi, hp, qi: (bi, 0, n_pairs + hp)),
            pl.BlockSpec((None, s, LANES_V7X), lambda bi, hp, qi: (bi, 0, 2 * n_pairs + hp)),
            pl.BlockSpec((win, win), const2),
            pl.BlockSpec((tq, tq), const2),
        ],
        out_specs=pl.BlockSpec((None, tq, LANES_V7X), lambda bi, hp, qi: (bi, qi, hp)),
        out_shape=jax.ShapeDtypeStruct((b, s, attn_width), jnp.bfloat16),
        compiler_params=pltpu.CompilerParams(
            dimension_semantics=("parallel", "parallel", "parallel"),
            vmem_limit_bytes=VMEM_LIMIT_BYTES),
        name="sb_attention",
    )(qkv3, qkv3, qkv3, tri(win), tri(tq))


def _mem_kv_kernel(m_ref, g_ref, wk_ref, wv_ref, k_ref, v_ref):
    mn = _rms(m_ref[...], g_ref[...]).astype(jnp.bfloat16)
    k_ref[...] = _dot(mn, wk_ref[...]).astype(jnp.bfloat16)
    v_ref[...] = _dot(mn, wv_ref[...]).astype(jnp.bfloat16)


def _mem_kv(mem2, gain, wk, wv, *, tm):
    n, d = mem2.shape
    row = pl.BlockSpec((tm, d), lambda i: (i, 0))
    full = pl.BlockSpec((d, d), lambda i: (0, 0))
    return pl.pallas_call(
        _mem_kv_kernel,
        grid=(n // tm,),
        in_specs=[row, pl.BlockSpec((1, d), lambda i: (0, 0)), full, full],
        out_specs=[row, row],
        out_shape=[jax.ShapeDtypeStruct((n, d), jnp.bfloat16)] * 2,
        compiler_params=pltpu.CompilerParams(
            dimension_semantics=("parallel",), vmem_limit_bytes=VMEM_LIMIT_BYTES),
        name="mem_kv",
    )(mem2, gain, wk, wv)


def _route_t(logits):
    neg = jnp.float32(-jnp.inf)
    big = jnp.float32(ROUTER_ROWS)
    row = lax.broadcasted_iota(jnp.int32, logits.shape, 0).astype(jnp.float32)

    def col_max(mask):
        return jnp.max(jnp.where(mask, logits, neg), axis=0, keepdims=True)

    def first_argmax(mask, mx):
        return jnp.min(jnp.where(mask & (logits == mx), row, big), axis=0, keepdims=True)

    gmask = row < N_GROUPS
    gmax = col_max(gmask)
    gsum = jnp.sum(jnp.where(gmask, jnp.exp(logits - gmax), 0.0), axis=0, keepdims=True)
    g_gate = 1.0 / gsum
    g_idx = first_argmax(gmask, gmax)

    lo = N_GROUPS + EXPERTS_PER_GROUP * g_idx
    emask = (row >= lo) & (row < lo + EXPERTS_PER_GROUP)
    m1 = col_max(emask)
    i1 = first_argmax(emask, m1)
    mask2 = emask & (row != i1)
    m2 = col_max(mask2)
    i2 = first_argmax(mask2, m2)
    esum = jnp.sum(jnp.where(emask, jnp.exp(logits - m1), 0.0), axis=0, keepdims=True)
    p1 = 1.0 / esum
    p2 = jnp.exp(m2 - m1) / esum
    tot = p1 + p2
    w1 = g_gate * (p1 / tot)
    w2 = g_gate * (p2 / tot)

    first = i1 < i2
    la = jnp.where(first, i1, i2) - lo
    lb = jnp.where(first, i2, i1) - lo
    pair = la * (2 * EXPERTS_PER_GROUP - 1 - la) * 0.5 + (lb - la - 1.0)
    bucket = g_idx * len(PAIRS) + pair
    return bucket, jnp.where(first, w1, w2), jnp.where(first, w2, w1)


def _mix_cross_kernel(x_ref, u_ref, halo_ref, a_ref, wpbd_ref, ps_ref, wo_ref, gc_ref,
                      wq_ref, km_ref, vm_ref, wom_ref, gf_ref, wr_ref, su_ref,
                      h_ref, xrow_ref, bucket_ref, rank_ref, cnt_ref, run_ref, *, tm, tiles_per_seq):
    i = pl.program_id(0)
    pool_width = u_ref.shape[1]
    gw = pool_width // len(POOL_WINDOWS)
    tile_in_seq = i % tiles_per_seq
    first = tile_in_seq == 0
    pos = tile_in_seq * tm + lax.broadcasted_iota(jnp.int32, (tm, 1), 0)

    halo = jnp.where(first, 0.0, halo_ref[...])
    u = u_ref[...]
    pooled = []
    for g, w in enumerate(POOL_WINDOWS):
        ug = u[:, g * gw:(g + 1) * gw]
        ext = jnp.concatenate([halo[:, g * gw:(g + 1) * gw], ug], axis=0)
        shift = 1
        while shift < w:
            ext = ext + pltpu.roll(ext, shift, 0)
            shift *= 2
        win = ext[POOL_HALO:, :]
        inv_count = 1.0 / jnp.minimum(pos + 1, w).astype(jnp.float32)
        pooled.append((win * inv_count - ug).astype(jnp.bfloat16))

    pool_out = _dot(jnp.concatenate(pooled, axis=1), wpbd_ref[...]) * ps_ref[...]
    mixed = jnp.concatenate([pool_out.astype(jnp.bfloat16), a_ref[...]], axis=1)
    h1 = x_ref[...] + _dot(mixed, wo_ref[...])

    hn = _rms(h1, gc_ref[...]).astype(jnp.bfloat16)
    d = h1.shape[1]
    hd = d // MEM_HEADS
    q = (_dot(hn, wq_ref[...]) * (1.0 / math.sqrt(hd))).astype(jnp.bfloat16)
    outs = []
    for hh in range(MEM_HEADS):
        sl = slice(hh * hd, (hh + 1) * hd)
        s = _dot_nt(q[:, sl], km_ref[:, sl])
        e = jnp.exp(s - jnp.max(s, axis=-1, keepdims=True))
        p = e * (1.0 / jnp.sum(e, axis=-1, keepdims=True))
        outs.append(_dot(p.astype(jnp.bfloat16), vm_ref[:, sl]))
    o = jnp.concatenate(outs, axis=-1).astype(jnp.bfloat16)
    h2 = h1 + _dot(o, wom_ref[...])
    h_ref[...] = h2

    xn = _rms(h2, gf_ref[...])
    xrow_ref[:, :d] = xn
    x_hi = xn.astype(jnp.bfloat16)
    x_lo = (xn - x_hi.astype(jnp.float32)).astype(jnp.bfloat16)
    both = _dot(x_hi, wr_ref[...])
    logits_tm = both[:, :ROUTER_LANES] + both[:, ROUTER_LANES:] + _dot(x_lo, wr_ref[:, :ROUTER_LANES])
    logits = logits_tm.T[:ROUTER_ROWS]
    bucket, w_a, w_b = _route_t(logits)

    @pl.when(i == 0)
    def _():
        run_ref[...] = jnp.zeros_like(run_ref)

    brow = lax.broadcasted_iota(jnp.int32, (ROUTER_ROWS, tm), 0).astype(jnp.float32)
    onehot = (brow == bucket).astype(jnp.float32)
    before = _dot(onehot.astype(jnp.bfloat16), su_ref[...])
    run = run_ref[...]
    rank = jnp.sum(onehot * (before + run[:, :1]), axis=0, keepdims=True)
    run = run + jnp.sum(onehot, axis=1, keepdims=True)
    run_ref[...] = run
    cnt_ref[...] = run
    bucket_ref[...] = bucket.astype(jnp.int32)
    rank_ref[...] = rank.astype(jnp.int32)

    prow = lax.broadcasted_iota(jnp.int32, (PAYLOAD_LANES, tm), 0)
    payload_t = jnp.where(prow == 0, w_a, jnp.where(prow == 1, w_b, 0.0))
    xrow_ref[:, d:] = payload_t.T


def _mix_cross(x2, u, attn, w_pool_bd, pool_scale, w_out, g_cross, wq, kmem, vmem, wom, g_ffn, w_router,
               *, tm, seq, mem_len):
    n, d = x2.shape
    pool_width = u.shape[1]
    attn_width = attn.shape[1]
    tiles_per_seq = seq // tm
    n_tiles = n // tm
    halo_blocks = tm // POOL_HALO
    su = (lax.broadcasted_iota(jnp.int32, (tm, tm), 0)
          < lax.broadcasted_iota(jnp.int32, (tm, tm), 1)).astype(jnp.bfloat16)
    kern = functools.partial(_mix_cross_kernel, tm=tm, tiles_per_seq=tiles_per_seq)
    const2 = lambda i: (0, 0)
    mem_map = lambda i: (i // tiles_per_seq, 0)
    return pl.pallas_call(
        kern,
        grid=(n_tiles,),
        in_specs=[
            pl.BlockSpec((tm, d), lambda i: (i, 0)),
            pl.BlockSpec((tm, pool_width), lambda i: (i, 0)),
            pl.BlockSpec((POOL_HALO, pool_width), lambda i: (jnp.maximum(i * halo_blocks - 1, 0), 0)),
            pl.BlockSpec((tm, attn_width), lambda i: (i, 0)),
            pl.BlockSpec(w_pool_bd.shape, const2),
            pl.BlockSpec((1, pool_width), const2),
            pl.BlockSpec(w_out.shape, const2),
            pl.BlockSpec((1, d), const2),
            pl.BlockSpec((d, d), const2),
            pl.BlockSpec((mem_len, d), mem_map),
            pl.BlockSpec((mem_len, d), mem_map),
            pl.BlockSpec((d, d), const2),
            pl.BlockSpec((1, d), const2),
            pl.BlockSpec((d, 2 * ROUTER_LANES), const2),
            pl.BlockSpec((tm, tm), const2),
        ],
        out_specs=[
            pl.BlockSpec((tm, d), lambda i: (i, 0)),
            pl.BlockSpec((tm, d + PAYLOAD_LANES), lambda i: (i, 0)),
            pl.BlockSpec((None, 1, tm), lambda i: (i, 0, 0)),
            pl.BlockSpec((None, 1, tm), lambda i: (i, 0, 0)),
            pl.BlockSpec((ROUTER_ROWS, LANES_V7X), const2),
        ],
        out_shape=[
            jax.ShapeDtypeStruct((n, d), jnp.float32),
            jax.ShapeDtypeStruct((n, d + PAYLOAD_LANES), jnp.float32),
            jax.ShapeDtypeStruct((n_tiles, 1, tm), jnp.int32),
            jax.ShapeDtypeStruct((n_tiles, 1, tm), jnp.int32),
            jax.ShapeDtypeStruct((ROUTER_ROWS, LANES_V7X), jnp.float32),
        ],
        scratch_shapes=[pltpu.VMEM((ROUTER_ROWS, LANES_V7X), jnp.float32)],
        compiler_params=pltpu.CompilerParams(
            dimension_semantics=("arbitrary",), vmem_limit_bytes=VMEM_LIMIT_BYTES),
        name="mix_cross",
    )(x2, u, u, attn, w_pool_bd, pool_scale, w_out, g_cross, wq, kmem, vmem, wom, g_ffn, w_router, su)


def _dispatch_kernel(pos_ref, x_ref, xs_init_ref, xs_ref, sem, *, tm):
    del xs_init_ref
    base = pl.program_id(0) * tm

    def row_copy(r, p):
        return pltpu.make_async_copy(x_ref.at[pl.ds(r, 1), :], xs_ref.at[pl.ds(p, 1), :], sem)

    for r in range(tm):
        row_copy(r, pos_ref[base + r]).start()
    pltpu.make_async_copy(x_ref, xs_ref.at[pl.ds(0, tm), :], sem).wait()


def _dispatch(pos, xrow, n_pad, *, tm):
    n, width = xrow.shape
    xs_init = jnp.zeros((n_pad, width), xrow.dtype)
    return pl.pallas_call(
        functools.partial(_dispatch_kernel, tm=tm),
        grid_spec=pltpu.PrefetchScalarGridSpec(
            num_scalar_prefetch=1,
            grid=(n // tm,),
            in_specs=[
                pl.BlockSpec((tm, width), lambda i, pos: (i, 0)),
                pl.BlockSpec(memory_space=pl.ANY),
            ],
            out_specs=pl.BlockSpec(memory_space=pl.ANY),
            scratch_shapes=[pltpu.SemaphoreType.DMA],
        ),
        out_shape=jax.ShapeDtypeStruct((n_pad, width), xrow.dtype),
        input_output_aliases={2: 0},
        compiler_params=pltpu.CompilerParams(
            dimension_semantics=("arbitrary",), vmem_limit_bytes=VMEM_LIMIT_BYTES),
        name="moe_dispatch",
    )(pos, xrow, xs_init)


def _expert_kernel(ea_ref, eb_ref, na_ref, xs_ref, wga_ref, wua_ref, wda_ref, wgb_ref, wub_ref, wdb_ref,
                   y_ref, *, d):
    del ea_ref, eb_ref
    active = pl.program_id(0) < na_ref[0]

    @pl.when(jnp.logical_not(active))
    def _():
        y_ref[...] = jnp.zeros_like(y_ref)

    @pl.when(active)
    def _():
        x = xs_ref[:, :d].astype(jnp.bfloat16)

        def mlp(wg_ref, wu_ref, wd_ref):
            gate = _dot(x, wg_ref[...])
            up = _dot(x, wu_ref[...])
            hmid = (gate * (1.0 / (1.0 + jnp.exp(-gate)))) * up
            return _dot(hmid.astype(jnp.bfloat16), wd_ref[...])

        y_ref[...] = (xs_ref[:, d:d + 1] * mlp(wga_ref, wua_ref, wda_ref)
                      + xs_ref[:, d + 1:d + 2] * mlp(wgb_ref, wub_ref, wdb_ref))


def _experts(tile_ea, tile_eb, n_active, xs, wg, wu, wd, *, tm):
    n_pad, width = xs.shape
    d, ff = wg.shape[1], wg.shape[2]
    row_map = lambda j, ea, eb, na: (jnp.minimum(j, na[0] - 1), 0)
    a_map = lambda j, ea, eb, na: (ea[j], 0, 0)
    b_map = lambda j, ea, eb, na: (eb[j], 0, 0)
    return pl.pallas_call(
        functools.partial(_expert_kernel, d=d),
        grid_spec=pltpu.PrefetchScalarGridSpec(
            num_scalar_prefetch=3,
            grid=(n_pad // tm,),
            in_specs=[
                pl.BlockSpec((tm, width), row_map),
                pl.BlockSpec((None, d, ff), a_map),
                pl.BlockSpec((None, d, ff), a_map),
                pl.BlockSpec((None, ff, d), a_map),
                pl.BlockSpec((None, d, ff), b_map),
                pl.BlockSpec((None, d, ff), b_map),
                pl.BlockSpec((None, ff, d), b_map),
            ],
            out_specs=pl.BlockSpec((tm, d), lambda j, ea, eb, na: (j, 0)),
        ),
        out_shape=jax.ShapeDtypeStruct((n_pad, d), jnp.float32),
        compiler_params=pltpu.CompilerParams(
            dimension_semantics=("arbitrary",), vmem_limit_bytes=VMEM_LIMIT_BYTES),
        name="moe_experts",
    )(tile_ea, tile_eb, n_active, xs, wg, wu, wd, wg, wu, wd)


def _combine_kernel(pos_ref, h_ref, y_ref, g_ref, o_ref, ybuf, sems, *, tm, n_tiles):
    i = pl.program_id(0)

    def row_copy(slot, r, p):
        return pltpu.make_async_copy(y_ref.at[pl.ds(p, 1), :], ybuf.at[slot, pl.ds(r, 1), :], sems.at[slot])

    @pl.when(i < n_tiles)
    def _():
        slot = i % 2
        for r in range(tm):
            row_copy(slot, r, pos_ref[i * tm + r]).start()

    @pl.when(i > 0)
    def _():
        slot = (i - 1) % 2
        pltpu.make_async_copy(y_ref.at[pl.ds(0, tm), :], ybuf.at[slot], sems.at[slot]).wait()
        o_ref[...] = _rms(h_ref[...] + ybuf[slot], g_ref[...])


def _combine(pos, h, y_sorted, g_final, *, tm):
    n, d = h.shape
    n_tiles = n // tm
    prev_tile = lambda i, pos: (jnp.maximum(i - 1, 0), 0)
    return pl.pallas_call(
        functools.partial(_combine_kernel, tm=tm, n_tiles=n_tiles),
        grid_spec=pltpu.PrefetchScalarGridSpec(
            num_scalar_prefetch=1,
            grid=(n_tiles + 1,),
            in_specs=[
                pl.BlockSpec((tm, d), prev_tile),
                pl.BlockSpec(memory_space=pl.ANY),
                pl.BlockSpec((1, d), lambda i, pos: (0, 0)),
            ],
            out_specs=pl.BlockSpec((tm, d), prev_tile),
            scratch_shapes=[pltpu.VMEM((2, tm, d), jnp.float32), pltpu.SemaphoreType.DMA((2,))],
        ),
        out_shape=jax.ShapeDtypeStruct((n, d), jnp.float32),
        compiler_params=pltpu.CompilerParams(
            dimension_semantics=("arbitrary",), vmem_limit_bytes=VMEM_LIMIT_BYTES),
        name="moe_combine",
    )(pos, h, y_sorted, g_final)


def _tile(n, pref):
    t = min(n, pref)
    assert n % t == 0, (n, t)
    return t


def _sorted_layout(bucket, rank, counts, *, tm_e, n_tiles_e):
    seg_tiles = (counts + tm_e - 1) // tm_e
    seg_end = jnp.cumsum(seg_tiles)
    seg_start = seg_end - seg_tiles
    pos = (seg_start * tm_e)[bucket] + rank
    n_active = seg_end[-1]
    tile = jnp.minimum(jnp.arange(n_tiles_e, dtype=jnp.int32), n_active - 1)
    tile_bucket = jnp.sum((tile[:, None] >= seg_end[None, :]).astype(jnp.int32), axis=1)
    group, pair = tile_bucket // len(PAIRS), tile_bucket % len(PAIRS)
    pair_a = jnp.array([p[0] for p in PAIRS], jnp.int32)
    pair_b = jnp.array([p[1] for p in PAIRS], jnp.int32)
    tile_ea = group * EXPERTS_PER_GROUP + pair_a[pair]
    tile_eb = group * EXPERTS_PER_GROUP + pair_b[pair]
    return pos.astype(jnp.int32), tile_ea.astype(jnp.int32), tile_eb.astype(jnp.int32), n_active.astype(jnp.int32)


def kernel(x, mem, norm_mix, w_in, w_pool, pool_scale, w_out, norm_cross, norm_mem, w_q_mem, w_k_mem,
           w_v_mem, w_o_mem, norm_ffn, w_group, w_expert, w_gate, w_up, w_down, norm_final):
    b, s, d = x.shape
    mem_len = mem.shape[1]
    depth = norm_mix.shape[0]
    pool_width = pool_scale.shape[1]
    attn_width = w_out.shape[1] - pool_width
    bf = jnp.bfloat16
    n = b * s

    assert depth == 1, "single-layer problem: the final RMSNorm is fused into the combine kernel"
    l = 0
    h = x.reshape(n, d)
    u, qkv = _in_proj(h, norm_mix[l][None], w_in[l].astype(bf),
                      pool_width=pool_width, attn_width=attn_width, tm=_tile(n, 512))
    tq = _tile(s, MXU_DIM_V7X)
    sub = tq
    attn = _sb_attention(qkv.reshape(b, s, 3 * attn_width), attn_width=attn_width, tq=tq, sub=sub, win=2 * sub)
    kmem, vmem = _mem_kv(mem.reshape(b * mem_len, d), norm_mem[l][None],
                         w_k_mem[l].astype(bf), w_v_mem[l].astype(bf), tm=mem_len)

    w_router = jnp.concatenate([w_group[l], w_expert[l]], axis=1)
    w_router = jnp.pad(w_router, ((0, 0), (0, ROUTER_LANES - w_router.shape[1])))
    wr_hi = w_router.astype(bf)
    wr_lo = (w_router - wr_hi.astype(jnp.float32)).astype(bf)
    w_pool_bd = jax.scipy.linalg.block_diag(*w_pool[l]).astype(bf)
    h, xrow, bucket, rank, counts = _mix_cross(
        h, u, attn.reshape(n, attn_width), w_pool_bd, pool_scale[l][None], w_out[l].astype(bf),
        norm_cross[l][None], w_q_mem[l].astype(bf), kmem, vmem, w_o_mem[l].astype(bf),
        norm_ffn[l][None], jnp.concatenate([wr_hi, wr_lo], axis=1), tm=_tile(s, 512), seq=s, mem_len=mem_len)

    tm_e = _tile(n, MXU_DIM_V7X)
    n_tiles_e = -(-(n + N_BUCKETS * (tm_e - 1)) // tm_e)
    pos, tile_ea, tile_eb, n_active = _sorted_layout(
        bucket.reshape(n), rank.reshape(n), counts[:N_BUCKETS, 0].astype(jnp.int32),
        tm_e=tm_e, n_tiles_e=n_tiles_e)
    xs = _dispatch(pos, xrow, n_tiles_e * tm_e, tm=_tile(n, 512))
    y_sorted = _experts(tile_ea, tile_eb, n_active[None], xs, w_gate[l].astype(bf), w_up[l].astype(bf),
                        w_down[l].astype(bf), tm=tm_e)
    out = _combine(pos, h, y_sorted, norm_final[None], tm=_tile(n, 512))
    return out.reshape(b, s, d)
```

```python
import functools
import math

import jax
import jax.numpy as jnp
from jax import lax
from jax.experimental import pallas as pl
from jax.experimental.pallas import tpu as pltpu

RMS_EPS = 1e-6
POOL_WINDOWS = (2, 4, 8, 16)
SB_HEAD_DIM = 64
MEM_HEADS = 4
N_GROUPS = 4
EXPERTS_PER_GROUP = 4
N_EXPERTS = N_GROUPS * EXPERTS_PER_GROUP

LANES_V7X = 128
MXU_DIM_V7X = 256
VMEM_LIMIT_BYTES = 56 * 1024 * 1024

LOG2E = 1.4426950408889634
UNDERFLOW_EXPONENT = 104.0
MASKED_LOGIT = -1e30
POOL_HALO = 16

PAIRS = tuple((a, b) for a in range(EXPERTS_PER_GROUP) for b in range(a + 1, EXPERTS_PER_GROUP))
N_BUCKETS = N_GROUPS * len(PAIRS)
ROUTER_ROWS = 32
ROUTER_LANES = LANES_V7X
PAYLOAD_LANES = LANES_V7X
assert N_GROUPS + N_EXPERTS <= ROUTER_ROWS and N_BUCKETS <= ROUTER_ROWS


def _rms(x, gain):
    ms = jnp.mean(x * x, axis=-1, keepdims=True)
    return x * lax.rsqrt(ms + RMS_EPS) * gain


def _dot(a, b):
    return jnp.dot(a, b, preferred_element_type=jnp.float32)


def _dot_nt(a, b):
    return lax.dot_general(a, b, (((1,), (1,)), ((), ())), preferred_element_type=jnp.float32)


def _in_proj_kernel(x_ref, g_ref, w_ref, u_ref, qkv_ref, *, pool_width, attn_width, q_scale):
    xn = _rms(x_ref[...], g_ref[...]).astype(jnp.bfloat16)
    proj = _dot(xn, w_ref[...])
    u_ref[...] = proj[:, :pool_width]
    q = proj[:, pool_width:pool_width + attn_width] * q_scale
    qkv_ref[:, :attn_width] = q.astype(jnp.bfloat16)
    qkv_ref[:, attn_width:] = proj[:, pool_width + attn_width:].astype(jnp.bfloat16)


def _in_proj(x2, gain, w_in_bf16, *, pool_width, attn_width, tm):
    n, d = x2.shape
    in_width = w_in_bf16.shape[1]
    kern = functools.partial(_in_proj_kernel, pool_width=pool_width, attn_width=attn_width,
                             q_scale=1.0 / math.sqrt(SB_HEAD_DIM))
    return pl.pallas_call(
        kern,
        grid=(n // tm,),
        in_specs=[
            pl.BlockSpec((tm, d), lambda i: (i, 0)),
            pl.BlockSpec((1, d), lambda i: (0, 0)),
            pl.BlockSpec((d, in_width), lambda i: (0, 0)),
        ],
        out_specs=[
            pl.BlockSpec((tm, pool_width), lambda i: (i, 0)),
            pl.BlockSpec((tm, 3 * attn_width), lambda i: (i, 0)),
        ],
        out_shape=[
            jax.ShapeDtypeStruct((n, pool_width), jnp.float32),
            jax.ShapeDtypeStruct((n, 3 * attn_width), jnp.bfloat16),
        ],
        compiler_params=pltpu.CompilerParams(
            dimension_semantics=("parallel",), vmem_limit_bytes=VMEM_LIMIT_BYTES),
        name="in_proj",
    )(x2, gain, w_in_bf16)


def _sb_tile(qh, kb, vb, tri, c, tail_mask):
    z = _dot_nt(qh, kb)
    if tail_mask is not None:
        m = tail_mask.shape[1]
        tail = jnp.where(tail_mask, z[:, -m:], MASKED_LOGIT)
        z = tail if m == z.shape[1] else jnp.concatenate([z[:, :-m], tail], axis=1)
    sp = (jnp.maximum(z, 0.0) + jnp.log(1.0 + jnp.exp2(jnp.abs(z) * (-LOG2E)))).astype(jnp.bfloat16)
    n = tri.shape[0]
    chunks, total = [], None
    for k0 in range(z.shape[1] - n, -1, -n):
        cum_k = _dot(sp[:, k0:k0 + n], tri)
        if total is not None:
            cum_k = cum_k + total
        total = cum_k[:, :1]
        chunks.insert(0, cum_k)
    cum = chunks[0] if len(chunks) == 1 else jnp.concatenate(chunks, axis=1)
    t = jnp.minimum(z - cum, 0.0)
    w = jnp.exp(t if c is None else t - c)
    return (total if c is None else c + total), _dot(w.astype(jnp.bfloat16), vb)


def _sb_attn_kernel(q_ref, k_ref, v_ref, tri_ref, o_ref, *, tq, sub, win):
    qi = pl.program_id(2)
    n_sub = tq // sub
    lane = lax.broadcasted_iota(jnp.int32, (tq, LANES_V7X), 1)
    head0 = lane < SB_HEAD_DIM
    q2 = q_ref[...]
    zero = jnp.zeros_like(q2)
    q_heads = (jnp.where(head0, q2, zero), jnp.where(head0, zero, q2))
    q_st = jnp.concatenate(q_heads, axis=0)

    def kv_block(j):
        start = pl.multiple_of(j * tq, tq)
        return k_ref[pl.ds(start, tq), :], v_ref[pl.ds(start, tq), :]

    def stacked_iota(rows, cols):
        row = lax.broadcasted_iota(jnp.int32, (2 * rows, cols), 0)
        col = lax.broadcasted_iota(jnp.int32, (2 * rows, cols), 1)
        return jnp.where(row >= rows, row - rows, row), col

    def first_block():
        row, col = stacked_iota(tq, tq)
        kb, vb = kv_block(0)
        return _sb_tile(q_st, kb, vb, tri_ref[...], None, col < row)

    def windows():
        row, col = stacked_iota(sub, sub)
        causal = col < row
        tri = tri_ref[...]
        parts = []
        for hf in range(n_sub):
            start = pl.multiple_of((qi * n_sub + hf + 1) * sub - win, sub)
            kw = k_ref[pl.ds(start, win), :]
            vw = v_ref[pl.ds(start, win), :]
            q_sub = jnp.concatenate([qh[hf * sub:(hf + 1) * sub] for qh in q_heads], axis=0)
            parts.append(_sb_tile(q_sub, kw, vw, tri, None, causal))
        return tuple(jnp.concatenate([p[i][h * sub:(h + 1) * sub] for h in range(2) for p in parts], axis=0)
                     for i in range(2))

    carry = lax.cond(qi == 0, first_block, windows)

    def live(carry):
        return jnp.min(carry[0], axis=0, keepdims=True)[0, 0] < UNDERFLOW_EXPONENT

    def add_block(carry, j, tail_mask):
        kb, vb = kv_block(j)
        c, acc = _sb_tile(q_st, kb, vb, tri_ref[...], carry[0], tail_mask)
        return c, carry[1] + acc

    def partial_block(carry):
        row, col = stacked_iota(tq, tq)
        fresh = col < (row // sub + 1) * sub + (tq - win)
        return add_block(carry, qi - 1, fresh)

    carry = lax.cond(jnp.logical_and(qi > 0, live(carry)), partial_block, lambda cr: cr, carry)

    def cond(state):
        step, alive, _ = state
        return jnp.logical_and(step < qi - 1, alive > 0)

    def body(state):
        step, _, carry = state
        carry = add_block(carry, qi - 2 - step, None)
        return step + 1, live(carry).astype(jnp.int32), carry

    _, _, carry = lax.while_loop(cond, body, (jnp.int32(0), live(carry).astype(jnp.int32), carry))
    acc = carry[1]
    o_ref[...] = jnp.where(head0, acc[:tq], acc[tq:]).astype(o_ref.dtype)


def _sb_attention(qkv3, *, attn_width, tq, sub, win):
    b, s, _ = qkv3.shape
    assert tq % sub == 0 and win % sub == 0 and sub <= win <= tq + sub, (tq, sub, win)
    n_pairs = attn_width // LANES_V7X
    n = math.gcd(math.gcd(tq, win), MXU_DIM_V7X)
    tri = (lax.broadcasted_iota(jnp.int32, (n, n), 0)
           >= lax.broadcasted_iota(jnp.int32, (n, n), 1)).astype(jnp.bfloat16)

    kern = functools.partial(_sb_attn_kernel, tq=tq, sub=sub, win=win)
    const2 = lambda bi, hp, qi: (0, 0)
    return pl.pallas_call(
        kern,
        grid=(b, n_pairs, s // tq),
        in_specs=[
            pl.BlockSpec((None, tq, LANES_V7X), lambda bi, hp, qi: (bi, qi, hp)),
            pl.BlockSpec((None, s, LANES_V7X), lambda bi, hp, qi: (bi, 0, n_pairs + hp)),
            pl.BlockSpec((None, s, LANES_V7X), lambda bi, hp, qi: (bi, 0, 2 * n_pairs + hp)),
            pl.BlockSpec((n, n), const2),
        ],
        out_specs=pl.BlockSpec((None, tq, LANES_V7X), lambda bi, hp, qi: (bi, qi, hp)),
        out_shape=jax.ShapeDtypeStruct((b, s, attn_width), jnp.bfloat16),
        compiler_params=pltpu.CompilerParams(
            dimension_semantics=("parallel", "parallel", "parallel"),
            vmem_limit_bytes=VMEM_LIMIT_BYTES),
        name="sb_attention",
    )(qkv3, qkv3, qkv3, tri)


def _mem_kv_kernel(m_ref, g_ref, wk_ref, wv_ref, k_ref, v_ref):
    mn = _rms(m_ref[...], g_ref[...]).astype(jnp.bfloat16)
    k_ref[...] = _dot(mn, wk_ref[...]).astype(jnp.bfloat16)
    v_ref[...] = _dot(mn, wv_ref[...]).astype(jnp.bfloat16)


def _mem_kv(mem2, gain, wk, wv, *, tm):
    n, d = mem2.shape
    row = pl.BlockSpec((tm, d), lambda i: (i, 0))
    full = pl.BlockSpec((d, d), lambda i: (0, 0))
    return pl.pallas_call(
        _mem_kv_kernel,
        grid=(n // tm,),
        in_specs=[row, pl.BlockSpec((1, d), lambda i: (0, 0)), full, full],
        out_specs=[row, row],
        out_shape=[jax.ShapeDtypeStruct((n, d), jnp.bfloat16)] * 2,
        compiler_params=pltpu.CompilerParams(
            dimension_semantics=("parallel",), vmem_limit_bytes=VMEM_LIMIT_BYTES),
        name="mem_kv",
    )(mem2, gain, wk, wv)


def _route_t(logits):
    neg = jnp.float32(-jnp.inf)
    big = jnp.float32(ROUTER_ROWS)
    row = lax.broadcasted_iota(jnp.int32, logits.shape, 0).astype(jnp.float32)

    def col_max(mask):
        return jnp.max(jnp.where(mask, logits, neg), axis=0, keepdims=True)

    def first_argmax(mask, mx):
        return jnp.min(jnp.where(mask & (logits == mx), row, big), axis=0, keepdims=True)

    gmask = row < N_GROUPS
    gmax = col_max(gmask)
    gsum = jnp.sum(jnp.where(gmask, jnp.exp(logits - gmax), 0.0), axis=0, keepdims=True)
    g_gate = 1.0 / gsum
    g_idx = first_argmax(gmask, gmax)

    lo = N_GROUPS + EXPERTS_PER_GROUP * g_idx
    emask = (row >= lo) & (row < lo + EXPERTS_PER_GROUP)
    m1 = col_max(emask)
    i1 = first_argmax(emask, m1)
    mask2 = emask & (row != i1)
    m2 = col_max(mask2)
    i2 = first_argmax(mask2, m2)
    esum = jnp.sum(jnp.where(emask, jnp.exp(logits - m1), 0.0), axis=0, keepdims=True)
    p1 = 1.0 / esum
    p2 = jnp.exp(m2 - m1) / esum
    tot = p1 + p2
    w1 = g_gate * (p1 / tot)
    w2 = g_gate * (p2 / tot)

    first = i1 < i2
    la = jnp.where(first, i1, i2) - lo
    lb = jnp.where(first, i2, i1) - lo
    pair = la * (2 * EXPERTS_PER_GROUP - 1 - la) * 0.5 + (lb - la - 1.0)
    bucket = g_idx * len(PAIRS) + pair
    return bucket, jnp.where(first, w1, w2), jnp.where(first, w2, w1)


def _mix_cross_kernel(x_ref, u_ref, halo_ref, a_ref, wpbd_ref, ps_ref, wo_ref, gc_ref,
                      wq_ref, km_ref, vm_ref, wom_ref, gf_ref, wr_ref, su_ref,
                      h_ref, xrow_ref, bucket_ref, rank_ref, cnt_ref, run_ref, *, tm, tiles_per_seq):
    i = pl.program_id(0)
    pool_width = u_ref.shape[1]
    gw = pool_width // len(POOL_WINDOWS)
    tile_in_seq = i % tiles_per_seq
    first = tile_in_seq == 0
    pos = tile_in_seq * tm + lax.broadcasted_iota(jnp.int32, (tm, 1), 0)

    halo = jnp.where(first, 0.0, halo_ref[...])
    u = u_ref[...]
    pooled = []
    for g, w in enumerate(POOL_WINDOWS):
        ug = u[:, g * gw:(g + 1) * gw]
        ext = jnp.concatenate([halo[:, g * gw:(g + 1) * gw], ug], axis=0)
        shift = 1
        while shift < w:
            ext = ext + pltpu.roll(ext, shift, 0)
            shift *= 2
        win = ext[POOL_HALO:, :]
        inv_count = 1.0 / jnp.minimum(pos + 1, w).astype(jnp.float32)
        pooled.append((win * inv_count - ug).astype(jnp.bfloat16))

    pool_out = _dot(jnp.concatenate(pooled, axis=1), wpbd_ref[...]) * ps_ref[...]
    mixed = jnp.concatenate([pool_out.astype(jnp.bfloat16), a_ref[...]], axis=1)
    h1 = x_ref[...] + _dot(mixed, wo_ref[...])

    hn = _rms(h1, gc_ref[...]).astype(jnp.bfloat16)
    d = h1.shape[1]
    hd = d // MEM_HEADS
    q = (_dot(hn, wq_ref[...]) * (1.0 / math.sqrt(hd))).astype(jnp.bfloat16)
    outs = []
    for hh in range(MEM_HEADS):
        sl = slice(hh * hd, (hh + 1) * hd)
        s = _dot_nt(q[:, sl], km_ref[:, sl])
        e = jnp.exp(s - jnp.max(s, axis=-1, keepdims=True))
        p = e * (1.0 / jnp.sum(e, axis=-1, keepdims=True))
        outs.append(_dot(p.astype(jnp.bfloat16), vm_ref[:, sl]))
    o = jnp.concatenate(outs, axis=-1).astype(jnp.bfloat16)
    h2 = h1 + _dot(o, wom_ref[...])
    h_ref[...] = h2

    xn = _rms(h2, gf_ref[...])
    xrow_ref[:, :d] = xn
    x_hi = xn.astype(jnp.bfloat16)
    x_lo = (xn - x_hi.astype(jnp.float32)).astype(jnp.bfloat16)
    both = _dot(x_hi, wr_ref[...])
    logits_tm = both[:, :ROUTER_LANES] + both[:, ROUTER_LANES:] + _dot(x_lo, wr_ref[:, :ROUTER_LANES])
    logits = logits_tm.T[:ROUTER_ROWS]
    bucket, w_a, w_b = _route_t(logits)

    @pl.when(i == 0)
    def _():
        run_ref[...] = jnp.zeros_like(run_ref)

    brow = lax.broadcasted_iota(jnp.int32, (ROUTER_ROWS, tm), 0).astype(jnp.float32)
    onehot = (brow == bucket).astype(jnp.float32)
    before = _dot(onehot.astype(jnp.bfloat16), su_ref[...])
    run = run_ref[...]
    rank = jnp.sum(onehot * (before + run[:, :1]), axis=0, keepdims=True)
    run = run + jnp.sum(onehot, axis=1, keepdims=True)
    run_ref[...] = run
    cnt_ref[...] = run
    bucket_ref[...] = bucket.astype(jnp.int32)
    rank_ref[...] = rank.astype(jnp.int32)

    prow = lax.broadcasted_iota(jnp.int32, (PAYLOAD_LANES, tm), 0)
    payload_t = jnp.where(prow == 0, w_a, jnp.where(prow == 1, w_b, 0.0))
    xrow_ref[:, d:] = payload_t.T


def _mix_cross(x2, u, attn, w_pool_bd, pool_scale, w_out, g_cross, wq, kmem, vmem, wom, g_ffn, w_router,
               *, tm, seq, mem_len):
    n, d = x2.shape
    pool_width = u.shape[1]
    attn_width = attn.shape[1]
    tiles_per_seq = seq // tm
    n_tiles = n // tm
    halo_blocks = tm // POOL_HALO
    su = (lax.broadcasted_iota(jnp.int32, (tm, tm), 0)
          < lax.broadcasted_iota(jnp.int32, (tm, tm), 1)).astype(jnp.bfloat16)
    kern = functools.partial(_mix_cross_kernel, tm=tm, tiles_per_seq=tiles_per_seq)
    const2 = lambda i: (0, 0)
    mem_map = lambda i: (i // tiles_per_seq, 0)
    return pl.pallas_call(
        kern,
        grid=(n_tiles,),
        in_specs=[
            pl.BlockSpec((tm, d), lambda i: (i, 0)),
            pl.BlockSpec((tm, pool_width), lambda i: (i, 0)),
            pl.BlockSpec((POOL_HALO, pool_width), lambda i: (jnp.maximum(i * halo_blocks - 1, 0), 0)),
            pl.BlockSpec((tm, attn_width), lambda i: (i, 0)),
            pl.BlockSpec(w_pool_bd.shape, const2),
            pl.BlockSpec((1, pool_width), const2),
            pl.BlockSpec(w_out.shape, const2),
            pl.BlockSpec((1, d), const2),
            pl.BlockSpec((d, d), const2),
            pl.BlockSpec((mem_len, d), mem_map),
            pl.BlockSpec((mem_len, d), mem_map),
            pl.BlockSpec((d, d), const2),
            pl.BlockSpec((1, d), const2),
            pl.BlockSpec((d, 2 * ROUTER_LANES), const2),
            pl.BlockSpec((tm, tm), const2),
        ],
        out_specs=[
            pl.BlockSpec((tm, d), lambda i: (i, 0)),
            pl.BlockSpec((tm, d + PAYLOAD_LANES), lambda i: (i, 0)),
            pl.BlockSpec((None, 1, tm), lambda i: (i, 0, 0)),
            pl.BlockSpec((None, 1, tm), lambda i: (i, 0, 0)),
            pl.BlockSpec((ROUTER_ROWS, LANES_V7X), const2),
        ],
        out_shape=[
            jax.ShapeDtypeStruct((n, d), jnp.float32),
            jax.ShapeDtypeStruct((n, d + PAYLOAD_LANES), jnp.float32),
            jax.ShapeDtypeStruct((n_tiles, 1, tm), jnp.int32),
            jax.ShapeDtypeStruct((n_tiles, 1, tm), jnp.int32),
            jax.ShapeDtypeStruct((ROUTER_ROWS, LANES_V7X), jnp.float32),
        ],
        scratch_shapes=[pltpu.VMEM((ROUTER_ROWS, LANES_V7X), jnp.float32)],
        compiler_params=pltpu.CompilerParams(
            dimension_semantics=("arbitrary",), vmem_limit_bytes=VMEM_LIMIT_BYTES),
        name="mix_cross",
    )(x2, u, u, attn, w_pool_bd, pool_scale, w_out, g_cross, wq, kmem, vmem, wom, g_ffn, w_router, su)


def _dispatch_kernel(pos_ref, x_ref, xs_init_ref, xs_ref, stage, sems, *, tm, n_tiles):
    del xs_init_ref
    i = pl.program_id(0)
    slot = i % 2
    base = i * tm

    def row_copy(slot, r, p):
        return pltpu.make_async_copy(stage.at[slot, pl.ds(r, 1), :], xs_ref.at[pl.ds(p, 1), :], sems.at[slot])

    def wait_slot(slot):
        pltpu.make_async_copy(stage.at[slot], xs_ref.at[pl.ds(0, tm), :], sems.at[slot]).wait()

    stage[slot] = x_ref[...]
    for r in range(tm):
        row_copy(slot, r, pos_ref[base + r]).start(priority=r % 2)

    @pl.when(i > 0)
    def _():
        wait_slot(1 - slot)

    @pl.when(i == n_tiles - 1)
    def _():
        wait_slot(slot)


def _dispatch(pos, xrow, n_pad, *, tm):
    n, width = xrow.shape
    xs_init = jnp.zeros((n_pad, width), xrow.dtype)
    return pl.pallas_call(
        functools.partial(_dispatch_kernel, tm=tm, n_tiles=n // tm),
        grid_spec=pltpu.PrefetchScalarGridSpec(
            num_scalar_prefetch=1,
            grid=(n // tm,),
            in_specs=[
                pl.BlockSpec((tm, width), lambda i, pos: (i, 0)),
                pl.BlockSpec(memory_space=pl.ANY),
            ],
            out_specs=pl.BlockSpec(memory_space=pl.ANY),
            scratch_shapes=[pltpu.VMEM((2, tm, width), xrow.dtype), pltpu.SemaphoreType.DMA((2,))],
        ),
        out_shape=jax.ShapeDtypeStruct((n_pad, width), xrow.dtype),
        input_output_aliases={2: 0},
        compiler_params=pltpu.CompilerParams(
            dimension_semantics=("arbitrary",), vmem_limit_bytes=VMEM_LIMIT_BYTES),
        name="moe_dispatch",
    )(pos, xrow, xs_init)


def _expert_kernel(ea_ref, eb_ref, na_ref, xs_ref, wgua_ref, wda_ref, wgub_ref, wdb_ref, y_ref, *, d):
    del ea_ref, eb_ref
    active = pl.program_id(0) < na_ref[0]

    @pl.when(jnp.logical_not(active))
    def _():
        y_ref[...] = jnp.zeros_like(y_ref)

    @pl.when(active)
    def _():
        x = xs_ref[:, :d].astype(jnp.bfloat16)

        def mlp(wgu_ref, wd_ref):
            gu = _dot(x, wgu_ref[...])
            ff = gu.shape[1] // 2
            gate, up = gu[:, :ff], gu[:, ff:]
            hmid = (gate * (1.0 / (1.0 + jnp.exp(-gate)))) * up
            return _dot(hmid.astype(jnp.bfloat16), wd_ref[...])

        y_ref[...] = (xs_ref[:, d:d + 1] * mlp(wgua_ref, wda_ref)
                      + xs_ref[:, d + 1:d + 2] * mlp(wgub_ref, wdb_ref))


def _experts(tile_ea, tile_eb, n_active, xs, wgu, wd, *, tm):
    n_pad, width = xs.shape
    d, ff = wd.shape[2], wd.shape[1]
    row_map = lambda j, ea, eb, na: (jnp.minimum(j, na[0] - 1), 0)
    a_map = lambda j, ea, eb, na: (ea[j], 0, 0)
    b_map = lambda j, ea, eb, na: (eb[j], 0, 0)
    return pl.pallas_call(
        functools.partial(_expert_kernel, d=d),
        grid_spec=pltpu.PrefetchScalarGridSpec(
            num_scalar_prefetch=3,
            grid=(n_pad // tm,),
            in_specs=[
                pl.BlockSpec((tm, width), row_map),
                pl.BlockSpec((None, d, 2 * ff), a_map),
                pl.BlockSpec((None, ff, d), a_map),
                pl.BlockSpec((None, d, 2 * ff), b_map),
                pl.BlockSpec((None, ff, d), b_map),
            ],
            out_specs=pl.BlockSpec((tm, d), lambda j, ea, eb, na: (j, 0)),
        ),
        out_shape=jax.ShapeDtypeStruct((n_pad, d), jnp.float32),
        compiler_params=pltpu.CompilerParams(
            dimension_semantics=("arbitrary",), vmem_limit_bytes=VMEM_LIMIT_BYTES),
        name="moe_experts",
    )(tile_ea, tile_eb, n_active, xs, wgu, wd, wgu, wd)


def _combine_kernel(pos_ref, h_ref, y_ref, g_ref, o_ref, ybuf, sems, *, tm, n_tiles):
    i = pl.program_id(0)

    def row_copy(slot, r, p):
        return pltpu.make_async_copy(y_ref.at[pl.ds(p, 1), :], ybuf.at[slot, pl.ds(r, 1), :], sems.at[slot])

    @pl.when(i < n_tiles)
    def _():
        slot = i % 2
        for r in range(tm):
            row_copy(slot, r, pos_ref[i * tm + r]).start(priority=r % 2)

    @pl.when(i > 0)
    def _():
        slot = (i - 1) % 2
        pltpu.make_async_copy(y_ref.at[pl.ds(0, tm), :], ybuf.at[slot], sems.at[slot]).wait()
        o_ref[...] = _rms(h_ref[...] + ybuf[slot], g_ref[...])


def _combine(pos, h, y_sorted, g_final, *, tm):
    n, d = h.shape
    n_tiles = n // tm
    prev_tile = lambda i, pos: (jnp.maximum(i - 1, 0), 0)
    return pl.pallas_call(
        functools.partial(_combine_kernel, tm=tm, n_tiles=n_tiles),
        grid_spec=pltpu.PrefetchScalarGridSpec(
            num_scalar_prefetch=1,
            grid=(n_tiles + 1,),
            in_specs=[
                pl.BlockSpec((tm, d), prev_tile),
                pl.BlockSpec(memory_space=pl.ANY),
                pl.BlockSpec((1, d), lambda i, pos: (0, 0)),
            ],
            out_specs=pl.BlockSpec((tm, d), prev_tile),
            scratch_shapes=[pltpu.VMEM((2, tm, d), jnp.float32), pltpu.SemaphoreType.DMA((2,))],
        ),
        out_shape=jax.ShapeDtypeStruct((n, d), jnp.float32),
        compiler_params=pltpu.CompilerParams(
            dimension_semantics=("arbitrary",), vmem_limit_bytes=VMEM_LIMIT_BYTES),
        name="moe_combine",
    )(pos, h, y_sorted, g_final)


def _tile(n, pref):
    t = min(n, pref)
    assert n % t == 0, (n, t)
    return t


def _sorted_layout(bucket, rank, counts, *, tm_e, n_tiles_e):
    seg_tiles = (counts + tm_e - 1) // tm_e
    seg_end = jnp.cumsum(seg_tiles)
    seg_start = seg_end - seg_tiles
    pos = (seg_start * tm_e)[bucket] + rank
    n_active = seg_end[-1]
    tile = jnp.minimum(jnp.arange(n_tiles_e, dtype=jnp.int32), n_active - 1)
    tile_bucket = jnp.sum((tile[:, None] >= seg_end[None, :]).astype(jnp.int32), axis=1)
    group, pair = tile_bucket // len(PAIRS), tile_bucket % len(PAIRS)
    pair_a = jnp.array([p[0] for p in PAIRS], jnp.int32)
    pair_b = jnp.array([p[1] for p in PAIRS], jnp.int32)
    tile_ea = group * EXPERTS_PER_GROUP + pair_a[pair]
    tile_eb = group * EXPERTS_PER_GROUP + pair_b[pair]
    return pos.astype(jnp.int32), tile_ea.astype(jnp.int32), tile_eb.astype(jnp.int32), n_active.astype(jnp.int32)


def kernel(x, mem, norm_mix, w_in, w_pool, pool_scale, w_out, norm_cross, norm_mem, w_q_mem, w_k_mem,
           w_v_mem, w_o_mem, norm_ffn, w_group, w_expert, w_gate, w_up, w_down, norm_final):
    b, s, d = x.shape
    mem_len = mem.shape[1]
    depth = norm_mix.shape[0]
    pool_width = pool_scale.shape[1]
    attn_width = w_out.shape[1] - pool_width
    bf = jnp.bfloat16
    n = b * s

    assert depth == 1, "single-layer problem: the final RMSNorm is fused into the combine kernel"
    l = 0
    h = x.reshape(n, d)
    u, qkv = _in_proj(h, norm_mix[l][None], w_in[l].astype(bf),
                      pool_width=pool_width, attn_width=attn_width, tm=_tile(n, 512))
    tq = _tile(s, 2 * MXU_DIM_V7X)
    sub = tq // 2
    attn = _sb_attention(qkv.reshape(b, s, 3 * attn_width), attn_width=attn_width, tq=tq, sub=sub, win=2 * sub)
    kmem, vmem = _mem_kv(mem.reshape(b * mem_len, d), norm_mem[l][None],
                         w_k_mem[l].astype(bf), w_v_mem[l].astype(bf), tm=mem_len)

    w_router = jnp.concatenate([w_group[l], w_expert[l]], axis=1)
    w_router = jnp.pad(w_router, ((0, 0), (0, ROUTER_LANES - w_router.shape[1])))
    wr_hi = w_router.astype(bf)
    wr_lo = (w_router - wr_hi.astype(jnp.float32)).astype(bf)
    w_pool_bd = jax.scipy.linalg.block_diag(*w_pool[l]).astype(bf)
    h, xrow, bucket, rank, counts = _mix_cross(
        h, u, attn.reshape(n, attn_width), w_pool_bd, pool_scale[l][None], w_out[l].astype(bf),
        norm_cross[l][None], w_q_mem[l].astype(bf), kmem, vmem, w_o_mem[l].astype(bf),
        norm_ffn[l][None], jnp.concatenate([wr_hi, wr_lo], axis=1), tm=_tile(s, 512), seq=s, mem_len=mem_len)

    tm_e = _tile(n, MXU_DIM_V7X)
    n_tiles_e = -(-(n + N_BUCKETS * (tm_e - 1)) // tm_e)
    pos, tile_ea, tile_eb, n_active = _sorted_layout(
        bucket.reshape(n), rank.reshape(n), counts[:N_BUCKETS, 0].astype(jnp.int32),
        tm_e=tm_e, n_tiles_e=n_tiles_e)
    xs = _dispatch(pos, xrow, n_tiles_e * tm_e, tm=_tile(n, 512))
    w_gate_up = jnp.concatenate([w_gate[l], w_up[l]], axis=-1).astype(bf)
    y_sorted = _experts(tile_ea, tile_eb, n_active[None], xs, w_gate_up, w_down[l].astype(bf), tm=tm_e)
    out = _combine(pos, h, y_sorted, norm_final[None], tm=_tile(n, 512))
    return out.reshape(b, s, d)
```

```python
import functools
import math

import jax
import jax.numpy as jnp
from jax import lax
from jax.experimental import pallas as pl
from jax.experimental.pallas import tpu as pltpu

RMS_EPS = 1e-6
POOL_WINDOWS = (2, 4, 8, 16)
SB_HEAD_DIM = 64
MEM_HEADS = 4
N_GROUPS = 4
EXPERTS_PER_GROUP = 4
N_EXPERTS = N_GROUPS * EXPERTS_PER_GROUP

LANES_V7X = 128
MXU_DIM_V7X = 256
VMEM_LIMIT_BYTES = 56 * 1024 * 1024

LOG2E = 1.4426950408889634
UNDERFLOW_EXPONENT = 104.0
MASKED_LOGIT = -1e30
POOL_HALO = 16

PAIRS = tuple((a, b) for a in range(EXPERTS_PER_GROUP) for b in range(a + 1, EXPERTS_PER_GROUP))
N_BUCKETS = N_GROUPS * len(PAIRS)
ROUTER_ROWS = 32
ROUTER_LANES = LANES_V7X
PAYLOAD_LANES = LANES_V7X
assert N_GROUPS + N_EXPERTS <= ROUTER_ROWS and N_BUCKETS <= ROUTER_ROWS


def _rms(x, gain):
    ms = jnp.mean(x * x, axis=-1, keepdims=True)
    return x * lax.rsqrt(ms + RMS_EPS) * gain


def _dot(a, b):
    return jnp.dot(a, b, preferred_element_type=jnp.float32)


def _dot_nt(a, b):
    return lax.dot_general(a, b, (((1,), (1,)), ((), ())), preferred_element_type=jnp.float32)


def _in_proj_kernel(x_ref, g_ref, w_ref, u_ref, qkv_ref, *, pool_width, attn_width, q_scale):
    xn = _rms(x_ref[...], g_ref[...]).astype(jnp.bfloat16)
    proj = _dot(xn, w_ref[...])
    u_ref[...] = proj[:, :pool_width]
    q = proj[:, pool_width:pool_width + attn_width] * q_scale
    qkv_ref[:, :attn_width] = q.astype(jnp.bfloat16)
    qkv_ref[:, attn_width:] = proj[:, pool_width + attn_width:].astype(jnp.bfloat16)


def _in_proj(x2, gain, w_in_bf16, *, pool_width, attn_width, tm):
    n, d = x2.shape
    in_width = w_in_bf16.shape[1]
    kern = functools.partial(_in_proj_kernel, pool_width=pool_width, attn_width=attn_width,
                             q_scale=1.0 / math.sqrt(SB_HEAD_DIM))
    return pl.pallas_call(
        kern,
        grid=(n // tm,),
        in_specs=[
            pl.BlockSpec((tm, d), lambda i: (i, 0)),
            pl.BlockSpec((1, d), lambda i: (0, 0)),
            pl.BlockSpec((d, in_width), lambda i: (0, 0)),
        ],
        out_specs=[
            pl.BlockSpec((tm, pool_width), lambda i: (i, 0)),
            pl.BlockSpec((tm, 3 * attn_width), lambda i: (i, 0)),
        ],
        out_shape=[
            jax.ShapeDtypeStruct((n, pool_width), jnp.float32),
            jax.ShapeDtypeStruct((n, 3 * attn_width), jnp.bfloat16),
        ],
        compiler_params=pltpu.CompilerParams(
            dimension_semantics=("parallel",), vmem_limit_bytes=VMEM_LIMIT_BYTES),
        name="in_proj",
    )(x2, gain, w_in_bf16)


def _sb_tile(qh, kb, vb, tri, c, tail_mask):
    z = _dot_nt(qh, kb)
    if tail_mask is not None:
        m = tail_mask.shape[1]
        tail = jnp.where(tail_mask, z[:, -m:], MASKED_LOGIT)
        z = tail if m == z.shape[1] else jnp.concatenate([z[:, :-m], tail], axis=1)
    sp = (jnp.maximum(z, 0.0) + jnp.log(1.0 + jnp.exp2(jnp.abs(z) * (-LOG2E)))).astype(jnp.bfloat16)
    n = tri.shape[0]
    chunks, total = [], None
    for k0 in range(z.shape[1] - n, -1, -n):
        cum_k = _dot(sp[:, k0:k0 + n], tri)
        if total is not None:
            cum_k = cum_k + total
        total = cum_k[:, :1]
        chunks.insert(0, cum_k)
    cum = chunks[0] if len(chunks) == 1 else jnp.concatenate(chunks, axis=1)
    t = jnp.minimum(z - cum, 0.0)
    w = jnp.exp(t if c is None else t - c)
    return (total if c is None else c + total), _dot(w.astype(jnp.bfloat16), vb)


def _sb_attn_kernel(q_ref, k_ref, v_ref, tri_ref, o_ref, *, tq, sub, win):
    qi = pl.program_id(2)
    n_sub = tq // sub
    lane = lax.broadcasted_iota(jnp.int32, (tq, LANES_V7X), 1)
    head0 = lane < SB_HEAD_DIM
    q2 = q_ref[...]
    zero = jnp.zeros_like(q2)
    q_heads = (jnp.where(head0, q2, zero), jnp.where(head0, zero, q2))
    q_st = jnp.concatenate(q_heads, axis=0)

    def kv_block(j):
        start = pl.multiple_of(j * tq, tq)
        return k_ref[pl.ds(start, tq), :], v_ref[pl.ds(start, tq), :]

    def stacked_iota(rows, cols):
        row = lax.broadcasted_iota(jnp.int32, (2 * rows, cols), 0)
        col = lax.broadcasted_iota(jnp.int32, (2 * rows, cols), 1)
        return jnp.where(row >= rows, row - rows, row), col

    def first_block():
        row, col = stacked_iota(tq, tq)
        kb, vb = kv_block(0)
        return _sb_tile(q_st, kb, vb, tri_ref[...], None, col < row)

    def windows():
        row, col = stacked_iota(sub, sub)
        causal = col < row
        tri = tri_ref[...]
        parts = []
        for hf in range(n_sub):
            start = pl.multiple_of((qi * n_sub + hf + 1) * sub - win, sub)
            kw = k_ref[pl.ds(start, win), :]
            vw = v_ref[pl.ds(start, win), :]
            q_sub = jnp.concatenate([qh[hf * sub:(hf + 1) * sub] for qh in q_heads], axis=0)
            parts.append(_sb_tile(q_sub, kw, vw, tri, None, causal))
        return tuple(jnp.concatenate([p[i][h * sub:(h + 1) * sub] for h in range(2) for p in parts], axis=0)
                     for i in range(2))

    carry = lax.cond(qi == 0, first_block, windows)

    def live(carry):
        return jnp.min(carry[0], axis=0, keepdims=True)[0, 0] < UNDERFLOW_EXPONENT

    def add_block(carry, j, tail_mask):
        kb, vb = kv_block(j)
        c, acc = _sb_tile(q_st, kb, vb, tri_ref[...], carry[0], tail_mask)
        return c, carry[1] + acc

    def partial_block(carry):
        row, col = stacked_iota(tq, tq)
        fresh = col < (row // sub + 1) * sub + (tq - win)
        return add_block(carry, qi - 1, fresh)

    carry = lax.cond(jnp.logical_and(qi > 0, live(carry)), partial_block, lambda cr: cr, carry)

    def cond(state):
        step, alive, _ = state
        return jnp.logical_and(step < qi - 1, alive > 0)

    def body(state):
        step, _, carry = state
        carry = add_block(carry, qi - 2 - step, None)
        return step + 1, live(carry).astype(jnp.int32), carry

    _, _, carry = lax.while_loop(cond, body, (jnp.int32(0), live(carry).astype(jnp.int32), carry))
    acc = carry[1]
    o_ref[...] = jnp.where(head0, acc[:tq], acc[tq:]).astype(o_ref.dtype)


def _sb_attention(qkv3, *, attn_width, tq, sub, win):
    b, s, _ = qkv3.shape
    assert tq % sub == 0 and win % sub == 0 and sub <= win <= tq + sub, (tq, sub, win)
    n_pairs = attn_width // LANES_V7X
    n = math.gcd(math.gcd(tq, win), MXU_DIM_V7X)
    tri = (lax.broadcasted_iota(jnp.int32, (n, n), 0)
           >= lax.broadcasted_iota(jnp.int32, (n, n), 1)).astype(jnp.bfloat16)

    kern = functools.partial(_sb_attn_kernel, tq=tq, sub=sub, win=win)
    const2 = lambda bi, hp, qi: (0, 0)
    return pl.pallas_call(
        kern,
        grid=(b, n_pairs, s // tq),
        in_specs=[
            pl.BlockSpec((None, tq, LANES_V7X), lambda bi, hp, qi: (bi, qi, hp)),
            pl.BlockSpec((None, s, LANES_V7X), lambda bi, hp, qi: (bi, 0, n_pairs + hp)),
            pl.BlockSpec((None, s, LANES_V7X), lambda bi, hp, qi: (bi, 0, 2 * n_pairs + hp)),
            pl.BlockSpec((n, n), const2),
        ],
        out_specs=pl.BlockSpec((None, tq, LANES_V7X), lambda bi, hp, qi: (bi, qi, hp)),
        out_shape=jax.ShapeDtypeStruct((b, s, attn_width), jnp.bfloat16),
        compiler_params=pltpu.CompilerParams(
            dimension_semantics=("parallel", "parallel", "parallel"),
            vmem_limit_bytes=VMEM_LIMIT_BYTES),
        name="sb_attention",
    )(qkv3, qkv3, qkv3, tri)


def _mem_kv_kernel(m_ref, g_ref, wk_ref, wv_ref, k_ref, v_ref):
    mn = _rms(m_ref[...], g_ref[...]).astype(jnp.bfloat16)
    k_ref[...] = _dot(mn, wk_ref[...]).astype(jnp.bfloat16)
    v_ref[...] = _dot(mn, wv_ref[...]).astype(jnp.bfloat16)


def _mem_kv(mem2, gain, wk, wv, *, tm):
    n, d = mem2.shape
    row = pl.BlockSpec((tm, d), lambda i: (i, 0))
    full = pl.BlockSpec((d, d), lambda i: (0, 0))
    return pl.pallas_call(
        _mem_kv_kernel,
        grid=(n // tm,),
        in_specs=[row, pl.BlockSpec((1, d), lambda i: (0, 0)), full, full],
        out_specs=[row, row],
        out_shape=[jax.ShapeDtypeStruct((n, d), jnp.bfloat16)] * 2,
        compiler_params=pltpu.CompilerParams(
            dimension_semantics=("parallel",), vmem_limit_bytes=VMEM_LIMIT_BYTES),
        name="mem_kv",
    )(mem2, gain, wk, wv)


def _route_t(logits):
    neg = jnp.float32(-jnp.inf)
    big = jnp.float32(ROUTER_ROWS)
    row = lax.broadcasted_iota(jnp.int32, logits.shape, 0).astype(jnp.float32)

    def col_max(mask):
        return jnp.max(jnp.where(mask, logits, neg), axis=0, keepdims=True)

    def first_argmax(mask, mx):
        return jnp.min(jnp.where(mask & (logits == mx), row, big), axis=0, keepdims=True)

    gmask = row < N_GROUPS
    gmax = col_max(gmask)
    gsum = jnp.sum(jnp.where(gmask, jnp.exp(logits - gmax), 0.0), axis=0, keepdims=True)
    g_gate = 1.0 / gsum
    g_idx = first_argmax(gmask, gmax)

    lo = N_GROUPS + EXPERTS_PER_GROUP * g_idx
    emask = (row >= lo) & (row < lo + EXPERTS_PER_GROUP)
    m1 = col_max(emask)
    i1 = first_argmax(emask, m1)
    mask2 = emask & (row != i1)
    m2 = col_max(mask2)
    i2 = first_argmax(mask2, m2)
    esum = jnp.sum(jnp.where(emask, jnp.exp(logits - m1), 0.0), axis=0, keepdims=True)
    p1 = 1.0 / esum
    p2 = jnp.exp(m2 - m1) / esum
    tot = p1 + p2
    w1 = g_gate * (p1 / tot)
    w2 = g_gate * (p2 / tot)

    first = i1 < i2
    la = jnp.where(first, i1, i2) - lo
    lb = jnp.where(first, i2, i1) - lo
    pair = la * (2 * EXPERTS_PER_GROUP - 1 - la) * 0.5 + (lb - la - 1.0)
    bucket = g_idx * len(PAIRS) + pair
    return bucket, jnp.where(first, w1, w2), jnp.where(first, w2, w1)


def _mix_cross_kernel(x_ref, u_ref, halo_ref, a_ref, wpbd_ref, ps_ref, wo_ref, gc_ref,
                      wq_ref, km_ref, vm_ref, wom_ref, gf_ref, wr_ref, su_ref,
                      h_ref, xrow_ref, bucket_ref, rank_ref, cnt_ref, run_ref, *, tm, tiles_per_seq):
    i = pl.program_id(0)
    pool_width = u_ref.shape[1]
    gw = pool_width // len(POOL_WINDOWS)
    tile_in_seq = i % tiles_per_seq
    first = tile_in_seq == 0
    pos = tile_in_seq * tm + lax.broadcasted_iota(jnp.int32, (tm, 1), 0)

    halo = jnp.where(first, 0.0, halo_ref[...])
    u = u_ref[...]
    pooled = []
    for g, w in enumerate(POOL_WINDOWS):
        ug = u[:, g * gw:(g + 1) * gw]
        ext = jnp.concatenate([halo[:, g * gw:(g + 1) * gw], ug], axis=0)
        shift = 1
        while shift < w:
            ext = ext + pltpu.roll(ext, shift, 0)
            shift *= 2
        win = ext[POOL_HALO:, :]
        inv_count = 1.0 / jnp.minimum(pos + 1, w).astype(jnp.float32)
        pooled.append((win * inv_count - ug).astype(jnp.bfloat16))

    pool_out = _dot(jnp.concatenate(pooled, axis=1), wpbd_ref[...]) * ps_ref[...]
    mixed = jnp.concatenate([pool_out.astype(jnp.bfloat16), a_ref[...]], axis=1)
    h1 = x_ref[...] + _dot(mixed, wo_ref[...])

    hn = _rms(h1, gc_ref[...]).astype(jnp.bfloat16)
    d = h1.shape[1]
    hd = d // MEM_HEADS
    q = (_dot(hn, wq_ref[...]) * (1.0 / math.sqrt(hd))).astype(jnp.bfloat16)
    outs = []
    for hh in range(MEM_HEADS):
        sl = slice(hh * hd, (hh + 1) * hd)
        s = _dot_nt(q[:, sl], km_ref[:, sl])
        e = jnp.exp(s - jnp.max(s, axis=-1, keepdims=True))
        p = e * (1.0 / jnp.sum(e, axis=-1, keepdims=True))
        outs.append(_dot(p.astype(jnp.bfloat16), vm_ref[:, sl]))
    o = jnp.concatenate(outs, axis=-1).astype(jnp.bfloat16)
    h2 = h1 + _dot(o, wom_ref[...])
    h_ref[...] = h2

    xn = _rms(h2, gf_ref[...])
    xrow_ref[:, :d] = xn
    x_hi = xn.astype(jnp.bfloat16)
    x_lo = (xn - x_hi.astype(jnp.float32)).astype(jnp.bfloat16)
    both = _dot(x_hi, wr_ref[...])
    logits_tm = both[:, :ROUTER_LANES] + both[:, ROUTER_LANES:] + _dot(x_lo, wr_ref[:, :ROUTER_LANES])
    logits = logits_tm.T[:ROUTER_ROWS]
    bucket, w_a, w_b = _route_t(logits)

    @pl.when(i == 0)
    def _():
        run_ref[...] = jnp.zeros_like(run_ref)

    brow = lax.broadcasted_iota(jnp.int32, (ROUTER_ROWS, tm), 0).astype(jnp.float32)
    onehot = (brow == bucket).astype(jnp.float32)
    before = _dot(onehot.astype(jnp.bfloat16), su_ref[...])
    run = run_ref[...]
    rank = jnp.sum(onehot * (before + run[:, :1]), axis=0, keepdims=True)
    run = run + jnp.sum(onehot, axis=1, keepdims=True)
    run_ref[...] = run
    cnt_ref[...] = run
    bucket_ref[...] = bucket.astype(jnp.int32)
    rank_ref[...] = rank.astype(jnp.int32)

    prow = lax.broadcasted_iota(jnp.int32, (PAYLOAD_LANES, tm), 0)
    payload_t = jnp.where(prow == 0, w_a, jnp.where(prow == 1, w_b, 0.0))
    xrow_ref[:, d:] = payload_t.T


def _mix_cross(x2, u, attn, w_pool_bd, pool_scale, w_out, g_cross, wq, kmem, vmem, wom, g_ffn, w_router,
               *, tm, seq, mem_len):
    n, d = x2.shape
    pool_width = u.shape[1]
    attn_width = attn.shape[1]
    tiles_per_seq = seq // tm
    n_tiles = n // tm
    halo_blocks = tm // POOL_HALO
    su = (lax.broadcasted_iota(jnp.int32, (tm, tm), 0)
          < lax.broadcasted_iota(jnp.int32, (tm, tm), 1)).astype(jnp.bfloat16)
    kern = functools.partial(_mix_cross_kernel, tm=tm, tiles_per_seq=tiles_per_seq)
    const2 = lambda i: (0, 0)
    mem_map = lambda i: (i // tiles_per_seq, 0)
    return pl.pallas_call(
        kern,
        grid=(n_tiles,),
        in_specs=[
            pl.BlockSpec((tm, d), lambda i: (i, 0)),
            pl.BlockSpec((tm, pool_width), lambda i: (i, 0)),
            pl.BlockSpec((POOL_HALO, pool_width), lambda i: (jnp.maximum(i * halo_blocks - 1, 0), 0)),
            pl.BlockSpec((tm, attn_width), lambda i: (i, 0)),
            pl.BlockSpec(w_pool_bd.shape, const2),
            pl.BlockSpec((1, pool_width), const2),
            pl.BlockSpec(w_out.shape, const2),
            pl.BlockSpec((1, d), const2),
            pl.BlockSpec((d, d), const2),
            pl.BlockSpec((mem_len, d), mem_map),
            pl.BlockSpec((mem_len, d), mem_map),
            pl.BlockSpec((d, d), const2),
            pl.BlockSpec((1, d), const2),
            pl.BlockSpec((d, 2 * ROUTER_LANES), const2),
            pl.BlockSpec((tm, tm), const2),
        ],
        out_specs=[
            pl.BlockSpec((tm, d), lambda i: (i, 0)),
            pl.BlockSpec((tm, d + PAYLOAD_LANES), lambda i: (i, 0)),
            pl.BlockSpec((None, 1, tm), lambda i: (i, 0, 0)),
            pl.BlockSpec((None, 1, tm), lambda i: (i, 0, 0)),
            pl.BlockSpec((ROUTER_ROWS, LANES_V7X), const2),
        ],
        out_shape=[
            jax.ShapeDtypeStruct((n, d), jnp.float32),
            jax.ShapeDtypeStruct((n, d + PAYLOAD_LANES), jnp.float32),
            jax.ShapeDtypeStruct((n_tiles, 1, tm), jnp.int32),
            jax.ShapeDtypeStruct((n_tiles, 1, tm), jnp.int32),
            jax.ShapeDtypeStruct((ROUTER_ROWS, LANES_V7X), jnp.float32),
        ],
        scratch_shapes=[pltpu.VMEM((ROUTER_ROWS, LANES_V7X), jnp.float32)],
        compiler_params=pltpu.CompilerParams(
            dimension_semantics=("arbitrary",), vmem_limit_bytes=VMEM_LIMIT_BYTES),
        name="mix_cross",
    )(x2, u, u, attn, w_pool_bd, pool_scale, w_out, g_cross, wq, kmem, vmem, wom, g_ffn, w_router, su)


def _dispatch_kernel(pos_ref, last_ref, n_last_ref, na_ref, x_ref, xs_ref, stage, zeros, sems, fill_sem,
                     *, tm, n_tiles, n_tiles_e):
    i = pl.program_id(0)
    slot = i % 2
    base = i * tm
    tm_e = zeros.shape[0]

    def fill_copy(j):
        return pltpu.make_async_copy(zeros, xs_ref.at[pl.ds(pl.multiple_of(j * tm_e, tm_e), tm_e), :], fill_sem)

    def for_range(lo, hi, fn):
        def body(k, carry):
            fn(k)
            return carry
        lax.fori_loop(lo, hi, body, 0)

    @pl.when(i == 0)
    def _():
        zeros[...] = jnp.zeros_like(zeros)
        for_range(0, n_last_ref[0], lambda k: fill_copy(last_ref[k]).start())
        for_range(na_ref[0], n_tiles_e, lambda j: fill_copy(j).start())
        for_range(0, n_last_ref[0], lambda k: fill_copy(0).wait())
        for_range(na_ref[0], n_tiles_e, lambda j: fill_copy(0).wait())

    def row_copy(slot, r, p):
        return pltpu.make_async_copy(stage.at[slot, pl.ds(r, 1), :], xs_ref.at[pl.ds(p, 1), :], sems.at[slot])

    def wait_slot(slot):
        pltpu.make_async_copy(stage.at[slot], xs_ref.at[pl.ds(0, tm), :], sems.at[slot]).wait()

    stage[slot] = x_ref[...]
    for r in range(tm):
        row_copy(slot, r, pos_ref[base + r]).start(priority=r % 2)

    @pl.when(i > 0)
    def _():
        wait_slot(1 - slot)

    @pl.when(i == n_tiles - 1)
    def _():
        wait_slot(slot)


def _dispatch(pos, last_tiles, n_last, n_active, xrow, *, tm, tm_e, n_tiles_e):
    n, width = xrow.shape
    return pl.pallas_call(
        functools.partial(_dispatch_kernel, tm=tm, n_tiles=n // tm, n_tiles_e=n_tiles_e),
        grid_spec=pltpu.PrefetchScalarGridSpec(
            num_scalar_prefetch=4,
            grid=(n // tm,),
            in_specs=[pl.BlockSpec((tm, width), lambda i, *_: (i, 0))],
            out_specs=pl.BlockSpec(memory_space=pl.ANY),
            scratch_shapes=[pltpu.VMEM((2, tm, width), xrow.dtype), pltpu.VMEM((tm_e, width), xrow.dtype),
                            pltpu.SemaphoreType.DMA((2,)), pltpu.SemaphoreType.DMA],
        ),
        out_shape=jax.ShapeDtypeStruct((n_tiles_e * tm_e, width), xrow.dtype),
        compiler_params=pltpu.CompilerParams(
            dimension_semantics=("arbitrary",), vmem_limit_bytes=VMEM_LIMIT_BYTES),
        name="moe_dispatch",
    )(pos, last_tiles, n_last, n_active, xrow)


def _expert_kernel(ea_ref, eb_ref, na_ref, xs_ref, wgua_ref, wda_ref, wgub_ref, wdb_ref, y_ref, *, d):
    del ea_ref, eb_ref
    active = pl.program_id(0) < na_ref[0]

    @pl.when(jnp.logical_not(active))
    def _():
        y_ref[...] = jnp.zeros_like(y_ref)

    @pl.when(active)
    def _():
        x = xs_ref[:, :d].astype(jnp.bfloat16)

        def mlp(wgu_ref, wd_ref):
            gu = _dot(x, wgu_ref[...])
            ff = gu.shape[1] // 2
            gate, up = gu[:, :ff], gu[:, ff:]
            hmid = (gate * (1.0 / (1.0 + jnp.exp(-gate)))) * up
            return _dot(hmid.astype(jnp.bfloat16), wd_ref[...])

        y_ref[...] = (xs_ref[:, d:d + 1] * mlp(wgua_ref, wda_ref)
                      + xs_ref[:, d + 1:d + 2] * mlp(wgub_ref, wdb_ref))


def _experts(tile_ea, tile_eb, n_active, xs, wgu, wd, *, tm):
    n_pad, width = xs.shape
    d, ff = wd.shape[2], wd.shape[1]
    row_map = lambda j, ea, eb, na: (jnp.minimum(j, na[0] - 1), 0)
    a_map = lambda j, ea, eb, na: (ea[j], 0, 0)
    b_map = lambda j, ea, eb, na: (eb[j], 0, 0)
    return pl.pallas_call(
        functools.partial(_expert_kernel, d=d),
        grid_spec=pltpu.PrefetchScalarGridSpec(
            num_scalar_prefetch=3,
            grid=(n_pad // tm,),
            in_specs=[
                pl.BlockSpec((tm, width), row_map),
                pl.BlockSpec((None, d, 2 * ff), a_map),
                pl.BlockSpec((None, ff, d), a_map),
                pl.BlockSpec((None, d, 2 * ff), b_map),
                pl.BlockSpec((None, ff, d), b_map),
            ],
            out_specs=pl.BlockSpec((tm, d), lambda j, ea, eb, na: (j, 0)),
        ),
        out_shape=jax.ShapeDtypeStruct((n_pad, d), jnp.float32),
        compiler_params=pltpu.CompilerParams(
            dimension_semantics=("arbitrary",), vmem_limit_bytes=VMEM_LIMIT_BYTES),
        name="moe_experts",
    )(tile_ea, tile_eb, n_active, xs, wgu, wd, wgu, wd)


def _combine_kernel(pos_ref, h_ref, y_ref, g_ref, o_ref, ybuf, sems, *, tm, n_tiles):
    i = pl.program_id(0)

    def row_copy(slot, r, p):
        return pltpu.make_async_copy(y_ref.at[pl.ds(p, 1), :], ybuf.at[slot, pl.ds(r, 1), :], sems.at[slot])

    @pl.when(i < n_tiles)
    def _():
        slot = i % 2
        for r in range(tm):
            row_copy(slot, r, pos_ref[i * tm + r]).start(priority=r % 2)

    @pl.when(i > 0)
    def _():
        slot = (i - 1) % 2
        pltpu.make_async_copy(y_ref.at[pl.ds(0, tm), :], ybuf.at[slot], sems.at[slot]).wait()
        o_ref[...] = _rms(h_ref[...] + ybuf[slot], g_ref[...])


def _combine(pos, h, y_sorted, g_final, *, tm):
    n, d = h.shape
    n_tiles = n // tm
    prev_tile = lambda i, pos: (jnp.maximum(i - 1, 0), 0)
    return pl.pallas_call(
        functools.partial(_combine_kernel, tm=tm, n_tiles=n_tiles),
        grid_spec=pltpu.PrefetchScalarGridSpec(
            num_scalar_prefetch=1,
            grid=(n_tiles + 1,),
            in_specs=[
                pl.BlockSpec((tm, d), prev_tile),
                pl.BlockSpec(memory_space=pl.ANY),
                pl.BlockSpec((1, d), lambda i, pos: (0, 0)),
            ],
            out_specs=pl.BlockSpec((tm, d), prev_tile),
            scratch_shapes=[pltpu.VMEM((2, tm, d), jnp.float32), pltpu.SemaphoreType.DMA((2,))],
        ),
        out_shape=jax.ShapeDtypeStruct((n, d), jnp.float32),
        compiler_params=pltpu.CompilerParams(
            dimension_semantics=("arbitrary",), vmem_limit_bytes=VMEM_LIMIT_BYTES),
        name="moe_combine",
    )(pos, h, y_sorted, g_final)


def _tile(n, pref):
    t = min(n, pref)
    assert n % t == 0, (n, t)
    return t


def _sorted_layout(bucket, rank, counts, *, tm_e, n_tiles_e):
    seg_tiles = (counts + tm_e - 1) // tm_e
    seg_end = jnp.cumsum(seg_tiles)
    seg_start = seg_end - seg_tiles
    pos = (seg_start * tm_e)[bucket] + rank
    nonempty = seg_tiles > 0
    last_tiles = jnp.sort(jnp.where(nonempty, seg_end - 1, n_tiles_e))
    n_active = seg_end[-1]
    tile = jnp.minimum(jnp.arange(n_tiles_e, dtype=jnp.int32), n_active - 1)
    tile_bucket = jnp.sum((tile[:, None] >= seg_end[None, :]).astype(jnp.int32), axis=1)
    group, pair = tile_bucket // len(PAIRS), tile_bucket % len(PAIRS)
    pair_a = jnp.array([p[0] for p in PAIRS], jnp.int32)
    pair_b = jnp.array([p[1] for p in PAIRS], jnp.int32)
    tile_ea = group * EXPERTS_PER_GROUP + pair_a[pair]
    tile_eb = group * EXPERTS_PER_GROUP + pair_b[pair]
    i32 = lambda a: a.astype(jnp.int32)
    return i32(pos), i32(last_tiles), i32(jnp.sum(nonempty)), i32(tile_ea), i32(tile_eb), i32(n_active)


def kernel(x, mem, norm_mix, w_in, w_pool, pool_scale, w_out, norm_cross, norm_mem, w_q_mem, w_k_mem,
           w_v_mem, w_o_mem, norm_ffn, w_group, w_expert, w_gate, w_up, w_down, norm_final):
    b, s, d = x.shape
    mem_len = mem.shape[1]
    depth = norm_mix.shape[0]
    pool_width = pool_scale.shape[1]
    attn_width = w_out.shape[1] - pool_width
    bf = jnp.bfloat16
    n = b * s

    assert depth == 1, "single-layer problem: the final RMSNorm is fused into the combine kernel"
    l = 0
    h = x.reshape(n, d)
    u, qkv = _in_proj(h, norm_mix[l][None], w_in[l].astype(bf),
                      pool_width=pool_width, attn_width=attn_width, tm=_tile(n, 1024))
    tq = _tile(s, 2 * MXU_DIM_V7X)
    sub = tq // 2
    attn = _sb_attention(qkv.reshape(b, s, 3 * attn_width), attn_width=attn_width, tq=tq, sub=sub, win=2 * sub)
    kmem, vmem = _mem_kv(mem.reshape(b * mem_len, d), norm_mem[l][None],
                         w_k_mem[l].astype(bf), w_v_mem[l].astype(bf), tm=mem_len)

    w_router = jnp.concatenate([w_group[l], w_expert[l]], axis=1)
    w_router = jnp.pad(w_router, ((0, 0), (0, ROUTER_LANES - w_router.shape[1])))
    wr_hi = w_router.astype(bf)
    wr_lo = (w_router - wr_hi.astype(jnp.float32)).astype(bf)
    w_pool_bd = jax.scipy.linalg.block_diag(*w_pool[l]).astype(bf)
    h, xrow, bucket, rank, counts = _mix_cross(
        h, u, attn.reshape(n, attn_width), w_pool_bd, pool_scale[l][None], w_out[l].astype(bf),
        norm_cross[l][None], w_q_mem[l].astype(bf), kmem, vmem, w_o_mem[l].astype(bf),
        norm_ffn[l][None], jnp.concatenate([wr_hi, wr_lo], axis=1), tm=_tile(s, 512), seq=s, mem_len=mem_len)

    tm_e = _tile(n, MXU_DIM_V7X)
    n_tiles_e = -(-(n + N_BUCKETS * (tm_e - 1)) // tm_e)
    pos, last_tiles, n_last, tile_ea, tile_eb, n_active = _sorted_layout(
        bucket.reshape(n), rank.reshape(n), counts[:N_BUCKETS, 0].astype(jnp.int32),
        tm_e=tm_e, n_tiles_e=n_tiles_e)
    xs = _dispatch(pos, last_tiles, n_last[None], n_active[None], xrow, tm=_tile(n, 512), tm_e=tm_e,
                   n_tiles_e=n_tiles_e)
    w_gate_up = jnp.concatenate([w_gate[l], w_up[l]], axis=-1).astype(bf)
    y_sorted = _experts(tile_ea, tile_eb, n_active[None], xs, w_gate_up, w_down[l].astype(bf), tm=tm_e)
    out = _combine(pos, h, y_sorted, norm_final[None], tm=_tile(n, 512))
    return out.reshape(b, s, d)
```

```python
import functools
import math

import jax
import jax.numpy as jnp
from jax import lax
from jax.experimental import pallas as pl
from jax.experimental.pallas import tpu as pltpu

RMS_EPS = 1e-6
POOL_WINDOWS = (2, 4, 8, 16)
SB_HEAD_DIM = 64
MEM_HEADS = 4
N_GROUPS = 4
EXPERTS_PER_GROUP = 4
N_EXPERTS = N_GROUPS * EXPERTS_PER_GROUP

LANES_V7X = 128
MXU_DIM_V7X = 256
VMEM_LIMIT_BYTES = 56 * 1024 * 1024

LOG2E = 1.4426950408889634
UNDERFLOW_EXPONENT = 104.0
MASKED_LOGIT = -1e30
POOL_HALO = 16

PAIRS = ((0, 1), (0, 2), (1, 2), (1, 3), (2, 3), (0, 3))
assert sorted(PAIRS) == [(a, b) for a in range(EXPERTS_PER_GROUP) for b in range(a + 1, EXPERTS_PER_GROUP)]
N_BUCKETS = N_GROUPS * len(PAIRS)
ROUTER_ROWS = 32
ROUTER_LANES = LANES_V7X
PAYLOAD_LANES = LANES_V7X
assert N_GROUPS + N_EXPERTS <= ROUTER_ROWS and N_BUCKETS <= ROUTER_ROWS


def _rms(x, gain):
    ms = jnp.mean(x * x, axis=-1, keepdims=True)
    return x * lax.rsqrt(ms + RMS_EPS) * gain


def _dot(a, b):
    return jnp.dot(a, b, preferred_element_type=jnp.float32)


def _dot_nt(a, b):
    return lax.dot_general(a, b, (((1,), (1,)), ((), ())), preferred_element_type=jnp.float32)


def _in_proj_kernel(x_ref, g_ref, w_ref, u_ref, qkv_ref, *, pool_width, attn_width, q_scale):
    xn = _rms(x_ref[...], g_ref[...]).astype(jnp.bfloat16)
    proj = _dot(xn, w_ref[...])
    u_ref[...] = proj[:, :pool_width]
    q = proj[:, pool_width:pool_width + attn_width] * q_scale
    qkv_ref[:, :attn_width] = q.astype(jnp.bfloat16)
    qkv_ref[:, attn_width:] = proj[:, pool_width + attn_width:].astype(jnp.bfloat16)


def _in_proj(x2, gain, w_in_bf16, *, pool_width, attn_width, tm):
    n, d = x2.shape
    in_width = w_in_bf16.shape[1]
    kern = functools.partial(_in_proj_kernel, pool_width=pool_width, attn_width=attn_width,
                             q_scale=1.0 / math.sqrt(SB_HEAD_DIM))
    return pl.pallas_call(
        kern,
        grid=(n // tm,),
        in_specs=[
            pl.BlockSpec((tm, d), lambda i: (i, 0)),
            pl.BlockSpec((1, d), lambda i: (0, 0)),
            pl.BlockSpec((d, in_width), lambda i: (0, 0)),
        ],
        out_specs=[
            pl.BlockSpec((tm, pool_width), lambda i: (i, 0)),
            pl.BlockSpec((tm, 3 * attn_width), lambda i: (i, 0)),
        ],
        out_shape=[
            jax.ShapeDtypeStruct((n, pool_width), jnp.float32),
            jax.ShapeDtypeStruct((n, 3 * attn_width), jnp.bfloat16),
        ],
        compiler_params=pltpu.CompilerParams(
            dimension_semantics=("parallel",), vmem_limit_bytes=VMEM_LIMIT_BYTES),
        name="in_proj",
    )(x2, gain, w_in_bf16)


def _sb_tile(qh, kb, vb, tri, c, tail_mask):
    z = _dot_nt(qh, kb)
    if tail_mask is not None:
        m = tail_mask.shape[1]
        tail = jnp.where(tail_mask, z[:, -m:], MASKED_LOGIT)
        z = tail if m == z.shape[1] else jnp.concatenate([z[:, :-m], tail], axis=1)
    sp = (jnp.maximum(z, 0.0) + jnp.log(1.0 + jnp.exp2(jnp.abs(z) * (-LOG2E)))).astype(jnp.bfloat16)
    n = tri.shape[0]
    chunks, total = [], None
    for k0 in range(z.shape[1] - n, -1, -n):
        cum_k = _dot(sp[:, k0:k0 + n], tri)
        if total is not None:
            cum_k = cum_k + total
        total = cum_k[:, :1]
        chunks.insert(0, cum_k)
    cum = chunks[0] if len(chunks) == 1 else jnp.concatenate(chunks, axis=1)
    w = jnp.exp(z - cum) if c is None else jnp.exp(jnp.minimum(z - cum, 0.0) - c)
    return (total if c is None else c + total), _dot(w.astype(jnp.bfloat16), vb)


def _sb_attn_kernel(q_ref, k_ref, v_ref, tri_ref, o_ref, *, tq, sub, win):
    qi = pl.program_id(2)
    n_sub = tq // sub
    lane = lax.broadcasted_iota(jnp.int32, (tq, LANES_V7X), 1)
    head0 = lane < SB_HEAD_DIM
    q2 = q_ref[...]
    zero = jnp.zeros_like(q2)
    q_heads = (jnp.where(head0, q2, zero), jnp.where(head0, zero, q2))
    q_st = jnp.concatenate(q_heads, axis=0)

    def kv_block(j):
        start = pl.multiple_of(j * tq, tq)
        return k_ref[pl.ds(start, tq), :], v_ref[pl.ds(start, tq), :]

    def stacked_iota(rows, cols):
        row = lax.broadcasted_iota(jnp.int32, (2 * rows, cols), 0)
        col = lax.broadcasted_iota(jnp.int32, (2 * rows, cols), 1)
        return jnp.where(row >= rows, row - rows, row), col

    def first_block():
        row, col = stacked_iota(tq, tq)
        kb, vb = kv_block(0)
        return _sb_tile(q_st, kb, vb, tri_ref[...], None, col < row)

    def windows():
        row, col = stacked_iota(sub, sub)
        causal = col < row
        tri = tri_ref[...]
        parts = []
        for hf in range(n_sub):
            start = pl.multiple_of((qi * n_sub + hf + 1) * sub - win, sub)
            kw = k_ref[pl.ds(start, win), :]
            vw = v_ref[pl.ds(start, win), :]
            q_sub = jnp.concatenate([qh[hf * sub:(hf + 1) * sub] for qh in q_heads], axis=0)
            parts.append(_sb_tile(q_sub, kw, vw, tri, None, causal))
        return tuple(jnp.concatenate([p[i][h * sub:(h + 1) * sub] for h in range(2) for p in parts], axis=0)
                     for i in range(2))

    carry = lax.cond(qi == 0, first_block, windows)

    def live(carry):
        return jnp.min(carry[0], axis=0, keepdims=True)[0, 0] < UNDERFLOW_EXPONENT

    def add_block(carry, j, tail_mask):
        kb, vb = kv_block(j)
        c, acc = _sb_tile(q_st, kb, vb, tri_ref[...], carry[0], tail_mask)
        return c, carry[1] + acc

    def partial_block(carry):
        row, col = stacked_iota(tq, tq)
        fresh = col < (row // sub + 1) * sub + (tq - win)
        return add_block(carry, qi - 1, fresh)

    carry = lax.cond(jnp.logical_and(qi > 0, live(carry)), partial_block, lambda cr: cr, carry)

    def cond(state):
        step, alive, _ = state
        return jnp.logical_and(step < qi - 1, alive > 0)

    def body(state):
        step, _, carry = state
        carry = add_block(carry, qi - 2 - step, None)
        return step + 1, live(carry).astype(jnp.int32), carry

    _, _, carry = lax.while_loop(cond, body, (jnp.int32(0), live(carry).astype(jnp.int32), carry))
    acc = carry[1]
    o_ref[...] = jnp.where(head0, acc[:tq], acc[tq:]).astype(o_ref.dtype)


def _sb_attention(qkv3, *, attn_width, tq, sub, win):
    b, s, _ = qkv3.shape
    assert tq % sub == 0 and win % sub == 0 and sub <= win <= tq + sub, (tq, sub, win)
    n_pairs = attn_width // LANES_V7X
    n = math.gcd(math.gcd(tq, win), MXU_DIM_V7X)
    tri = (lax.broadcasted_iota(jnp.int32, (n, n), 0)
           >= lax.broadcasted_iota(jnp.int32, (n, n), 1)).astype(jnp.bfloat16)

    kern = functools.partial(_sb_attn_kernel, tq=tq, sub=sub, win=win)
    const2 = lambda bi, hp, qi: (0, 0)
    return pl.pallas_call(
        kern,
        grid=(b, n_pairs, s // tq),
        in_specs=[
            pl.BlockSpec((None, tq, LANES_V7X), lambda bi, hp, qi: (bi, qi, hp)),
            pl.BlockSpec((None, s, LANES_V7X), lambda bi, hp, qi: (bi, 0, n_pairs + hp)),
            pl.BlockSpec((None, s, LANES_V7X), lambda bi, hp, qi: (bi, 0, 2 * n_pairs + hp)),
            pl.BlockSpec((n, n), const2),
        ],
        out_specs=pl.BlockSpec((None, tq, LANES_V7X), lambda bi, hp, qi: (bi, qi, hp)),
        out_shape=jax.ShapeDtypeStruct((b, s, attn_width), jnp.bfloat16),
        compiler_params=pltpu.CompilerParams(
            dimension_semantics=("parallel", "parallel", "parallel"),
            vmem_limit_bytes=VMEM_LIMIT_BYTES),
        name="sb_attention",
    )(qkv3, qkv3, qkv3, tri)


def _mem_kv_kernel(m_ref, g_ref, wk_ref, wv_ref, k_ref, v_ref):
    mn = _rms(m_ref[...], g_ref[...]).astype(jnp.bfloat16)
    k_ref[...] = _dot(mn, wk_ref[...]).astype(jnp.bfloat16)
    v_ref[...] = _dot(mn, wv_ref[...]).astype(jnp.bfloat16)


def _mem_kv(mem2, gain, wk, wv, *, tm):
    n, d = mem2.shape
    row = pl.BlockSpec((tm, d), lambda i: (i, 0))
    full = pl.BlockSpec((d, d), lambda i: (0, 0))
    return pl.pallas_call(
        _mem_kv_kernel,
        grid=(n // tm,),
        in_specs=[row, pl.BlockSpec((1, d), lambda i: (0, 0)), full, full],
        out_specs=[row, row],
        out_shape=[jax.ShapeDtypeStruct((n, d), jnp.bfloat16)] * 2,
        compiler_params=pltpu.CompilerParams(
            dimension_semantics=("parallel",), vmem_limit_bytes=VMEM_LIMIT_BYTES),
        name="mem_kv",
    )(mem2, gain, wk, wv)


def _route_t(logits):
    neg = jnp.float32(-jnp.inf)
    big = jnp.float32(ROUTER_ROWS)
    row = lax.broadcasted_iota(jnp.int32, logits.shape, 0).astype(jnp.float32)

    def col_max(mask):
        return jnp.max(jnp.where(mask, logits, neg), axis=0, keepdims=True)

    def first_argmax(mask, mx):
        return jnp.min(jnp.where(mask & (logits == mx), row, big), axis=0, keepdims=True)

    gmask = row < N_GROUPS
    gmax = col_max(gmask)
    gsum = jnp.sum(jnp.where(gmask, jnp.exp(logits - gmax), 0.0), axis=0, keepdims=True)
    g_gate = 1.0 / gsum
    g_idx = first_argmax(gmask, gmax)

    lo = N_GROUPS + EXPERTS_PER_GROUP * g_idx
    emask = (row >= lo) & (row < lo + EXPERTS_PER_GROUP)
    m1 = col_max(emask)
    i1 = first_argmax(emask, m1)
    mask2 = emask & (row != i1)
    m2 = col_max(mask2)
    i2 = first_argmax(mask2, m2)
    esum = jnp.sum(jnp.where(emask, jnp.exp(logits - m1), 0.0), axis=0, keepdims=True)
    p1 = 1.0 / esum
    p2 = jnp.exp(m2 - m1) / esum
    tot = p1 + p2
    w1 = g_gate * (p1 / tot)
    w2 = g_gate * (p2 / tot)

    first = i1 < i2
    la = jnp.where(first, i1, i2) - lo
    lb = jnp.where(first, i2, i1) - lo
    pair = sum(jnp.where((la == a) & (lb == b), float(p), 0.0) for p, (a, b) in enumerate(PAIRS))
    bucket = g_idx * len(PAIRS) + pair
    return bucket, jnp.where(first, w1, w2), jnp.where(first, w2, w1)


def _mix_cross_kernel(x_ref, u_ref, halo_ref, a_ref, wpbd_ref, ps_ref, wo_ref, gc_ref,
                      wq_ref, km_ref, vm_ref, wom_ref, gf_ref, wr_ref, su_ref,
                      h_ref, xrow_ref, bucket_ref, rank_ref, cnt_ref, run_ref, *, tm, tiles_per_seq):
    i = pl.program_id(0)
    pool_width = u_ref.shape[1]
    gw = pool_width // len(POOL_WINDOWS)
    tile_in_seq = i % tiles_per_seq
    first = tile_in_seq == 0
    pos = tile_in_seq * tm + lax.broadcasted_iota(jnp.int32, (tm, 1), 0)

    halo = jnp.where(first, 0.0, halo_ref[...])
    u = u_ref[...]
    pooled = []
    for g, w in enumerate(POOL_WINDOWS):
        ug = u[:, g * gw:(g + 1) * gw]
        ext = jnp.concatenate([halo[:, g * gw:(g + 1) * gw], ug], axis=0)
        shift = 1
        while shift < w:
            ext = ext + pltpu.roll(ext, shift, 0)
            shift *= 2
        win = ext[POOL_HALO:, :]
        inv_count = 1.0 / jnp.minimum(pos + 1, w).astype(jnp.float32)
        pooled.append((win * inv_count - ug).astype(jnp.bfloat16))

    pool_out = _dot(jnp.concatenate(pooled, axis=1), wpbd_ref[...]) * ps_ref[...]
    mixed = jnp.concatenate([pool_out.astype(jnp.bfloat16), a_ref[...]], axis=1)
    h1 = x_ref[...] + _dot(mixed, wo_ref[...])

    hn = _rms(h1, gc_ref[...]).astype(jnp.bfloat16)
    d = h1.shape[1]
    hd = d // MEM_HEADS
    q = (_dot(hn, wq_ref[...]) * (1.0 / math.sqrt(hd))).astype(jnp.bfloat16)
    outs = []
    for hh in range(MEM_HEADS):
        sl = slice(hh * hd, (hh + 1) * hd)
        s = _dot_nt(q[:, sl], km_ref[:, sl])
        e = jnp.exp(s - jnp.max(s, axis=-1, keepdims=True))
        p = e * (1.0 / jnp.sum(e, axis=-1, keepdims=True))
        outs.append(_dot(p.astype(jnp.bfloat16), vm_ref[:, sl]))
    o = jnp.concatenate(outs, axis=-1).astype(jnp.bfloat16)
    h2 = h1 + _dot(o, wom_ref[...])
    h_ref[...] = h2

    xn = _rms(h2, gf_ref[...])
    xrow_ref[:, :d] = xn
    x_hi = xn.astype(jnp.bfloat16)
    x_lo = (xn - x_hi.astype(jnp.float32)).astype(jnp.bfloat16)
    both = _dot(x_hi, wr_ref[...])
    logits_tm = both[:, :ROUTER_LANES] + both[:, ROUTER_LANES:] + _dot(x_lo, wr_ref[:, :ROUTER_LANES])
    logits = logits_tm.T[:ROUTER_ROWS]
    bucket, w_a, w_b = _route_t(logits)

    @pl.when(i == 0)
    def _():
        run_ref[...] = jnp.zeros_like(run_ref)

    brow = lax.broadcasted_iota(jnp.int32, (ROUTER_ROWS, tm), 0).astype(jnp.float32)
    onehot = (brow == bucket).astype(jnp.float32)
    before = _dot(onehot.astype(jnp.bfloat16), su_ref[...])
    run = run_ref[...]
    rank = jnp.sum(onehot * (before + run[:, :1]), axis=0, keepdims=True)
    run = run + jnp.sum(onehot, axis=1, keepdims=True)
    run_ref[...] = run
    cnt_ref[...] = run
    bucket_ref[...] = bucket.astype(jnp.int32)
    rank_ref[...] = rank.astype(jnp.int32)

    prow = lax.broadcasted_iota(jnp.int32, (PAYLOAD_LANES, tm), 0)
    payload_t = jnp.where(prow == 0, w_a, jnp.where(prow == 1, w_b, 0.0))
    xrow_ref[:, d:] = payload_t.T


def _mix_cross(x2, u, attn, w_pool_bd, pool_scale, w_out, g_cross, wq, kmem, vmem, wom, g_ffn, w_router,
               *, tm, seq, mem_len):
    n, d = x2.shape
    pool_width = u.shape[1]
    attn_width = attn.shape[1]
    tiles_per_seq = seq // tm
    n_tiles = n // tm
    halo_blocks = tm // POOL_HALO
    su = (lax.broadcasted_iota(jnp.int32, (tm, tm), 0)
          < lax.broadcasted_iota(jnp.int32, (tm, tm), 1)).astype(jnp.bfloat16)
    kern = functools.partial(_mix_cross_kernel, tm=tm, tiles_per_seq=tiles_per_seq)
    const2 = lambda i: (0, 0)
    mem_map = lambda i: (i // tiles_per_seq, 0)
    return pl.pallas_call(
        kern,
        grid=(n_tiles,),
        in_specs=[
            pl.BlockSpec((tm, d), lambda i: (i, 0)),
            pl.BlockSpec((tm, pool_width), lambda i: (i, 0)),
            pl.BlockSpec((POOL_HALO, pool_width), lambda i: (jnp.maximum(i * halo_blocks - 1, 0), 0)),
            pl.BlockSpec((tm, attn_width), lambda i: (i, 0)),
            pl.BlockSpec(w_pool_bd.shape, const2),
            pl.BlockSpec((1, pool_width), const2),
            pl.BlockSpec(w_out.shape, const2),
            pl.BlockSpec((1, d), const2),
            pl.BlockSpec((d, d), const2),
            pl.BlockSpec((mem_len, d), mem_map),
            pl.BlockSpec((mem_len, d), mem_map),
            pl.BlockSpec((d, d), const2),
            pl.BlockSpec((1, d), const2),
            pl.BlockSpec((d, 2 * ROUTER_LANES), const2),
            pl.BlockSpec((tm, tm), const2),
        ],
        out_specs=[
            pl.BlockSpec((tm, d), lambda i: (i, 0)),
            pl.BlockSpec((tm, d + PAYLOAD_LANES), lambda i: (i, 0)),
            pl.BlockSpec((None, 1, tm), lambda i: (i, 0, 0)),
            pl.BlockSpec((None, 1, tm), lambda i: (i, 0, 0)),
            pl.BlockSpec((ROUTER_ROWS, LANES_V7X), const2),
        ],
        out_shape=[
            jax.ShapeDtypeStruct((n, d), jnp.float32),
            jax.ShapeDtypeStruct((n, d + PAYLOAD_LANES), jnp.float32),
            jax.ShapeDtypeStruct((n_tiles, 1, tm), jnp.int32),
            jax.ShapeDtypeStruct((n_tiles, 1, tm), jnp.int32),
            jax.ShapeDtypeStruct((ROUTER_ROWS, LANES_V7X), jnp.float32),
        ],
        scratch_shapes=[pltpu.VMEM((ROUTER_ROWS, LANES_V7X), jnp.float32)],
        compiler_params=pltpu.CompilerParams(
            dimension_semantics=("arbitrary",), vmem_limit_bytes=VMEM_LIMIT_BYTES),
        name="mix_cross",
    )(x2, u, u, attn, w_pool_bd, pool_scale, w_out, g_cross, wq, kmem, vmem, wom, g_ffn, w_router, su)


def _dispatch_kernel(pos_ref, last_ref, n_last_ref, na_ref, x_ref, xs_ref, stage, zeros, sems, fill_sem,
                     *, tm, n_tiles, n_tiles_e):
    i = pl.program_id(0)
    slot = i % 2
    base = i * tm
    tm_e = zeros.shape[0]

    def fill_copy(j):
        return pltpu.make_async_copy(zeros, xs_ref.at[pl.ds(pl.multiple_of(j * tm_e, tm_e), tm_e), :], fill_sem)

    def for_range(lo, hi, fn):
        def body(k, carry):
            fn(k)
            return carry
        lax.fori_loop(lo, hi, body, 0)

    @pl.when(i == 0)
    def _():
        zeros[...] = jnp.zeros_like(zeros)
        for_range(0, n_last_ref[0], lambda k: fill_copy(last_ref[k]).start())
        for_range(na_ref[0], n_tiles_e, lambda j: fill_copy(j).start())
        for_range(0, n_last_ref[0], lambda k: fill_copy(0).wait())
        for_range(na_ref[0], n_tiles_e, lambda j: fill_copy(0).wait())

    def row_copy(slot, r, p):
        return pltpu.make_async_copy(stage.at[slot, pl.ds(r, 1), :], xs_ref.at[pl.ds(p, 1), :], sems.at[slot])

    def wait_slot(slot):
        pltpu.make_async_copy(stage.at[slot], xs_ref.at[pl.ds(0, tm), :], sems.at[slot]).wait()

    stage[slot] = x_ref[...]
    for r in range(tm):
        row_copy(slot, r, pos_ref[base + r]).start(priority=r % 2)

    @pl.when(i > 0)
    def _():
        wait_slot(1 - slot)

    @pl.when(i == n_tiles - 1)
    def _():
        wait_slot(slot)


def _dispatch(pos, last_tiles, n_last, n_active, xrow, *, tm, tm_e, n_tiles_e):
    n, width = xrow.shape
    return pl.pallas_call(
        functools.partial(_dispatch_kernel, tm=tm, n_tiles=n // tm, n_tiles_e=n_tiles_e),
        grid_spec=pltpu.PrefetchScalarGridSpec(
            num_scalar_prefetch=4,
            grid=(n // tm,),
            in_specs=[pl.BlockSpec((tm, width), lambda i, *_: (i, 0))],
            out_specs=pl.BlockSpec(memory_space=pl.ANY),
            scratch_shapes=[pltpu.VMEM((2, tm, width), xrow.dtype), pltpu.VMEM((tm_e, width), xrow.dtype),
                            pltpu.SemaphoreType.DMA((2,)), pltpu.SemaphoreType.DMA],
        ),
        out_shape=jax.ShapeDtypeStruct((n_tiles_e * tm_e, width), xrow.dtype),
        compiler_params=pltpu.CompilerParams(
            dimension_semantics=("arbitrary",), vmem_limit_bytes=VMEM_LIMIT_BYTES),
        name="moe_dispatch",
    )(pos, last_tiles, n_last, n_active, xrow)


def _expert_kernel(ea_ref, eb_ref, na_ref, xs_ref, wgua_ref, wda_ref, wgub_ref, wdb_ref, y_ref, *, d):
    del ea_ref, eb_ref
    active = pl.program_id(0) < na_ref[0]

    @pl.when(jnp.logical_not(active))
    def _():
        y_ref[...] = jnp.zeros_like(y_ref)

    @pl.when(active)
    def _():
        x = xs_ref[:, :d].astype(jnp.bfloat16)

        def mlp(wgu_ref, wd_ref):
            gu = _dot(x, wgu_ref[...])
            ff = gu.shape[1] // 2
            gate, up = gu[:, :ff], gu[:, ff:]
            hmid = (gate * (1.0 / (1.0 + jnp.exp(-gate)))) * up
            return _dot(hmid.astype(jnp.bfloat16), wd_ref[...])

        y_ref[...] = (xs_ref[:, d:d + 1] * mlp(wgua_ref, wda_ref)
                      + xs_ref[:, d + 1:d + 2] * mlp(wgub_ref, wdb_ref))


def _experts(tile_ea, tile_eb, n_active, xs, wgu, wd, *, tm):
    n_pad, width = xs.shape
    d, ff = wd.shape[2], wd.shape[1]
    row_map = lambda j, ea, eb, na: (jnp.minimum(j, na[0] - 1), 0)
    a_map = lambda j, ea, eb, na: (ea[j], 0, 0)
    b_map = lambda j, ea, eb, na: (eb[j], 0, 0)
    return pl.pallas_call(
        functools.partial(_expert_kernel, d=d),
        grid_spec=pltpu.PrefetchScalarGridSpec(
            num_scalar_prefetch=3,
            grid=(n_pad // tm,),
            in_specs=[
                pl.BlockSpec((tm, width), row_map),
                pl.BlockSpec((None, d, 2 * ff), a_map),
                pl.BlockSpec((None, ff, d), a_map),
                pl.BlockSpec((None, d, 2 * ff), b_map),
                pl.BlockSpec((None, ff, d), b_map),
            ],
            out_specs=pl.BlockSpec((tm, d), lambda j, ea, eb, na: (j, 0)),
        ),
        out_shape=jax.ShapeDtypeStruct((n_pad, d), jnp.float32),
        compiler_params=pltpu.CompilerParams(
            dimension_semantics=("arbitrary",), vmem_limit_bytes=VMEM_LIMIT_BYTES),
        name="moe_experts",
    )(tile_ea, tile_eb, n_active, xs, wgu, wd, wgu, wd)


def _combine_kernel(pos_ref, h_ref, y_ref, g_ref, o_ref, ybuf, sems, *, tm, n_tiles):
    i = pl.program_id(0)

    def row_copy(slot, r, p):
        return pltpu.make_async_copy(y_ref.at[pl.ds(p, 1), :], ybuf.at[slot, pl.ds(r, 1), :], sems.at[slot])

    @pl.when(i < n_tiles)
    def _():
        slot = i % 2
        for r in range(tm):
            row_copy(slot, r, pos_ref[i * tm + r]).start(priority=r % 2)

    @pl.when(i > 0)
    def _():
        slot = (i - 1) % 2
        pltpu.make_async_copy(y_ref.at[pl.ds(0, tm), :], ybuf.at[slot], sems.at[slot]).wait()
        o_ref[...] = _rms(h_ref[...] + ybuf[slot], g_ref[...])


def _combine(pos, h, y_sorted, g_final, *, tm):
    n, d = h.shape
    n_tiles = n // tm
    prev_tile = lambda i, pos: (jnp.maximum(i - 1, 0), 0)
    return pl.pallas_call(
        functools.partial(_combine_kernel, tm=tm, n_tiles=n_tiles),
        grid_spec=pltpu.PrefetchScalarGridSpec(
            num_scalar_prefetch=1,
            grid=(n_tiles + 1,),
            in_specs=[
                pl.BlockSpec((tm, d), prev_tile),
                pl.BlockSpec(memory_space=pl.ANY),
                pl.BlockSpec((1, d), lambda i, pos: (0, 0)),
            ],
            out_specs=pl.BlockSpec((tm, d), prev_tile),
            scratch_shapes=[pltpu.VMEM((2, tm, d), jnp.float32), pltpu.SemaphoreType.DMA((2,))],
        ),
        out_shape=jax.ShapeDtypeStruct((n, d), jnp.float32),
        compiler_params=pltpu.CompilerParams(
            dimension_semantics=("arbitrary",), vmem_limit_bytes=VMEM_LIMIT_BYTES),
        name="moe_combine",
    )(pos, h, y_sorted, g_final)


def _tile(n, pref):
    t = min(n, pref)
    assert n % t == 0, (n, t)
    return t


def _sorted_layout(bucket, rank, counts, *, tm_e, n_tiles_e):
    seg_tiles = (counts + tm_e - 1) // tm_e
    seg_end = jnp.cumsum(seg_tiles)
    seg_start = seg_end - seg_tiles
    pos = (seg_start * tm_e)[bucket] + rank
    nonempty = seg_tiles > 0
    last_tiles = jnp.sort(jnp.where(nonempty, seg_end - 1, n_tiles_e))
    n_active = seg_end[-1]
    tile = jnp.minimum(jnp.arange(n_tiles_e, dtype=jnp.int32), n_active - 1)
    tile_bucket = jnp.sum((tile[:, None] >= seg_end[None, :]).astype(jnp.int32), axis=1)
    group, pair = tile_bucket // len(PAIRS), tile_bucket % len(PAIRS)
    pair_a = jnp.array([p[0] for p in PAIRS], jnp.int32)
    pair_b = jnp.array([p[1] for p in PAIRS], jnp.int32)
    tile_ea = group * EXPERTS_PER_GROUP + pair_a[pair]
    tile_eb = group * EXPERTS_PER_GROUP + pair_b[pair]
    i32 = lambda a: a.astype(jnp.int32)
    return i32(pos), i32(last_tiles), i32(jnp.sum(nonempty)), i32(tile_ea), i32(tile_eb), i32(n_active)


def kernel(x, mem, norm_mix, w_in, w_pool, pool_scale, w_out, norm_cross, norm_mem, w_q_mem, w_k_mem,
           w_v_mem, w_o_mem, norm_ffn, w_group, w_expert, w_gate, w_up, w_down, norm_final):
    b, s, d = x.shape
    mem_len = mem.shape[1]
    depth = norm_mix.shape[0]
    pool_width = pool_scale.shape[1]
    attn_width = w_out.shape[1] - pool_width
    bf = jnp.bfloat16
    n = b * s

    assert depth == 1, "single-layer problem: the final RMSNorm is fused into the combine kernel"
    l = 0
    h = x.reshape(n, d)
    u, qkv = _in_proj(h, norm_mix[l][None], w_in[l].astype(bf),
                      pool_width=pool_width, attn_width=attn_width, tm=_tile(n, 1024))
    tq = _tile(s, 2 * MXU_DIM_V7X)
    sub = tq // 2
    attn = _sb_attention(qkv.reshape(b, s, 3 * attn_width), attn_width=attn_width, tq=tq, sub=sub, win=2 * sub)
    kmem, vmem = _mem_kv(mem.reshape(b * mem_len, d), norm_mem[l][None],
                         w_k_mem[l].astype(bf), w_v_mem[l].astype(bf), tm=mem_len)

    w_router = jnp.concatenate([w_group[l], w_expert[l]], axis=1)
    w_router = jnp.pad(w_router, ((0, 0), (0, ROUTER_LANES - w_router.shape[1])))
    wr_hi = w_router.astype(bf)
    wr_lo = (w_router - wr_hi.astype(jnp.float32)).astype(bf)
    w_pool_bd = jax.scipy.linalg.block_diag(*w_pool[l]).astype(bf)
    h, xrow, bucket, rank, counts = _mix_cross(
        h, u, attn.reshape(n, attn_width), w_pool_bd, pool_scale[l][None], w_out[l].astype(bf),
        norm_cross[l][None], w_q_mem[l].astype(bf), kmem, vmem, w_o_mem[l].astype(bf),
        norm_ffn[l][None], jnp.concatenate([wr_hi, wr_lo], axis=1), tm=_tile(s, 512), seq=s, mem_len=mem_len)

    tm_e = _tile(n, MXU_DIM_V7X)
    n_tiles_e = -(-(n + N_BUCKETS * (tm_e - 1)) // tm_e)
    pos, last_tiles, n_last, tile_ea, tile_eb, n_active = _sorted_layout(
        bucket.reshape(n), rank.reshape(n), counts[:N_BUCKETS, 0].astype(jnp.int32),
        tm_e=tm_e, n_tiles_e=n_tiles_e)
    xs = _dispatch(pos, last_tiles, n_last[None], n_active[None], xrow, tm=_tile(n, 512), tm_e=tm_e,
                   n_tiles_e=n_tiles_e)
    w_gate_up = jnp.concatenate([w_gate[l], w_up[l]], axis=-1).astype(bf)
    y_sorted = _experts(tile_ea, tile_eb, n_active[None], xs, w_gate_up, w_down[l].astype(bf), tm=tm_e)
    out = _combine(pos, h, y_sorted, norm_final[None], tm=_tile(n, 512))
    return out.reshape(b, s, d)
```

```python
import functools
import math

import jax
import jax.numpy as jnp
from jax import lax
from jax.experimental import pallas as pl
from jax.experimental.pallas import tpu as pltpu

RMS_EPS = 1e-6
POOL_WINDOWS = (2, 4, 8, 16)
SB_HEAD_DIM = 64
MEM_HEADS = 4
N_GROUPS = 4
EXPERTS_PER_GROUP = 4
N_EXPERTS = N_GROUPS * EXPERTS_PER_GROUP

LANES_V7X = 128
MXU_DIM_V7X = 256
VMEM_LIMIT_BYTES = 56 * 1024 * 1024

LOG2E = 1.4426950408889634
UNDERFLOW_EXPONENT = 104.0
MASKED_LOGIT = -1e30
POOL_HALO = 16

PAIRS = ((0, 1), (0, 2), (1, 2), (1, 3), (2, 3), (0, 3))
assert sorted(PAIRS) == [(a, b) for a in range(EXPERTS_PER_GROUP) for b in range(a + 1, EXPERTS_PER_GROUP)]
N_BUCKETS = N_GROUPS * len(PAIRS)
ROUTER_ROWS = 32
ROUTER_LANES = LANES_V7X
PAYLOAD_LANES = LANES_V7X
assert N_GROUPS + N_EXPERTS <= ROUTER_ROWS and N_BUCKETS <= ROUTER_ROWS


def _rms(x, gain):
    ms = jnp.mean(x * x, axis=-1, keepdims=True)
    return x * lax.rsqrt(ms + RMS_EPS) * gain


def _dot(a, b):
    return jnp.dot(a, b, preferred_element_type=jnp.float32)


def _dot_nt(a, b):
    return lax.dot_general(a, b, (((1,), (1,)), ((), ())), preferred_element_type=jnp.float32)


def _in_proj_kernel(x_ref, g_ref, w_ref, u_ref, qkv_ref, *, pool_width, attn_width, q_scale):
    xn = _rms(x_ref[...], g_ref[...]).astype(jnp.bfloat16)
    proj = _dot(xn, w_ref[...])
    u_ref[...] = proj[:, :pool_width]
    q = proj[:, pool_width:pool_width + attn_width] * q_scale
    qkv = jnp.concatenate([q, proj[:, pool_width + attn_width:]], axis=1).astype(jnp.bfloat16)
    for j in range(qkv_ref.shape[0]):
        qkv_ref[j] = qkv[:, j * LANES_V7X:(j + 1) * LANES_V7X]


def _in_proj(x2, gain, w_in_bf16, *, pool_width, attn_width, tm):
    n, d = x2.shape
    in_width = w_in_bf16.shape[1]
    n_slabs = 3 * attn_width // LANES_V7X
    kern = functools.partial(_in_proj_kernel, pool_width=pool_width, attn_width=attn_width,
                             q_scale=1.0 / math.sqrt(SB_HEAD_DIM))
    return pl.pallas_call(
        kern,
        grid=(n // tm,),
        in_specs=[
            pl.BlockSpec((tm, d), lambda i: (i, 0)),
            pl.BlockSpec((1, d), lambda i: (0, 0)),
            pl.BlockSpec((d, in_width), lambda i: (0, 0)),
        ],
        out_specs=[
            pl.BlockSpec((tm, pool_width), lambda i: (i, 0)),
            pl.BlockSpec((n_slabs, tm, LANES_V7X), lambda i: (0, i, 0)),
        ],
        out_shape=[
            jax.ShapeDtypeStruct((n, pool_width), jnp.float32),
            jax.ShapeDtypeStruct((n_slabs, n, LANES_V7X), jnp.bfloat16),
        ],
        compiler_params=pltpu.CompilerParams(
            dimension_semantics=("parallel",), vmem_limit_bytes=VMEM_LIMIT_BYTES),
        name="in_proj",
    )(x2, gain, w_in_bf16)


def _sb_tile(qh, kb, vb, tri, c, tail_mask):
    z = _dot_nt(qh, kb)
    if tail_mask is not None:
        m = tail_mask.shape[1]
        tail = jnp.where(tail_mask, z[:, -m:], MASKED_LOGIT)
        z = tail if m == z.shape[1] else jnp.concatenate([z[:, :-m], tail], axis=1)
    sp = (jnp.maximum(z, 0.0) + jnp.log(1.0 + jnp.exp2(jnp.abs(z) * (-LOG2E)))).astype(jnp.bfloat16)
    n = tri.shape[0]
    chunks, total = [], None
    for k0 in range(z.shape[1] - n, -1, -n):
        cum_k = _dot(sp[:, k0:k0 + n], tri)
        if total is not None:
            cum_k = cum_k + total
        total = cum_k[:, :1]
        chunks.insert(0, cum_k)
    cum = chunks[0] if len(chunks) == 1 else jnp.concatenate(chunks, axis=1)
    w = jnp.exp(z - cum) if c is None else jnp.exp(jnp.minimum(z - cum, 0.0) - c)
    return (total if c is None else c + total), _dot(w.astype(jnp.bfloat16), vb)


def _sb_attn_kernel(q_ref, k_ref, v_ref, tri_ref, o_ref, *, tq, sub, win):
    qi = pl.program_id(2)
    n_sub = tq // sub
    lane = lax.broadcasted_iota(jnp.int32, (tq, LANES_V7X), 1)
    head0 = lane < SB_HEAD_DIM
    q2 = q_ref[...]
    zero = jnp.zeros_like(q2)
    q_heads = (jnp.where(head0, q2, zero), jnp.where(head0, zero, q2))
    q_st = jnp.concatenate(q_heads, axis=0)

    def kv_block(j):
        start = pl.multiple_of(j * tq, tq)
        return k_ref[pl.ds(start, tq), :], v_ref[pl.ds(start, tq), :]

    def stacked_iota(rows, cols):
        row = lax.broadcasted_iota(jnp.int32, (2 * rows, cols), 0)
        col = lax.broadcasted_iota(jnp.int32, (2 * rows, cols), 1)
        return jnp.where(row >= rows, row - rows, row), col

    def first_block():
        row, col = stacked_iota(tq, tq)
        kb, vb = kv_block(0)
        return _sb_tile(q_st, kb, vb, tri_ref[...], None, col < row)

    def windows():
        row, col = stacked_iota(sub, sub)
        causal = col < row
        tri = tri_ref[...]
        parts = []
        for hf in range(n_sub):
            start = pl.multiple_of((qi * n_sub + hf + 1) * sub - win, sub)
            kw = k_ref[pl.ds(start, win), :]
            vw = v_ref[pl.ds(start, win), :]
            q_sub = jnp.concatenate([qh[hf * sub:(hf + 1) * sub] for qh in q_heads], axis=0)
            parts.append(_sb_tile(q_sub, kw, vw, tri, None, causal))
        return tuple(jnp.concatenate([p[i][h * sub:(h + 1) * sub] for h in range(2) for p in parts], axis=0)
                     for i in range(2))

    carry = lax.cond(qi == 0, first_block, windows)

    def live(carry):
        return jnp.min(carry[0], axis=0, keepdims=True)[0, 0] < UNDERFLOW_EXPONENT

    def add_block(carry, j, tail_mask):
        kb, vb = kv_block(j)
        c, acc = _sb_tile(q_st, kb, vb, tri_ref[...], carry[0], tail_mask)
        return c, carry[1] + acc

    def partial_block(carry):
        row, col = stacked_iota(tq, tq)
        fresh = col < (row // sub + 1) * sub + (tq - win)
        return add_block(carry, qi - 1, fresh)

    carry = lax.cond(jnp.logical_and(qi > 0, live(carry)), partial_block, lambda cr: cr, carry)

    def cond(state):
        step, alive, _ = state
        return jnp.logical_and(step < qi - 1, alive > 0)

    def body(state):
        step, _, carry = state
        carry = add_block(carry, qi - 2 - step, None)
        return step + 1, live(carry).astype(jnp.int32), carry

    _, _, carry = lax.while_loop(cond, body, (jnp.int32(0), live(carry).astype(jnp.int32), carry))
    acc = carry[1]
    o_ref[...] = jnp.where(head0, acc[:tq], acc[tq:]).astype(o_ref.dtype)


def _sb_attention(qkv, *, tq, sub, win):
    n_slabs, b, s, _ = qkv.shape
    assert tq % sub == 0 and win % sub == 0 and sub <= win <= tq + sub, (tq, sub, win)
    n_pairs = n_slabs // 3
    n = math.gcd(math.gcd(tq, win), MXU_DIM_V7X)
    tri = (lax.broadcasted_iota(jnp.int32, (n, n), 0)
           >= lax.broadcasted_iota(jnp.int32, (n, n), 1)).astype(jnp.bfloat16)

    kern = functools.partial(_sb_attn_kernel, tq=tq, sub=sub, win=win)
    const2 = lambda bi, hp, qi: (0, 0)
    return pl.pallas_call(
        kern,
        grid=(b, n_pairs, s // tq),
        in_specs=[
            pl.BlockSpec((None, None, tq, LANES_V7X), lambda bi, hp, qi: (hp, bi, qi, 0)),
            pl.BlockSpec((None, None, s, LANES_V7X), lambda bi, hp, qi: (n_pairs + hp, bi, 0, 0)),
            pl.BlockSpec((None, None, s, LANES_V7X), lambda bi, hp, qi: (2 * n_pairs + hp, bi, 0, 0)),
            pl.BlockSpec((n, n), const2),
        ],
        out_specs=pl.BlockSpec((None, None, tq, LANES_V7X), lambda bi, hp, qi: (hp, bi, qi, 0)),
        out_shape=jax.ShapeDtypeStruct((n_pairs, b, s, LANES_V7X), jnp.bfloat16),
        compiler_params=pltpu.CompilerParams(
            dimension_semantics=("parallel", "parallel", "parallel"),
            vmem_limit_bytes=VMEM_LIMIT_BYTES),
        name="sb_attention",
    )(qkv, qkv, qkv, tri)


def _mem_kv_kernel(m_ref, g_ref, wk_ref, wv_ref, k_ref, v_ref):
    mn = _rms(m_ref[...], g_ref[...]).astype(jnp.bfloat16)
    k_ref[...] = _dot(mn, wk_ref[...]).astype(jnp.bfloat16)
    v_ref[...] = _dot(mn, wv_ref[...]).astype(jnp.bfloat16)


def _mem_kv(mem2, gain, wk, wv, *, tm):
    n, d = mem2.shape
    row = pl.BlockSpec((tm, d), lambda i: (i, 0))
    full = pl.BlockSpec((d, d), lambda i: (0, 0))
    return pl.pallas_call(
        _mem_kv_kernel,
        grid=(n // tm,),
        in_specs=[row, pl.BlockSpec((1, d), lambda i: (0, 0)), full, full],
        out_specs=[row, row],
        out_shape=[jax.ShapeDtypeStruct((n, d), jnp.bfloat16)] * 2,
        compiler_params=pltpu.CompilerParams(
            dimension_semantics=("parallel",), vmem_limit_bytes=VMEM_LIMIT_BYTES),
        name="mem_kv",
    )(mem2, gain, wk, wv)


def _route_t(logits):
    neg = jnp.float32(-jnp.inf)
    big = jnp.float32(ROUTER_ROWS)
    row = lax.broadcasted_iota(jnp.int32, logits.shape, 0).astype(jnp.float32)

    def col_max(mask):
        return jnp.max(jnp.where(mask, logits, neg), axis=0, keepdims=True)

    def first_argmax(mask, mx):
        return jnp.min(jnp.where(mask & (logits == mx), row, big), axis=0, keepdims=True)

    gmask = row < N_GROUPS
    gmax = col_max(gmask)
    gsum = jnp.sum(jnp.where(gmask, jnp.exp(logits - gmax), 0.0), axis=0, keepdims=True)
    g_gate = 1.0 / gsum
    g_idx = first_argmax(gmask, gmax)

    lo = N_GROUPS + EXPERTS_PER_GROUP * g_idx
    emask = (row >= lo) & (row < lo + EXPERTS_PER_GROUP)
    m1 = col_max(emask)
    i1 = first_argmax(emask, m1)
    mask2 = emask & (row != i1)
    m2 = col_max(mask2)
    i2 = first_argmax(mask2, m2)
    esum = jnp.sum(jnp.where(emask, jnp.exp(logits - m1), 0.0), axis=0, keepdims=True)
    p1 = 1.0 / esum
    p2 = jnp.exp(m2 - m1) / esum
    tot = p1 + p2
    w1 = g_gate * (p1 / tot)
    w2 = g_gate * (p2 / tot)

    first = i1 < i2
    la = jnp.where(first, i1, i2) - lo
    lb = jnp.where(first, i2, i1) - lo
    pair = sum(jnp.where((la == a) & (lb == b), float(p), 0.0) for p, (a, b) in enumerate(PAIRS))
    bucket = g_idx * len(PAIRS) + pair
    return bucket, jnp.where(first, w1, w2), jnp.where(first, w2, w1)


def _mix_cross_kernel(x_ref, u_ref, halo_ref, a_ref, wpbd_ref, ps_ref, wo_ref, gc_ref,
                      wq_ref, km_ref, vm_ref, wom_ref, gf_ref, wr_ref, su_ref,
                      h_ref, xrow_ref, bucket_ref, rank_ref, cnt_ref, run_ref, *, tm, tiles_per_seq):
    i = pl.program_id(0)
    pool_width = u_ref.shape[1]
    gw = pool_width // len(POOL_WINDOWS)
    tile_in_seq = i % tiles_per_seq
    first = tile_in_seq == 0
    pos = tile_in_seq * tm + lax.broadcasted_iota(jnp.int32, (tm, 1), 0)

    halo = jnp.where(first, 0.0, halo_ref[...])
    u = u_ref[...]
    pooled = []
    for g, w in enumerate(POOL_WINDOWS):
        ug = u[:, g * gw:(g + 1) * gw]
        ext = jnp.concatenate([halo[:, g * gw:(g + 1) * gw], ug], axis=0)
        shift = 1
        while shift < w:
            ext = ext + pltpu.roll(ext, shift, 0)
            shift *= 2
        win = ext[POOL_HALO:, :]
        inv_count = 1.0 / jnp.minimum(pos + 1, w).astype(jnp.float32)
        pooled.append((win * inv_count - ug).astype(jnp.bfloat16))

    pool_out = _dot(jnp.concatenate(pooled, axis=1), wpbd_ref[...]) * ps_ref[...]
    mixed = jnp.concatenate([pool_out.astype(jnp.bfloat16)] + [a_ref[j] for j in range(a_ref.shape[0])], axis=1)
    h1 = x_ref[...] + _dot(mixed, wo_ref[...])

    hn = _rms(h1, gc_ref[...]).astype(jnp.bfloat16)
    d = h1.shape[1]
    hd = d // MEM_HEADS
    q = (_dot(hn, wq_ref[...]) * (1.0 / math.sqrt(hd))).astype(jnp.bfloat16)
    outs = []
    for hh in range(MEM_HEADS):
        sl = slice(hh * hd, (hh + 1) * hd)
        s = _dot_nt(q[:, sl], km_ref[:, sl])
        e = jnp.exp(s - jnp.max(s, axis=-1, keepdims=True))
        p = e * (1.0 / jnp.sum(e, axis=-1, keepdims=True))
        outs.append(_dot(p.astype(jnp.bfloat16), vm_ref[:, sl]))
    o = jnp.concatenate(outs, axis=-1).astype(jnp.bfloat16)
    h2 = h1 + _dot(o, wom_ref[...])
    h_ref[...] = h2

    xn = _rms(h2, gf_ref[...])
    xrow_ref[:, :d] = xn
    x_hi = xn.astype(jnp.bfloat16)
    x_lo = (xn - x_hi.astype(jnp.float32)).astype(jnp.bfloat16)
    both = _dot(x_hi, wr_ref[...])
    logits_tm = both[:, :ROUTER_LANES] + both[:, ROUTER_LANES:] + _dot(x_lo, wr_ref[:, :ROUTER_LANES])
    logits = logits_tm.T[:ROUTER_ROWS]
    bucket, w_a, w_b = _route_t(logits)

    @pl.when(i == 0)
    def _():
        run_ref[...] = jnp.zeros_like(run_ref)

    brow = lax.broadcasted_iota(jnp.int32, (ROUTER_ROWS, tm), 0).astype(jnp.float32)
    onehot = (brow == bucket).astype(jnp.float32)
    before = _dot(onehot.astype(jnp.bfloat16), su_ref[...])
    run = run_ref[...]
    rank = jnp.sum(onehot * (before + run[:, :1]), axis=0, keepdims=True)
    run = run + jnp.sum(onehot, axis=1, keepdims=True)
    run_ref[...] = run
    cnt_ref[...] = run
    bucket_ref[...] = bucket.astype(jnp.int32)
    rank_ref[...] = rank.astype(jnp.int32)

    prow = lax.broadcasted_iota(jnp.int32, (PAYLOAD_LANES, tm), 0)
    payload_t = jnp.where(prow == 0, w_a, jnp.where(prow == 1, w_b, 0.0))
    xrow_ref[:, d:] = payload_t.T


def _mix_cross(x2, u, attn, w_pool_bd, pool_scale, w_out, g_cross, wq, kmem, vmem, wom, g_ffn, w_router,
               *, tm, seq, mem_len):
    n, d = x2.shape
    pool_width = u.shape[1]
    tiles_per_seq = seq // tm
    n_tiles = n // tm
    halo_blocks = tm // POOL_HALO
    su = (lax.broadcasted_iota(jnp.int32, (tm, tm), 0)
          < lax.broadcasted_iota(jnp.int32, (tm, tm), 1)).astype(jnp.bfloat16)
    kern = functools.partial(_mix_cross_kernel, tm=tm, tiles_per_seq=tiles_per_seq)
    const2 = lambda i: (0, 0)
    mem_map = lambda i: (i // tiles_per_seq, 0)
    return pl.pallas_call(
        kern,
        grid=(n_tiles,),
        in_specs=[
            pl.BlockSpec((tm, d), lambda i: (i, 0)),
            pl.BlockSpec((tm, pool_width), lambda i: (i, 0)),
            pl.BlockSpec((POOL_HALO, pool_width), lambda i: (jnp.maximum(i * halo_blocks - 1, 0), 0)),
            pl.BlockSpec((attn.shape[0], tm, LANES_V7X), lambda i: (0, i, 0)),
            pl.BlockSpec(w_pool_bd.shape, const2),
            pl.BlockSpec((1, pool_width), const2),
            pl.BlockSpec(w_out.shape, const2),
            pl.BlockSpec((1, d), const2),
            pl.BlockSpec((d, d), const2),
            pl.BlockSpec((mem_len, d), mem_map),
            pl.BlockSpec((mem_len, d), mem_map),
            pl.BlockSpec((d, d), const2),
            pl.BlockSpec((1, d), const2),
            pl.BlockSpec((d, 2 * ROUTER_LANES), const2),
            pl.BlockSpec((tm, tm), const2),
        ],
        out_specs=[
            pl.BlockSpec((tm, d), lambda i: (i, 0)),
            pl.BlockSpec((tm, d + PAYLOAD_LANES), lambda i: (i, 0)),
            pl.BlockSpec((None, 1, tm), lambda i: (i, 0, 0)),
            pl.BlockSpec((None, 1, tm), lambda i: (i, 0, 0)),
            pl.BlockSpec((ROUTER_ROWS, LANES_V7X), const2),
        ],
        out_shape=[
            jax.ShapeDtypeStruct((n, d), jnp.float32),
            jax.ShapeDtypeStruct((n, d + PAYLOAD_LANES), jnp.float32),
            jax.ShapeDtypeStruct((n_tiles, 1, tm), jnp.int32),
            jax.ShapeDtypeStruct((n_tiles, 1, tm), jnp.int32),
            jax.ShapeDtypeStruct((ROUTER_ROWS, LANES_V7X), jnp.float32),
        ],
        scratch_shapes=[pltpu.VMEM((ROUTER_ROWS, LANES_V7X), jnp.float32)],
        compiler_params=pltpu.CompilerParams(
            dimension_semantics=("arbitrary",), vmem_limit_bytes=VMEM_LIMIT_BYTES),
        name="mix_cross",
    )(x2, u, u, attn, w_pool_bd, pool_scale, w_out, g_cross, wq, kmem, vmem, wom, g_ffn, w_router, su)


def _dispatch_kernel(pos_ref, last_ref, n_last_ref, na_ref, x_ref, xs_ref, stage, zeros, sems, fill_sem,
                     *, tm, n_tiles, n_tiles_e):
    i = pl.program_id(0)
    slot = i % 2
    base = i * tm
    tm_e = zeros.shape[0]

    def fill_copy(j):
        return pltpu.make_async_copy(zeros, xs_ref.at[pl.ds(pl.multiple_of(j * tm_e, tm_e), tm_e), :], fill_sem)

    def for_range(lo, hi, fn):
        def body(k, carry):
            fn(k)
            return carry
        lax.fori_loop(lo, hi, body, 0)

    @pl.when(i == 0)
    def _():
        zeros[...] = jnp.zeros_like(zeros)
        for_range(0, n_last_ref[0], lambda k: fill_copy(last_ref[k]).start())
        for_range(na_ref[0], n_tiles_e, lambda j: fill_copy(j).start())
        for_range(0, n_last_ref[0], lambda k: fill_copy(0).wait())
        for_range(na_ref[0], n_tiles_e, lambda j: fill_copy(0).wait())

    def row_copy(slot, r, p):
        return pltpu.make_async_copy(stage.at[slot, pl.ds(r, 1), :], xs_ref.at[pl.ds(p, 1), :], sems.at[slot])

    def wait_slot(slot):
        pltpu.make_async_copy(stage.at[slot], xs_ref.at[pl.ds(0, tm), :], sems.at[slot]).wait()

    stage[slot] = x_ref[...]
    for r in range(tm):
        row_copy(slot, r, pos_ref[base + r]).start(priority=r % 2)

    @pl.when(i > 0)
    def _():
        wait_slot(1 - slot)

    @pl.when(i == n_tiles - 1)
    def _():
        wait_slot(slot)


def _dispatch(pos, last_tiles, n_last, n_active, xrow, *, tm, tm_e, n_tiles_e):
    n, width = xrow.shape
    return pl.pallas_call(
        functools.partial(_dispatch_kernel, tm=tm, n_tiles=n // tm, n_tiles_e=n_tiles_e),
        grid_spec=pltpu.PrefetchScalarGridSpec(
            num_scalar_prefetch=4,
            grid=(n // tm,),
            in_specs=[pl.BlockSpec((tm, width), lambda i, *_: (i, 0))],
            out_specs=pl.BlockSpec(memory_space=pl.ANY),
            scratch_shapes=[pltpu.VMEM((2, tm, width), xrow.dtype), pltpu.VMEM((tm_e, width), xrow.dtype),
                            pltpu.SemaphoreType.DMA((2,)), pltpu.SemaphoreType.DMA],
        ),
        out_shape=jax.ShapeDtypeStruct((n_tiles_e * tm_e, width), xrow.dtype),
        compiler_params=pltpu.CompilerParams(
            dimension_semantics=("arbitrary",), vmem_limit_bytes=VMEM_LIMIT_BYTES),
        name="moe_dispatch",
    )(pos, last_tiles, n_last, n_active, xrow)


def _expert_kernel(ea_ref, eb_ref, na_ref, xs_ref, wgua_ref, wda_ref, wgub_ref, wdb_ref, y_ref, *, d):
    del ea_ref, eb_ref
    active = pl.program_id(0) < na_ref[0]

    @pl.when(jnp.logical_not(active))
    def _():
        y_ref[...] = jnp.zeros_like(y_ref)

    @pl.when(active)
    def _():
        x = xs_ref[:, :d].astype(jnp.bfloat16)

        def mlp(wgu_ref, wd_ref):
            gu = _dot(x, wgu_ref[...])
            ff = gu.shape[1] // 2
            gate, up = gu[:, :ff], gu[:, ff:]
            hmid = (gate * (1.0 / (1.0 + jnp.exp(-gate)))) * up
            return _dot(hmid.astype(jnp.bfloat16), wd_ref[...])

        y_ref[...] = (xs_ref[:, d:d + 1] * mlp(wgua_ref, wda_ref)
                      + xs_ref[:, d + 1:d + 2] * mlp(wgub_ref, wdb_ref))


def _experts(tile_ea, tile_eb, n_active, xs, wgu, wd, *, tm):
    n_pad, width = xs.shape
    d, ff = wd.shape[2], wd.shape[1]
    row_map = lambda j, ea, eb, na: (jnp.minimum(j, na[0] - 1), 0)
    a_map = lambda j, ea, eb, na: (ea[j], 0, 0)
    b_map = lambda j, ea, eb, na: (eb[j], 0, 0)
    return pl.pallas_call(
        functools.partial(_expert_kernel, d=d),
        grid_spec=pltpu.PrefetchScalarGridSpec(
            num_scalar_prefetch=3,
            grid=(n_pad // tm,),
            in_specs=[
                pl.BlockSpec((tm, width), row_map),
                pl.BlockSpec((None, d, 2 * ff), a_map),
                pl.BlockSpec((None, ff, d), a_map),
                pl.BlockSpec((None, d, 2 * ff), b_map),
                pl.BlockSpec((None, ff, d), b_map),
            ],
            out_specs=pl.BlockSpec((tm, d), lambda j, ea, eb, na: (j, 0)),
        ),
        out_shape=jax.ShapeDtypeStruct((n_pad, d), jnp.float32),
        compiler_params=pltpu.CompilerParams(
            dimension_semantics=("arbitrary",), vmem_limit_bytes=VMEM_LIMIT_BYTES),
        name="moe_experts",
    )(tile_ea, tile_eb, n_active, xs, wgu, wd, wgu, wd)


def _combine_kernel(pos_ref, h_ref, y_ref, g_ref, o_ref, ybuf, sems, *, tm, n_tiles):
    i = pl.program_id(0)

    def row_copy(slot, r, p):
        return pltpu.make_async_copy(y_ref.at[pl.ds(p, 1), :], ybuf.at[slot, pl.ds(r, 1), :], sems.at[slot])

    @pl.when(i < n_tiles)
    def _():
        slot = i % 2
        for r in range(tm):
            row_copy(slot, r, pos_ref[i * tm + r]).start(priority=r % 2)

    @pl.when(i > 0)
    def _():
        slot = (i - 1) % 2
        pltpu.make_async_copy(y_ref.at[pl.ds(0, tm), :], ybuf.at[slot], sems.at[slot]).wait()
        o_ref[...] = _rms(h_ref[...] + ybuf[slot], g_ref[...])


def _combine(pos, h, y_sorted, g_final, *, tm):
    n, d = h.shape
    n_tiles = n // tm
    prev_tile = lambda i, pos: (jnp.maximum(i - 1, 0), 0)
    return pl.pallas_call(
        functools.partial(_combine_kernel, tm=tm, n_tiles=n_tiles),
        grid_spec=pltpu.PrefetchScalarGridSpec(
            num_scalar_prefetch=1,
            grid=(n_tiles + 1,),
            in_specs=[
                pl.BlockSpec((tm, d), prev_tile),
                pl.BlockSpec(memory_space=pl.ANY),
                pl.BlockSpec((1, d), lambda i, pos: (0, 0)),
            ],
            out_specs=pl.BlockSpec((tm, d), prev_tile),
            scratch_shapes=[pltpu.VMEM((2, tm, d), jnp.float32), pltpu.SemaphoreType.DMA((2,))],
        ),
        out_shape=jax.ShapeDtypeStruct((n, d), jnp.float32),
        compiler_params=pltpu.CompilerParams(
            dimension_semantics=("arbitrary",), vmem_limit_bytes=VMEM_LIMIT_BYTES),
        name="moe_combine",
    )(pos, h, y_sorted, g_final)


def _tile(n, pref):
    t = min(n, pref)
    assert n % t == 0, (n, t)
    return t


def _sorted_layout(bucket, rank, counts, *, tm_e, n_tiles_e):
    seg_tiles = (counts + tm_e - 1) // tm_e
    seg_end = jnp.cumsum(seg_tiles)
    seg_start = seg_end - seg_tiles
    pos = (seg_start * tm_e)[bucket] + rank
    nonempty = seg_tiles > 0
    last_tiles = jnp.sort(jnp.where(nonempty, seg_end - 1, n_tiles_e))
    n_active = seg_end[-1]
    tile = jnp.minimum(jnp.arange(n_tiles_e, dtype=jnp.int32), n_active - 1)
    tile_bucket = jnp.sum((tile[:, None] >= seg_end[None, :]).astype(jnp.int32), axis=1)
    group, pair = tile_bucket // len(PAIRS), tile_bucket % len(PAIRS)
    pair_a = jnp.array([p[0] for p in PAIRS], jnp.int32)
    pair_b = jnp.array([p[1] for p in PAIRS], jnp.int32)
    tile_ea = group * EXPERTS_PER_GROUP + pair_a[pair]
    tile_eb = group * EXPERTS_PER_GROUP + pair_b[pair]
    i32 = lambda a: a.astype(jnp.int32)
    return i32(pos), i32(last_tiles), i32(jnp.sum(nonempty)), i32(tile_ea), i32(tile_eb), i32(n_active)


def kernel(x, mem, norm_mix, w_in, w_pool, pool_scale, w_out, norm_cross, norm_mem, w_q_mem, w_k_mem,
           w_v_mem, w_o_mem, norm_ffn, w_group, w_expert, w_gate, w_up, w_down, norm_final):
    b, s, d = x.shape
    mem_len = mem.shape[1]
    depth = norm_mix.shape[0]
    pool_width = pool_scale.shape[1]
    attn_width = w_out.shape[1] - pool_width
    bf = jnp.bfloat16
    n = b * s

    assert depth == 1, "single-layer problem: the final RMSNorm is fused into the combine kernel"
    l = 0
    h = x.reshape(n, d)
    u, qkv = _in_proj(h, norm_mix[l][None], w_in[l].astype(bf),
                      pool_width=pool_width, attn_width=attn_width, tm=_tile(n, 1024))
    tq = _tile(s, 2 * MXU_DIM_V7X)
    sub = tq // 2
    attn = _sb_attention(qkv.reshape(-1, b, s, LANES_V7X), tq=tq, sub=sub, win=2 * sub)
    kmem, vmem = _mem_kv(mem.reshape(b * mem_len, d), norm_mem[l][None],
                         w_k_mem[l].astype(bf), w_v_mem[l].astype(bf), tm=mem_len)

    w_router = jnp.concatenate([w_group[l], w_expert[l]], axis=1)
    w_router = jnp.pad(w_router, ((0, 0), (0, ROUTER_LANES - w_router.shape[1])))
    wr_hi = w_router.astype(bf)
    wr_lo = (w_router - wr_hi.astype(jnp.float32)).astype(bf)
    w_pool_bd = jax.scipy.linalg.block_diag(*w_pool[l]).astype(bf)
    h, xrow, bucket, rank, counts = _mix_cross(
        h, u, attn.reshape(-1, n, LANES_V7X), w_pool_bd, pool_scale[l][None], w_out[l].astype(bf),
        norm_cross[l][None], w_q_mem[l].astype(bf), kmem, vmem, w_o_mem[l].astype(bf),
        norm_ffn[l][None], jnp.concatenate([wr_hi, wr_lo], axis=1), tm=_tile(s, 512), seq=s, mem_len=mem_len)

    tm_e = _tile(n, MXU_DIM_V7X)
    n_tiles_e = -(-(n + N_BUCKETS * (tm_e - 1)) // tm_e)
    pos, last_tiles, n_last, tile_ea, tile_eb, n_active = _sorted_layout(
        bucket.reshape(n), rank.reshape(n), counts[:N_BUCKETS, 0].astype(jnp.int32),
        tm_e=tm_e, n_tiles_e=n_tiles_e)
    xs = _dispatch(pos, last_tiles, n_last[None], n_active[None], xrow, tm=_tile(n, 512), tm_e=tm_e,
                   n_tiles_e=n_tiles_e)
    w_gate_up = jnp.concatenate([w_gate[l], w_up[l]], axis=-1).astype(bf)
    y_sorted = _experts(tile_ea, tile_eb, n_active[None], xs, w_gate_up, w_down[l].astype(bf), tm=tm_e)
    out = _combine(pos, h, y_sorted, norm_final[None], tm=_tile(n, 512))
    return out.reshape(b, s, d)
```

```python
import functools
import math

import jax
import jax.numpy as jnp
from jax import lax
from jax.experimental import pallas as pl
from jax.experimental.pallas import tpu as pltpu

RMS_EPS = 1e-6
POOL_WINDOWS = (2, 4, 8, 16)
SB_HEAD_DIM = 64
MEM_HEADS = 4
N_GROUPS = 4
EXPERTS_PER_GROUP = 4
N_EXPERTS = N_GROUPS * EXPERTS_PER_GROUP

LANES_V7X = 128
MXU_DIM_V7X = 256
VMEM_LIMIT_BYTES = 56 * 1024 * 1024

LOG2E = 1.4426950408889634
UNDERFLOW_EXPONENT = 104.0
MASKED_LOGIT = -1e30
POOL_HALO = 16

PAIRS = ((0, 1), (0, 2), (1, 2), (1, 3), (2, 3), (0, 3))
assert sorted(PAIRS) == [(a, b) for a in range(EXPERTS_PER_GROUP) for b in range(a + 1, EXPERTS_PER_GROUP)]
N_BUCKETS = N_GROUPS * len(PAIRS)
ROUTER_ROWS = 32
ROUTER_LANES = LANES_V7X
PAYLOAD_LANES = LANES_V7X
assert N_GROUPS + N_EXPERTS <= ROUTER_ROWS and N_BUCKETS <= ROUTER_ROWS


def _rms(x, gain):
    ms = jnp.mean(x * x, axis=-1, keepdims=True)
    return x * lax.rsqrt(ms + RMS_EPS) * gain


def _dot(a, b):
    return jnp.dot(a, b, preferred_element_type=jnp.float32)


def _dot_nt(a, b):
    return lax.dot_general(a, b, (((1,), (1,)), ((), ())), preferred_element_type=jnp.float32)


def _in_proj_kernel(x_ref, g_ref, w_ref, u_ref, qkv_ref, *, pool_width, attn_width, q_scale):
    xn = _rms(x_ref[...], g_ref[...]).astype(jnp.bfloat16)
    proj = _dot(xn, w_ref[...])
    u_ref[...] = proj[:, :pool_width]
    q = proj[:, pool_width:pool_width + attn_width] * q_scale
    qkv = jnp.concatenate([q, proj[:, pool_width + attn_width:]], axis=1).astype(jnp.bfloat16)
    for j in range(qkv_ref.shape[0]):
        qkv_ref[j] = qkv[:, j * LANES_V7X:(j + 1) * LANES_V7X]


def _in_proj(x2, gain, w_in_bf16, *, pool_width, attn_width, tm):
    n, d = x2.shape
    in_width = w_in_bf16.shape[1]
    n_slabs = 3 * attn_width // LANES_V7X
    kern = functools.partial(_in_proj_kernel, pool_width=pool_width, attn_width=attn_width,
                             q_scale=1.0 / math.sqrt(SB_HEAD_DIM))
    return pl.pallas_call(
        kern,
        grid=(n // tm,),
        in_specs=[
            pl.BlockSpec((tm, d), lambda i: (i, 0)),
            pl.BlockSpec((1, d), lambda i: (0, 0)),
            pl.BlockSpec((d, in_width), lambda i: (0, 0)),
        ],
        out_specs=[
            pl.BlockSpec((tm, pool_width), lambda i: (i, 0)),
            pl.BlockSpec((n_slabs, tm, LANES_V7X), lambda i: (0, i, 0)),
        ],
        out_shape=[
            jax.ShapeDtypeStruct((n, pool_width), jnp.float32),
            jax.ShapeDtypeStruct((n_slabs, n, LANES_V7X), jnp.bfloat16),
        ],
        compiler_params=pltpu.CompilerParams(
            dimension_semantics=("parallel",), vmem_limit_bytes=VMEM_LIMIT_BYTES),
        name="in_proj",
    )(x2, gain, w_in_bf16)


def _sb_tile(qh, kb, vb, tri, c, tail_mask):
    z = _dot_nt(qh, kb)
    if tail_mask is not None:
        m = tail_mask.shape[1]
        tail = jnp.where(tail_mask, z[:, -m:], MASKED_LOGIT)
        z = tail if m == z.shape[1] else jnp.concatenate([z[:, :-m], tail], axis=1)
    sp = (jnp.maximum(z, 0.0) + jnp.log(1.0 + jnp.exp2(jnp.abs(z) * (-LOG2E)))).astype(jnp.bfloat16)
    n = tri.shape[0]
    chunks, total = [], None
    for k0 in range(z.shape[1] - n, -1, -n):
        cum_k = _dot(sp[:, k0:k0 + n], tri)
        if total is not None:
            cum_k = cum_k + total
        total = cum_k[:, :1]
        chunks.insert(0, cum_k)
    cum = chunks[0] if len(chunks) == 1 else jnp.concatenate(chunks, axis=1)
    w = jnp.exp(z - cum) if c is None else jnp.exp(jnp.minimum(z - cum, 0.0) - c)
    return (total if c is None else c + total), _dot(w.astype(jnp.bfloat16), vb)


def _sb_attn_kernel(q_ref, k_ref, v_ref, tri_ref, o_ref, *, tq, sub, win):
    qi = pl.program_id(2)
    n_sub = tq // sub
    lane = lax.broadcasted_iota(jnp.int32, (tq, LANES_V7X), 1)
    head0 = lane < SB_HEAD_DIM
    q2 = q_ref[...]
    zero = jnp.zeros_like(q2)
    q_heads = (jnp.where(head0, q2, zero), jnp.where(head0, zero, q2))
    q_st = jnp.concatenate(q_heads, axis=0)

    def kv_block(j):
        start = pl.multiple_of(j * tq, tq)
        return k_ref[pl.ds(start, tq), :], v_ref[pl.ds(start, tq), :]

    def stacked_iota(rows, cols):
        row = lax.broadcasted_iota(jnp.int32, (2 * rows, cols), 0)
        col = lax.broadcasted_iota(jnp.int32, (2 * rows, cols), 1)
        return jnp.where(row >= rows, row - rows, row), col

    def first_block():
        row, col = stacked_iota(tq, tq)
        kb, vb = kv_block(0)
        return _sb_tile(q_st, kb, vb, tri_ref[...], None, col < row)

    def windows():
        row, col = stacked_iota(sub, sub)
        causal = col < row
        tri = tri_ref[...]
        parts = []
        for hf in range(n_sub):
            start = pl.multiple_of((qi * n_sub + hf + 1) * sub - win, sub)
            kw = k_ref[pl.ds(start, win), :]
            vw = v_ref[pl.ds(start, win), :]
            q_sub = jnp.concatenate([qh[hf * sub:(hf + 1) * sub] for qh in q_heads], axis=0)
            parts.append(_sb_tile(q_sub, kw, vw, tri, None, causal))
        return tuple(jnp.concatenate([p[i][h * sub:(h + 1) * sub] for h in range(2) for p in parts], axis=0)
                     for i in range(2))

    carry = lax.cond(qi == 0, first_block, windows)

    def live(carry):
        return jnp.min(carry[0], axis=0, keepdims=True)[0, 0] < UNDERFLOW_EXPONENT

    def add_block(carry, j, tail_mask):
        kb, vb = kv_block(j)
        c, acc = _sb_tile(q_st, kb, vb, tri_ref[...], carry[0], tail_mask)
        return c, carry[1] + acc

    def partial_block(back):
        def fn(carry):
            row, col = stacked_iota(tq, tq)
            fresh = col < (row // sub + 1) * sub - win + back * tq
            return add_block(carry, qi - back, fresh)
        return fn

    for back in ((0, 1) if win < tq else (1,)):
        carry = lax.cond(jnp.logical_and(qi > 0, live(carry)), partial_block(back), lambda cr: cr, carry)

    def cond(state):
        step, alive, _ = state
        return jnp.logical_and(step < qi - 1, alive > 0)

    def body(state):
        step, _, carry = state
        carry = add_block(carry, qi - 2 - step, None)
        return step + 1, live(carry).astype(jnp.int32), carry

    _, _, carry = lax.while_loop(cond, body, (jnp.int32(0), live(carry).astype(jnp.int32), carry))
    acc = carry[1]
    o_ref[...] = jnp.where(head0, acc[:tq], acc[tq:]).astype(o_ref.dtype)


def _sb_attention(qkv, *, tq, sub, win):
    n_slabs, b, s, _ = qkv.shape
    assert tq % sub == 0 and win % sub == 0 and sub <= win <= tq + sub, (tq, sub, win)
    n_pairs = n_slabs // 3
    n = math.gcd(math.gcd(tq, win), MXU_DIM_V7X)
    tri = (lax.broadcasted_iota(jnp.int32, (n, n), 0)
           >= lax.broadcasted_iota(jnp.int32, (n, n), 1)).astype(jnp.bfloat16)

    kern = functools.partial(_sb_attn_kernel, tq=tq, sub=sub, win=win)
    const2 = lambda bi, hp, qi: (0, 0)
    return pl.pallas_call(
        kern,
        grid=(b, n_pairs, s // tq),
        in_specs=[
            pl.BlockSpec((None, None, tq, LANES_V7X), lambda bi, hp, qi: (hp, bi, qi, 0)),
            pl.BlockSpec((None, None, s, LANES_V7X), lambda bi, hp, qi: (n_pairs + hp, bi, 0, 0)),
            pl.BlockSpec((None, None, s, LANES_V7X), lambda bi, hp, qi: (2 * n_pairs + hp, bi, 0, 0)),
            pl.BlockSpec((n, n), const2),
        ],
        out_specs=pl.BlockSpec((None, None, tq, LANES_V7X), lambda bi, hp, qi: (hp, bi, qi, 0)),
        out_shape=jax.ShapeDtypeStruct((n_pairs, b, s, LANES_V7X), jnp.bfloat16),
        compiler_params=pltpu.CompilerParams(
            dimension_semantics=("parallel", "parallel", "parallel"),
            vmem_limit_bytes=VMEM_LIMIT_BYTES),
        name="sb_attention",
    )(qkv, qkv, qkv, tri)


def _mem_kv_kernel(m_ref, g_ref, wk_ref, wv_ref, k_ref, v_ref):
    mn = _rms(m_ref[...], g_ref[...]).astype(jnp.bfloat16)
    k_ref[...] = _dot(mn, wk_ref[...]).astype(jnp.bfloat16)
    v_ref[...] = _dot(mn, wv_ref[...]).astype(jnp.bfloat16)


def _mem_kv(mem2, gain, wk, wv, *, tm):
    n, d = mem2.shape
    row = pl.BlockSpec((tm, d), lambda i: (i, 0))
    full = pl.BlockSpec((d, d), lambda i: (0, 0))
    return pl.pallas_call(
        _mem_kv_kernel,
        grid=(n // tm,),
        in_specs=[row, pl.BlockSpec((1, d), lambda i: (0, 0)), full, full],
        out_specs=[row, row],
        out_shape=[jax.ShapeDtypeStruct((n, d), jnp.bfloat16)] * 2,
        compiler_params=pltpu.CompilerParams(
            dimension_semantics=("parallel",), vmem_limit_bytes=VMEM_LIMIT_BYTES),
        name="mem_kv",
    )(mem2, gain, wk, wv)


def _route_t(logits):
    neg = jnp.float32(-jnp.inf)
    big = jnp.float32(ROUTER_ROWS)
    row = lax.broadcasted_iota(jnp.int32, logits.shape, 0).astype(jnp.float32)

    def col_max(mask):
        return jnp.max(jnp.where(mask, logits, neg), axis=0, keepdims=True)

    def first_argmax(mask, mx):
        return jnp.min(jnp.where(mask & (logits == mx), row, big), axis=0, keepdims=True)

    gmask = row < N_GROUPS
    gmax = col_max(gmask)
    gsum = jnp.sum(jnp.where(gmask, jnp.exp(logits - gmax), 0.0), axis=0, keepdims=True)
    g_gate = 1.0 / gsum
    g_idx = first_argmax(gmask, gmax)

    lo = N_GROUPS + EXPERTS_PER_GROUP * g_idx
    emask = (row >= lo) & (row < lo + EXPERTS_PER_GROUP)
    m1 = col_max(emask)
    i1 = first_argmax(emask, m1)
    mask2 = emask & (row != i1)
    m2 = col_max(mask2)
    i2 = first_argmax(mask2, m2)
    esum = jnp.sum(jnp.where(emask, jnp.exp(logits - m1), 0.0), axis=0, keepdims=True)
    p1 = 1.0 / esum
    p2 = jnp.exp(m2 - m1) / esum
    tot = p1 + p2
    w1 = g_gate * (p1 / tot)
    w2 = g_gate * (p2 / tot)

    first = i1 < i2
    la = jnp.where(first, i1, i2) - lo
    lb = jnp.where(first, i2, i1) - lo
    pair = sum(jnp.where((la == a) & (lb == b), float(p), 0.0) for p, (a, b) in enumerate(PAIRS))
    bucket = g_idx * len(PAIRS) + pair
    return bucket, jnp.where(first, w1, w2), jnp.where(first, w2, w1)


def _mix_cross_kernel(x_ref, u_ref, halo_ref, a_ref, wpbd_ref, ps_ref, wo_ref, gc_ref,
                      wq_ref, km_ref, vm_ref, wom_ref, gf_ref, wr_ref, su_ref,
                      h_ref, xrow_ref, bucket_ref, rank_ref, cnt_ref, run_ref, *, tm, tiles_per_seq):
    i = pl.program_id(0)
    pool_width = u_ref.shape[1]
    gw = pool_width // len(POOL_WINDOWS)
    tile_in_seq = i % tiles_per_seq
    first = tile_in_seq == 0
    pos = tile_in_seq * tm + lax.broadcasted_iota(jnp.int32, (tm, 1), 0)

    halo = jnp.where(first, 0.0, halo_ref[...])
    u = u_ref[...]
    pooled = []
    for g, w in enumerate(POOL_WINDOWS):
        ug = u[:, g * gw:(g + 1) * gw]
        ext = jnp.concatenate([halo[:, g * gw:(g + 1) * gw], ug], axis=0)
        shift = 1
        while shift < w:
            ext = ext + pltpu.roll(ext, shift, 0)
            shift *= 2
        win = ext[POOL_HALO:, :]
        inv_count = 1.0 / jnp.minimum(pos + 1, w).astype(jnp.float32)
        pooled.append((win * inv_count - ug).astype(jnp.bfloat16))

    pool_out = _dot(jnp.concatenate(pooled, axis=1), wpbd_ref[...]) * ps_ref[...]
    mixed = jnp.concatenate([pool_out.astype(jnp.bfloat16)] + [a_ref[j] for j in range(a_ref.shape[0])], axis=1)
    h1 = x_ref[...] + _dot(mixed, wo_ref[...])

    hn = _rms(h1, gc_ref[...]).astype(jnp.bfloat16)
    d = h1.shape[1]
    hd = d // MEM_HEADS
    q = (_dot(hn, wq_ref[...]) * (1.0 / math.sqrt(hd))).astype(jnp.bfloat16)
    outs = []
    for hh in range(MEM_HEADS):
        sl = slice(hh * hd, (hh + 1) * hd)
        s = _dot_nt(q[:, sl], km_ref[:, sl])
        e = jnp.exp(s - jnp.max(s, axis=-1, keepdims=True))
        p = e * (1.0 / jnp.sum(e, axis=-1, keepdims=True))
        outs.append(_dot(p.astype(jnp.bfloat16), vm_ref[:, sl]))
    o = jnp.concatenate(outs, axis=-1).astype(jnp.bfloat16)
    h2 = h1 + _dot(o, wom_ref[...])
    h_ref[...] = h2

    xn = _rms(h2, gf_ref[...])
    xrow_ref[:, :d] = xn
    x_hi = xn.astype(jnp.bfloat16)
    x_lo = (xn - x_hi.astype(jnp.float32)).astype(jnp.bfloat16)
    both = _dot(x_hi, wr_ref[...])
    logits_tm = both[:, :ROUTER_LANES] + both[:, ROUTER_LANES:] + _dot(x_lo, wr_ref[:, :ROUTER_LANES])
    logits = logits_tm.T[:ROUTER_ROWS]
    bucket, w_a, w_b = _route_t(logits)

    @pl.when(i == 0)
    def _():
        run_ref[...] = jnp.zeros_like(run_ref)

    brow = lax.broadcasted_iota(jnp.int32, (ROUTER_ROWS, tm), 0).astype(jnp.float32)
    onehot = (brow == bucket).astype(jnp.float32)
    before = _dot(onehot.astype(jnp.bfloat16), su_ref[...])
    run = run_ref[...]
    rank = jnp.sum(onehot * (before + run[:, :1]), axis=0, keepdims=True)
    run = run + jnp.sum(onehot, axis=1, keepdims=True)
    run_ref[...] = run
    cnt_ref[...] = run
    bucket_ref[...] = bucket.astype(jnp.int32)
    rank_ref[...] = rank.astype(jnp.int32)

    prow = lax.broadcasted_iota(jnp.int32, (PAYLOAD_LANES, tm), 0)
    payload_t = jnp.where(prow == 0, w_a, jnp.where(prow == 1, w_b, 0.0))
    xrow_ref[:, d:] = payload_t.T


def _mix_cross(x2, u, attn, w_pool_bd, pool_scale, w_out, g_cross, wq, kmem, vmem, wom, g_ffn, w_router,
               *, tm, seq, mem_len):
    n, d = x2.shape
    pool_width = u.shape[1]
    tiles_per_seq = seq // tm
    n_tiles = n // tm
    halo_blocks = tm // POOL_HALO
    su = (lax.broadcasted_iota(jnp.int32, (tm, tm), 0)
          < lax.broadcasted_iota(jnp.int32, (tm, tm), 1)).astype(jnp.bfloat16)
    kern = functools.partial(_mix_cross_kernel, tm=tm, tiles_per_seq=tiles_per_seq)
    const2 = lambda i: (0, 0)
    mem_map = lambda i: (i // tiles_per_seq, 0)
    return pl.pallas_call(
        kern,
        grid=(n_tiles,),
        in_specs=[
            pl.BlockSpec((tm, d), lambda i: (i, 0)),
            pl.BlockSpec((tm, pool_width), lambda i: (i, 0)),
            pl.BlockSpec((POOL_HALO, pool_width), lambda i: (jnp.maximum(i * halo_blocks - 1, 0), 0)),
            pl.BlockSpec((attn.shape[0], tm, LANES_V7X), lambda i: (0, i, 0)),
            pl.BlockSpec(w_pool_bd.shape, const2),
            pl.BlockSpec((1, pool_width), const2),
            pl.BlockSpec(w_out.shape, const2),
            pl.BlockSpec((1, d), const2),
            pl.BlockSpec((d, d), const2),
            pl.BlockSpec((mem_len, d), mem_map),
            pl.BlockSpec((mem_len, d), mem_map),
            pl.BlockSpec((d, d), const2),
            pl.BlockSpec((1, d), const2),
            pl.BlockSpec((d, 2 * ROUTER_LANES), const2),
            pl.BlockSpec((tm, tm), const2),
        ],
        out_specs=[
            pl.BlockSpec((tm, d), lambda i: (i, 0)),
            pl.BlockSpec((tm, d + PAYLOAD_LANES), lambda i: (i, 0)),
            pl.BlockSpec((None, 1, tm), lambda i: (i, 0, 0)),
            pl.BlockSpec((None, 1, tm), lambda i: (i, 0, 0)),
            pl.BlockSpec((ROUTER_ROWS, LANES_V7X), const2),
        ],
        out_shape=[
            jax.ShapeDtypeStruct((n, d), jnp.float32),
            jax.ShapeDtypeStruct((n, d + PAYLOAD_LANES), jnp.float32),
            jax.ShapeDtypeStruct((n_tiles, 1, tm), jnp.int32),
            jax.ShapeDtypeStruct((n_tiles, 1, tm), jnp.int32),
            jax.ShapeDtypeStruct((ROUTER_ROWS, LANES_V7X), jnp.float32),
        ],
        scratch_shapes=[pltpu.VMEM((ROUTER_ROWS, LANES_V7X), jnp.float32)],
        compiler_params=pltpu.CompilerParams(
            dimension_semantics=("arbitrary",), vmem_limit_bytes=VMEM_LIMIT_BYTES),
        name="mix_cross",
    )(x2, u, u, attn, w_pool_bd, pool_scale, w_out, g_cross, wq, kmem, vmem, wom, g_ffn, w_router, su)


def _dispatch_kernel(pos_ref, last_ref, n_last_ref, na_ref, x_ref, xs_ref, stage, zeros, sems, fill_sem,
                     *, tm, n_tiles, n_tiles_e):
    i = pl.program_id(0)
    slot = i % 2
    base = i * tm
    tm_e = zeros.shape[0]

    def fill_copy(j):
        return pltpu.make_async_copy(zeros, xs_ref.at[pl.ds(pl.multiple_of(j * tm_e, tm_e), tm_e), :], fill_sem)

    def for_range(lo, hi, fn):
        def body(k, carry):
            fn(k)
            return carry
        lax.fori_loop(lo, hi, body, 0)

    @pl.when(i == 0)
    def _():
        zeros[...] = jnp.zeros_like(zeros)
        for_range(0, n_last_ref[0], lambda k: fill_copy(last_ref[k]).start())
        for_range(na_ref[0], n_tiles_e, lambda j: fill_copy(j).start())
        for_range(0, n_last_ref[0], lambda k: fill_copy(0).wait())
        for_range(na_ref[0], n_tiles_e, lambda j: fill_copy(0).wait())

    def row_copy(slot, r, p):
        return pltpu.make_async_copy(stage.at[slot, pl.ds(r, 1), :], xs_ref.at[pl.ds(p, 1), :], sems.at[slot])

    def wait_slot(slot):
        pltpu.make_async_copy(stage.at[slot], xs_ref.at[pl.ds(0, tm), :], sems.at[slot]).wait()

    stage[slot] = x_ref[...]
    for r in range(tm):
        row_copy(slot, r, pos_ref[base + r]).start(priority=r % 2)

    @pl.when(i > 0)
    def _():
        wait_slot(1 - slot)

    @pl.when(i == n_tiles - 1)
    def _():
        wait_slot(slot)


def _dispatch(pos, last_tiles, n_last, n_active, xrow, *, tm, tm_e, n_tiles_e):
    n, width = xrow.shape
    return pl.pallas_call(
        functools.partial(_dispatch_kernel, tm=tm, n_tiles=n // tm, n_tiles_e=n_tiles_e),
        grid_spec=pltpu.PrefetchScalarGridSpec(
            num_scalar_prefetch=4,
            grid=(n // tm,),
            in_specs=[pl.BlockSpec((tm, width), lambda i, *_: (i, 0))],
            out_specs=pl.BlockSpec(memory_space=pl.ANY),
            scratch_shapes=[pltpu.VMEM((2, tm, width), xrow.dtype), pltpu.VMEM((tm_e, width), xrow.dtype),
                            pltpu.SemaphoreType.DMA((2,)), pltpu.SemaphoreType.DMA],
        ),
        out_shape=jax.ShapeDtypeStruct((n_tiles_e * tm_e, width), xrow.dtype),
        compiler_params=pltpu.CompilerParams(
            dimension_semantics=("arbitrary",), vmem_limit_bytes=VMEM_LIMIT_BYTES),
        name="moe_dispatch",
    )(pos, last_tiles, n_last, n_active, xrow)


def _expert_kernel(ea_ref, eb_ref, na_ref, xs_ref, wgua_ref, wda_ref, wgub_ref, wdb_ref, y_ref, *, d):
    del ea_ref, eb_ref
    active = pl.program_id(0) < na_ref[0]

    @pl.when(jnp.logical_not(active))
    def _():
        y_ref[...] = jnp.zeros_like(y_ref)

    @pl.when(active)
    def _():
        x = xs_ref[:, :d].astype(jnp.bfloat16)

        def mlp(wgu_ref, wd_ref):
            gu = _dot(x, wgu_ref[...])
            ff = gu.shape[1] // 2
            gate, up = gu[:, :ff], gu[:, ff:]
            hmid = (gate * (1.0 / (1.0 + jnp.exp(-gate)))) * up
            return _dot(hmid.astype(jnp.bfloat16), wd_ref[...])

        y_ref[...] = (xs_ref[:, d:d + 1] * mlp(wgua_ref, wda_ref)
                      + xs_ref[:, d + 1:d + 2] * mlp(wgub_ref, wdb_ref))


def _experts(tile_ea, tile_eb, n_active, xs, wgu, wd, *, tm):
    n_pad, width = xs.shape
    d, ff = wd.shape[2], wd.shape[1]
    row_map = lambda j, ea, eb, na: (jnp.minimum(j, na[0] - 1), 0)
    a_map = lambda j, ea, eb, na: (ea[j], 0, 0)
    b_map = lambda j, ea, eb, na: (eb[j], 0, 0)
    return pl.pallas_call(
        functools.partial(_expert_kernel, d=d),
        grid_spec=pltpu.PrefetchScalarGridSpec(
            num_scalar_prefetch=3,
            grid=(n_pad // tm,),
            in_specs=[
                pl.BlockSpec((tm, width), row_map),
                pl.BlockSpec((None, d, 2 * ff), a_map),
                pl.BlockSpec((None, ff, d), a_map),
                pl.BlockSpec((None, d, 2 * ff), b_map),
                pl.BlockSpec((None, ff, d), b_map),
            ],
            out_specs=pl.BlockSpec((tm, d), lambda j, ea, eb, na: (j, 0)),
        ),
        out_shape=jax.ShapeDtypeStruct((n_pad, d), jnp.float32),
        compiler_params=pltpu.CompilerParams(
            dimension_semantics=("arbitrary",), vmem_limit_bytes=VMEM_LIMIT_BYTES),
        name="moe_experts",
    )(tile_ea, tile_eb, n_active, xs, wgu, wd, wgu, wd)


def _combine_kernel(pos_ref, h_ref, y_ref, g_ref, o_ref, ybuf, sems, *, tm, n_tiles):
    i = pl.program_id(0)

    def row_copy(slot, r, p):
        return pltpu.make_async_copy(y_ref.at[pl.ds(p, 1), :], ybuf.at[slot, pl.ds(r, 1), :], sems.at[slot])

    @pl.when(i < n_tiles)
    def _():
        slot = i % 2
        for r in range(tm):
            row_copy(slot, r, pos_ref[i * tm + r]).start(priority=r % 2)

    @pl.when(i > 0)
    def _():
        slot = (i - 1) % 2
        pltpu.make_async_copy(y_ref.at[pl.ds(0, tm), :], ybuf.at[slot], sems.at[slot]).wait()
        o_ref[...] = _rms(h_ref[...] + ybuf[slot], g_ref[...])


def _combine(pos, h, y_sorted, g_final, *, tm):
    n, d = h.shape
    n_tiles = n // tm
    prev_tile = lambda i, pos: (jnp.maximum(i - 1, 0), 0)
    return pl.pallas_call(
        functools.partial(_combine_kernel, tm=tm, n_tiles=n_tiles),
        grid_spec=pltpu.PrefetchScalarGridSpec(
            num_scalar_prefetch=1,
            grid=(n_tiles + 1,),
            in_specs=[
                pl.BlockSpec((tm, d), prev_tile),
                pl.BlockSpec(memory_space=pl.ANY),
                pl.BlockSpec((1, d), lambda i, pos: (0, 0)),
            ],
            out_specs=pl.BlockSpec((tm, d), prev_tile),
            scratch_shapes=[pltpu.VMEM((2, tm, d), jnp.float32), pltpu.SemaphoreType.DMA((2,))],
        ),
        out_shape=jax.ShapeDtypeStruct((n, d), jnp.float32),
        compiler_params=pltpu.CompilerParams(
            dimension_semantics=("arbitrary",), vmem_limit_bytes=VMEM_LIMIT_BYTES),
        name="moe_combine",
    )(pos, h, y_sorted, g_final)


def _tile(n, pref):
    t = min(n, pref)
    assert n % t == 0, (n, t)
    return t


def _sorted_layout(bucket, rank, counts, *, tm_e, n_tiles_e):
    seg_tiles = (counts + tm_e - 1) // tm_e
    seg_end = jnp.cumsum(seg_tiles)
    seg_start = seg_end - seg_tiles
    pos = (seg_start * tm_e)[bucket] + rank
    nonempty = seg_tiles > 0
    last_tiles = jnp.sort(jnp.where(nonempty, seg_end - 1, n_tiles_e))
    n_active = seg_end[-1]
    tile = jnp.minimum(jnp.arange(n_tiles_e, dtype=jnp.int32), n_active - 1)
    tile_bucket = jnp.sum((tile[:, None] >= seg_end[None, :]).astype(jnp.int32), axis=1)
    group, pair = tile_bucket // len(PAIRS), tile_bucket % len(PAIRS)
    pair_a = jnp.array([p[0] for p in PAIRS], jnp.int32)
    pair_b = jnp.array([p[1] for p in PAIRS], jnp.int32)
    tile_ea = group * EXPERTS_PER_GROUP + pair_a[pair]
    tile_eb = group * EXPERTS_PER_GROUP + pair_b[pair]
    i32 = lambda a: a.astype(jnp.int32)
    return i32(pos), i32(last_tiles), i32(jnp.sum(nonempty)), i32(tile_ea), i32(tile_eb), i32(n_active)


def kernel(x, mem, norm_mix, w_in, w_pool, pool_scale, w_out, norm_cross, norm_mem, w_q_mem, w_k_mem,
           w_v_mem, w_o_mem, norm_ffn, w_group, w_expert, w_gate, w_up, w_down, norm_final):
    b, s, d = x.shape
    mem_len = mem.shape[1]
    depth = norm_mix.shape[0]
    pool_width = pool_scale.shape[1]
    attn_width = w_out.shape[1] - pool_width
    bf = jnp.bfloat16
    n = b * s

    assert depth == 1, "single-layer problem: the final RMSNorm is fused into the combine kernel"
    l = 0
    h = x.reshape(n, d)
    u, qkv = _in_proj(h, norm_mix[l][None], w_in[l].astype(bf),
                      pool_width=pool_width, attn_width=attn_width, tm=_tile(n, 1024))
    tq = _tile(s, 4 * MXU_DIM_V7X)
    sub = min(tq, MXU_DIM_V7X)
    attn = _sb_attention(qkv.reshape(-1, b, s, LANES_V7X), tq=tq, sub=sub, win=2 * sub)
    kmem, vmem = _mem_kv(mem.reshape(b * mem_len, d), norm_mem[l][None],
                         w_k_mem[l].astype(bf), w_v_mem[l].astype(bf), tm=mem_len)

    w_router = jnp.concatenate([w_group[l], w_expert[l]], axis=1)
    w_router = jnp.pad(w_router, ((0, 0), (0, ROUTER_LANES - w_router.shape[1])))
    wr_hi = w_router.astype(bf)
    wr_lo = (w_router - wr_hi.astype(jnp.float32)).astype(bf)
    w_pool_bd = jax.scipy.linalg.block_diag(*w_pool[l]).astype(bf)
    h, xrow, bucket, rank, counts = _mix_cross(
        h, u, attn.reshape(-1, n, LANES_V7X), w_pool_bd, pool_scale[l][None], w_out[l].astype(bf),
        norm_cross[l][None], w_q_mem[l].astype(bf), kmem, vmem, w_o_mem[l].astype(bf),
        norm_ffn[l][None], jnp.concatenate([wr_hi, wr_lo], axis=1), tm=_tile(s, 512), seq=s, mem_len=mem_len)

    tm_e = _tile(n, MXU_DIM_V7X)
    n_tiles_e = -(-(n + N_BUCKETS * (tm_e - 1)) // tm_e)
    pos, last_tiles, n_last, tile_ea, tile_eb, n_active = _sorted_layout(
        bucket.reshape(n), rank.reshape(n), counts[:N_BUCKETS, 0].astype(jnp.int32),
        tm_e=tm_e, n_tiles_e=n_tiles_e)
    xs = _dispatch(pos, last_tiles, n_last[None], n_active[None], xrow, tm=_tile(n, 1024), tm_e=tm_e,
                   n_tiles_e=n_tiles_e)
    w_gate_up = jnp.concatenate([w_gate[l], w_up[l]], axis=-1).astype(bf)
    y_sorted = _experts(tile_ea, tile_eb, n_active[None], xs, w_gate_up, w_down[l].astype(bf), tm=tm_e)
    out = _combine(pos, h, y_sorted, norm_final[None], tm=_tile(n, 1024))
    return out.reshape(b, s, d)
```

```python
import functools
import math

import jax
import jax.numpy as jnp
from jax import lax
from jax.experimental import pallas as pl
from jax.experimental.pallas import tpu as pltpu

RMS_EPS = 1e-6
POOL_WINDOWS = (2, 4, 8, 16)
SB_HEAD_DIM = 64
MEM_HEADS = 4
N_GROUPS = 4
EXPERTS_PER_GROUP = 4
N_EXPERTS = N_GROUPS * EXPERTS_PER_GROUP

LANES_V7X = 128
MXU_DIM_V7X = 256
VMEM_LIMIT_BYTES = 56 * 1024 * 1024

LOG2E = 1.4426950408889634
UNDERFLOW_EXPONENT = 104.0
MASKED_LOGIT = -1e30
POOL_HALO = 16

PAIRS = ((0, 1), (0, 2), (1, 2), (1, 3), (2, 3), (0, 3))
assert sorted(PAIRS) == [(a, b) for a in range(EXPERTS_PER_GROUP) for b in range(a + 1, EXPERTS_PER_GROUP)]
N_BUCKETS = N_GROUPS * len(PAIRS)
ROUTER_ROWS = 32
ROUTER_LANES = LANES_V7X
PAYLOAD_LANES = LANES_V7X
assert N_GROUPS + N_EXPERTS <= ROUTER_ROWS and N_BUCKETS <= ROUTER_ROWS


def _rms(x, gain):
    ms = jnp.mean(x * x, axis=-1, keepdims=True)
    return x * lax.rsqrt(ms + RMS_EPS) * gain


def _dot(a, b):
    return jnp.dot(a, b, preferred_element_type=jnp.float32)


def _dot_nt(a, b):
    return lax.dot_general(a, b, (((1,), (1,)), ((), ())), preferred_element_type=jnp.float32)


def _in_proj_kernel(x_ref, g_ref, w_ref, u_ref, qkv_ref, *, pool_width, attn_width, q_scale):
    xn = _rms(x_ref[...], g_ref[...]).astype(jnp.bfloat16)
    proj = _dot(xn, w_ref[...])
    u_ref[...] = proj[:, :pool_width]
    q = proj[:, pool_width:pool_width + attn_width] * q_scale
    qkv = jnp.concatenate([q, proj[:, pool_width + attn_width:]], axis=1).astype(jnp.bfloat16)
    for j in range(qkv_ref.shape[0]):
        qkv_ref[j] = qkv[:, j * LANES_V7X:(j + 1) * LANES_V7X]


def _in_proj(x2, gain, w_in_bf16, *, pool_width, attn_width, tm):
    n, d = x2.shape
    in_width = w_in_bf16.shape[1]
    n_slabs = 3 * attn_width // LANES_V7X
    kern = functools.partial(_in_proj_kernel, pool_width=pool_width, attn_width=attn_width,
                             q_scale=1.0 / math.sqrt(SB_HEAD_DIM))
    return pl.pallas_call(
        kern,
        grid=(n // tm,),
        in_specs=[
            pl.BlockSpec((tm, d), lambda i: (i, 0)),
            pl.BlockSpec((1, d), lambda i: (0, 0)),
            pl.BlockSpec((d, in_width), lambda i: (0, 0)),
        ],
        out_specs=[
            pl.BlockSpec((tm, pool_width), lambda i: (i, 0)),
            pl.BlockSpec((n_slabs, tm, LANES_V7X), lambda i: (0, i, 0)),
        ],
        out_shape=[
            jax.ShapeDtypeStruct((n, pool_width), jnp.float32),
            jax.ShapeDtypeStruct((n_slabs, n, LANES_V7X), jnp.bfloat16),
        ],
        compiler_params=pltpu.CompilerParams(
            dimension_semantics=("parallel",), vmem_limit_bytes=VMEM_LIMIT_BYTES),
        name="in_proj",
    )(x2, gain, w_in_bf16)


def _sb_tile(qh, kb, vb, tri, c, tail_mask):
    z = _dot_nt(qh, kb)
    if tail_mask is not None:
        m = tail_mask.shape[1]
        tail = jnp.where(tail_mask, z[:, -m:], MASKED_LOGIT)
        z = tail if m == z.shape[1] else jnp.concatenate([z[:, :-m], tail], axis=1)
    sp = (jnp.maximum(z, 0.0) + jnp.log(1.0 + jnp.exp2(jnp.abs(z) * (-LOG2E)))).astype(jnp.bfloat16)
    n = tri.shape[0]
    chunks, total = [], None
    for k0 in range(z.shape[1] - n, -1, -n):
        cum_k = _dot(sp[:, k0:k0 + n], tri)
        if total is not None:
            cum_k = cum_k + total
        total = cum_k[:, :1]
        chunks.insert(0, cum_k)
    cum = chunks[0] if len(chunks) == 1 else jnp.concatenate(chunks, axis=1)
    w = jnp.exp(z - cum) if c is None else jnp.exp(jnp.minimum(z - cum, 0.0) - c)
    return (total if c is None else c + total), _dot(w.astype(jnp.bfloat16), vb)


def _sb_attn_kernel(q_ref, k_ref, v_ref, tri_ref, o_ref, *, tq, sub, win):
    qi = pl.program_id(2)
    n_sub = tq // sub
    lane = lax.broadcasted_iota(jnp.int32, (tq, LANES_V7X), 1)
    head0 = lane < SB_HEAD_DIM
    q2 = q_ref[...]
    zero = jnp.zeros_like(q2)
    q_heads = (jnp.where(head0, q2, zero), jnp.where(head0, zero, q2))
    q_st = jnp.concatenate(q_heads, axis=0)

    def kv_block(j):
        start = pl.multiple_of(j * tq, tq)
        return k_ref[pl.ds(start, tq), :], v_ref[pl.ds(start, tq), :]

    def stacked_iota(rows, cols):
        row = lax.broadcasted_iota(jnp.int32, (2 * rows, cols), 0)
        col = lax.broadcasted_iota(jnp.int32, (2 * rows, cols), 1)
        return jnp.where(row >= rows, row - rows, row), col

    def first_block():
        row, col = stacked_iota(tq, tq)
        kb, vb = kv_block(0)
        return _sb_tile(q_st, kb, vb, tri_ref[...], None, col < row)

    def windows():
        row, col = stacked_iota(sub, sub)
        causal = col < row
        tri = tri_ref[...]
        parts = []
        for hf in range(n_sub):
            start = pl.multiple_of((qi * n_sub + hf + 1) * sub - win, sub)
            kw = k_ref[pl.ds(start, win), :]
            vw = v_ref[pl.ds(start, win), :]
            q_sub = jnp.concatenate([qh[hf * sub:(hf + 1) * sub] for qh in q_heads], axis=0)
            parts.append(_sb_tile(q_sub, kw, vw, tri, None, causal))
        return tuple(jnp.concatenate([p[i][h * sub:(h + 1) * sub] for h in range(2) for p in parts], axis=0)
                     for i in range(2))

    carry = lax.cond(qi == 0, first_block, windows)

    def live(carry):
        return jnp.min(carry[0], axis=0, keepdims=True)[0, 0] < UNDERFLOW_EXPONENT

    def add_block(carry, j, tail_mask):
        kb, vb = kv_block(j)
        c, acc = _sb_tile(q_st, kb, vb, tri_ref[...], carry[0], tail_mask)
        return c, carry[1] + acc

    def partial_block(back):
        def fn(carry):
            row, col = stacked_iota(tq, tq)
            fresh = col < (row // sub + 1) * sub - win + back * tq
            return add_block(carry, qi - back, fresh)
        return fn

    for back in ((0, 1) if win < tq else (1,)):
        carry = lax.cond(jnp.logical_and(qi > 0, live(carry)), partial_block(back), lambda cr: cr, carry)

    def cond(state):
        step, alive, _ = state
        return jnp.logical_and(step < qi - 1, alive > 0)

    def body(state):
        step, _, carry = state
        carry = add_block(carry, qi - 2 - step, None)
        return step + 1, live(carry).astype(jnp.int32), carry

    _, _, carry = lax.while_loop(cond, body, (jnp.int32(0), live(carry).astype(jnp.int32), carry))
    acc = carry[1]
    o_ref[...] = jnp.where(head0, acc[:tq], acc[tq:]).astype(o_ref.dtype)


def _sb_attention(qkv, *, tq, sub, win):
    n_slabs, b, s, _ = qkv.shape
    assert tq % sub == 0 and win % sub == 0 and sub <= win <= tq + sub, (tq, sub, win)
    n_pairs = n_slabs // 3
    n = math.gcd(math.gcd(tq, win), MXU_DIM_V7X)
    tri = (lax.broadcasted_iota(jnp.int32, (n, n), 0)
           >= lax.broadcasted_iota(jnp.int32, (n, n), 1)).astype(jnp.bfloat16)

    kern = functools.partial(_sb_attn_kernel, tq=tq, sub=sub, win=win)
    const2 = lambda bi, hp, qi: (0, 0)
    return pl.pallas_call(
        kern,
        grid=(b, n_pairs, s // tq),
        in_specs=[
            pl.BlockSpec((None, None, tq, LANES_V7X), lambda bi, hp, qi: (hp, bi, qi, 0)),
            pl.BlockSpec((None, None, s, LANES_V7X), lambda bi, hp, qi: (n_pairs + hp, bi, 0, 0)),
            pl.BlockSpec((None, None, s, LANES_V7X), lambda bi, hp, qi: (2 * n_pairs + hp, bi, 0, 0)),
            pl.BlockSpec((n, n), const2),
        ],
        out_specs=pl.BlockSpec((None, None, tq, LANES_V7X), lambda bi, hp, qi: (hp, bi, qi, 0)),
        out_shape=jax.ShapeDtypeStruct((n_pairs, b, s, LANES_V7X), jnp.bfloat16),
        compiler_params=pltpu.CompilerParams(
            dimension_semantics=("parallel", "parallel", "parallel"),
            vmem_limit_bytes=VMEM_LIMIT_BYTES),
        name="sb_attention",
    )(qkv, qkv, qkv, tri)


def _mem_kv_kernel(m_ref, g_ref, wk_ref, wv_ref, k_ref, v_ref):
    mn = _rms(m_ref[...], g_ref[...]).astype(jnp.bfloat16)
    k_ref[...] = _dot(mn, wk_ref[...]).astype(jnp.bfloat16)
    v_ref[...] = _dot(mn, wv_ref[...]).astype(jnp.bfloat16)


def _mem_kv(mem2, gain, wk, wv, *, tm):
    n, d = mem2.shape
    row = pl.BlockSpec((tm, d), lambda i: (i, 0))
    full = pl.BlockSpec((d, d), lambda i: (0, 0))
    return pl.pallas_call(
        _mem_kv_kernel,
        grid=(n // tm,),
        in_specs=[row, pl.BlockSpec((1, d), lambda i: (0, 0)), full, full],
        out_specs=[row, row],
        out_shape=[jax.ShapeDtypeStruct((n, d), jnp.bfloat16)] * 2,
        compiler_params=pltpu.CompilerParams(
            dimension_semantics=("parallel",), vmem_limit_bytes=VMEM_LIMIT_BYTES),
        name="mem_kv",
    )(mem2, gain, wk, wv)


def _route_t(logits):
    neg = jnp.float32(-jnp.inf)
    big = jnp.float32(ROUTER_ROWS)
    row = lax.broadcasted_iota(jnp.int32, logits.shape, 0).astype(jnp.float32)

    def col_max(mask):
        return jnp.max(jnp.where(mask, logits, neg), axis=0, keepdims=True)

    def first_argmax(mask, mx):
        return jnp.min(jnp.where(mask & (logits == mx), row, big), axis=0, keepdims=True)

    gmask = row < N_GROUPS
    gmax = col_max(gmask)
    gsum = jnp.sum(jnp.where(gmask, jnp.exp(logits - gmax), 0.0), axis=0, keepdims=True)
    g_gate = 1.0 / gsum
    g_idx = first_argmax(gmask, gmax)

    lo = N_GROUPS + EXPERTS_PER_GROUP * g_idx
    emask = (row >= lo) & (row < lo + EXPERTS_PER_GROUP)
    m1 = col_max(emask)
    i1 = first_argmax(emask, m1)
    mask2 = emask & (row != i1)
    m2 = col_max(mask2)
    i2 = first_argmax(mask2, m2)
    esum = jnp.sum(jnp.where(emask, jnp.exp(logits - m1), 0.0), axis=0, keepdims=True)
    p1 = 1.0 / esum
    p2 = jnp.exp(m2 - m1) / esum
    tot = p1 + p2
    w1 = g_gate * (p1 / tot)
    w2 = g_gate * (p2 / tot)

    first = i1 < i2
    la = jnp.where(first, i1, i2) - lo
    lb = jnp.where(first, i2, i1) - lo
    pair = sum(jnp.where((la == a) & (lb == b), float(p), 0.0) for p, (a, b) in enumerate(PAIRS))
    bucket = g_idx * len(PAIRS) + pair
    return bucket, jnp.where(first, w1, w2), jnp.where(first, w2, w1)


def _mix_cross_kernel(x_ref, u_ref, halo_ref, a_ref, wpbd_ref, ps_ref, wo_ref, gc_ref,
                      wq_ref, km_ref, vm_ref, wom_ref, gf_ref, wr_ref, su_ref,
                      h_ref, xrow_ref, bucket_ref, rank_ref, cnt_ref, run_ref, *, tm, tiles_per_seq):
    i = pl.program_id(0)
    pool_width = u_ref.shape[1]
    gw = pool_width // len(POOL_WINDOWS)
    tile_in_seq = i % tiles_per_seq
    first = tile_in_seq == 0
    pos = tile_in_seq * tm + lax.broadcasted_iota(jnp.int32, (tm, 1), 0)

    halo = jnp.where(first, 0.0, halo_ref[...])
    u = u_ref[...]
    pooled = []
    for g, w in enumerate(POOL_WINDOWS):
        ug = u[:, g * gw:(g + 1) * gw]
        ext = jnp.concatenate([halo[:, g * gw:(g + 1) * gw], ug], axis=0)
        shift = 1
        while shift < w:
            ext = ext + pltpu.roll(ext, shift, 0)
            shift *= 2
        win = ext[POOL_HALO:, :]
        inv_count = 1.0 / jnp.minimum(pos + 1, w).astype(jnp.float32)
        pooled.append((win * inv_count - ug).astype(jnp.bfloat16))

    pool_out = _dot(jnp.concatenate(pooled, axis=1), wpbd_ref[...]) * ps_ref[...]
    mixed = jnp.concatenate([pool_out.astype(jnp.bfloat16)] + [a_ref[j] for j in range(a_ref.shape[0])], axis=1)
    h1 = x_ref[...] + _dot(mixed, wo_ref[...])

    hn = _rms(h1, gc_ref[...]).astype(jnp.bfloat16)
    d = h1.shape[1]
    hd = d // MEM_HEADS
    q = (_dot(hn, wq_ref[...]) * (1.0 / math.sqrt(hd))).astype(jnp.bfloat16)
    outs = []
    for hh in range(MEM_HEADS):
        sl = slice(hh * hd, (hh + 1) * hd)
        s = _dot_nt(q[:, sl], km_ref[:, sl])
        e = jnp.exp(s - jnp.max(s, axis=-1, keepdims=True))
        p = e * (1.0 / jnp.sum(e, axis=-1, keepdims=True))
        outs.append(_dot(p.astype(jnp.bfloat16), vm_ref[:, sl]))
    o = jnp.concatenate(outs, axis=-1).astype(jnp.bfloat16)
    h2 = h1 + _dot(o, wom_ref[...])
    h_ref[...] = h2

    xn = _rms(h2, gf_ref[...])
    xrow_ref[:, :d] = xn
    x_hi = xn.astype(jnp.bfloat16)
    x_lo = (xn - x_hi.astype(jnp.float32)).astype(jnp.bfloat16)
    both = _dot(x_hi, wr_ref[...])
    logits_tm = both[:, :ROUTER_LANES] + both[:, ROUTER_LANES:] + _dot(x_lo, wr_ref[:, :ROUTER_LANES])
    logits = logits_tm.T[:ROUTER_ROWS]
    bucket, w_a, w_b = _route_t(logits)

    @pl.when(i == 0)
    def _():
        run_ref[...] = jnp.zeros_like(run_ref)

    brow = lax.broadcasted_iota(jnp.int32, (ROUTER_ROWS, tm), 0).astype(jnp.float32)
    onehot = (brow == bucket).astype(jnp.float32)
    before = _dot(onehot.astype(jnp.bfloat16), su_ref[...])
    run = run_ref[...]
    rank = jnp.sum(onehot * (before + run[:, :1]), axis=0, keepdims=True)
    run = run + jnp.sum(onehot, axis=1, keepdims=True)
    run_ref[...] = run
    cnt_ref[...] = run
    bucket_ref[...] = bucket.astype(jnp.int32)
    rank_ref[...] = rank.astype(jnp.int32)

    prow = lax.broadcasted_iota(jnp.int32, (PAYLOAD_LANES, tm), 0)
    payload_t = jnp.where(prow == 0, w_a, jnp.where(prow == 1, w_b, 0.0))
    xrow_ref[:, d:] = payload_t.T


def _mix_cross(x2, u, attn, w_pool_bd, pool_scale, w_out, g_cross, wq, kmem, vmem, wom, g_ffn, w_router,
               *, tm, seq, mem_len):
    n, d = x2.shape
    pool_width = u.shape[1]
    tiles_per_seq = seq // tm
    n_tiles = n // tm
    halo_blocks = tm // POOL_HALO
    su = (lax.broadcasted_iota(jnp.int32, (tm, tm), 0)
          < lax.broadcasted_iota(jnp.int32, (tm, tm), 1)).astype(jnp.bfloat16)
    kern = functools.partial(_mix_cross_kernel, tm=tm, tiles_per_seq=tiles_per_seq)
    const2 = lambda i: (0, 0)
    mem_map = lambda i: (i // tiles_per_seq, 0)
    return pl.pallas_call(
        kern,
        grid=(n_tiles,),
        in_specs=[
            pl.BlockSpec((tm, d), lambda i: (i, 0)),
            pl.BlockSpec((tm, pool_width), lambda i: (i, 0)),
            pl.BlockSpec((POOL_HALO, pool_width), lambda i: (jnp.maximum(i * halo_blocks - 1, 0), 0)),
            pl.BlockSpec((attn.shape[0], tm, LANES_V7X), lambda i: (0, i, 0)),
            pl.BlockSpec(w_pool_bd.shape, const2),
            pl.BlockSpec((1, pool_width), const2),
            pl.BlockSpec(w_out.shape, const2),
            pl.BlockSpec((1, d), const2),
            pl.BlockSpec((d, d), const2),
            pl.BlockSpec((mem_len, d), mem_map),
            pl.BlockSpec((mem_len, d), mem_map),
            pl.BlockSpec((d, d), const2),
            pl.BlockSpec((1, d), const2),
            pl.BlockSpec((d, 2 * ROUTER_LANES), const2),
            pl.BlockSpec((tm, tm), const2),
        ],
        out_specs=[
            pl.BlockSpec((tm, d), lambda i: (i, 0)),
            pl.BlockSpec((tm, d + PAYLOAD_LANES), lambda i: (i, 0)),
            pl.BlockSpec((None, 1, tm), lambda i: (i, 0, 0)),
            pl.BlockSpec((None, 1, tm), lambda i: (i, 0, 0)),
            pl.BlockSpec((ROUTER_ROWS, LANES_V7X), const2),
        ],
        out_shape=[
            jax.ShapeDtypeStruct((n, d), jnp.float32),
            jax.ShapeDtypeStruct((n, d + PAYLOAD_LANES), jnp.float32),
            jax.ShapeDtypeStruct((n_tiles, 1, tm), jnp.int32),
            jax.ShapeDtypeStruct((n_tiles, 1, tm), jnp.int32),
            jax.ShapeDtypeStruct((ROUTER_ROWS, LANES_V7X), jnp.float32),
        ],
        scratch_shapes=[pltpu.VMEM((ROUTER_ROWS, LANES_V7X), jnp.float32)],
        compiler_params=pltpu.CompilerParams(
            dimension_semantics=("arbitrary",), vmem_limit_bytes=VMEM_LIMIT_BYTES),
        name="mix_cross",
    )(x2, u, u, attn, w_pool_bd, pool_scale, w_out, g_cross, wq, kmem, vmem, wom, g_ffn, w_router, su)


def _dispatch_kernel(pos_ref, last_ref, n_last_ref, na_ref, x_ref, xs_ref, stage, zeros, sems, fill_sem,
                     *, tm, n_tiles, n_tiles_e):
    i = pl.program_id(0)
    slot = i % 2
    base = i * tm
    tm_e = zeros.shape[0]

    def fill_copy(j):
        return pltpu.make_async_copy(zeros, xs_ref.at[pl.ds(pl.multiple_of(j * tm_e, tm_e), tm_e), :], fill_sem)

    def for_range(lo, hi, fn):
        def body(k, carry):
            fn(k)
            return carry
        lax.fori_loop(lo, hi, body, 0)

    @pl.when(i == 0)
    def _():
        zeros[...] = jnp.zeros_like(zeros)
        for_range(0, n_last_ref[0], lambda k: fill_copy(last_ref[k]).start())
        for_range(na_ref[0], n_tiles_e, lambda j: fill_copy(j).start())
        for_range(0, n_last_ref[0], lambda k: fill_copy(0).wait())
        for_range(na_ref[0], n_tiles_e, lambda j: fill_copy(0).wait())

    def row_copy(slot, r, p):
        return pltpu.make_async_copy(stage.at[slot, pl.ds(r, 1), :], xs_ref.at[pl.ds(p, 1), :], sems.at[slot])

    def wait_slot(slot):
        pltpu.make_async_copy(stage.at[slot], xs_ref.at[pl.ds(0, tm), :], sems.at[slot]).wait()

    stage[slot] = x_ref[...]
    for r in range(tm):
        row_copy(slot, r, pos_ref[base + r]).start(priority=r % 2)

    @pl.when(i > 0)
    def _():
        wait_slot(1 - slot)

    @pl.when(i == n_tiles - 1)
    def _():
        wait_slot(slot)


def _dispatch(pos, last_tiles, n_last, n_active, xrow, *, tm, tm_e, n_tiles_e):
    n, width = xrow.shape
    return pl.pallas_call(
        functools.partial(_dispatch_kernel, tm=tm, n_tiles=n // tm, n_tiles_e=n_tiles_e),
        grid_spec=pltpu.PrefetchScalarGridSpec(
            num_scalar_prefetch=4,
            grid=(n // tm,),
            in_specs=[pl.BlockSpec((tm, width), lambda i, *_: (i, 0))],
            out_specs=pl.BlockSpec(memory_space=pl.ANY),
            scratch_shapes=[pltpu.VMEM((2, tm, width), xrow.dtype), pltpu.VMEM((tm_e, width), xrow.dtype),
                            pltpu.SemaphoreType.DMA((2,)), pltpu.SemaphoreType.DMA],
        ),
        out_shape=jax.ShapeDtypeStruct((n_tiles_e * tm_e, width), xrow.dtype),
        compiler_params=pltpu.CompilerParams(
            dimension_semantics=("arbitrary",), vmem_limit_bytes=VMEM_LIMIT_BYTES),
        name="moe_dispatch",
    )(pos, last_tiles, n_last, n_active, xrow)


def _expert_kernel(ea_ref, eb_ref, na_ref, xs_ref, wgua_ref, wda_ref, wgub_ref, wdb_ref, y_ref, *, d):
    del ea_ref, eb_ref
    active = pl.program_id(0) < na_ref[0]

    @pl.when(jnp.logical_not(active))
    def _():
        y_ref[...] = jnp.zeros_like(y_ref)

    @pl.when(active)
    def _():
        x = xs_ref[:, :d].astype(jnp.bfloat16)

        def mlp(wgu_ref, wd_ref):
            gu = _dot(x, wgu_ref[...])
            ff = gu.shape[1] // 2
            gate, up = gu[:, :ff], gu[:, ff:]
            hmid = (gate * (1.0 / (1.0 + jnp.exp(-gate)))) * up
            return _dot(hmid.astype(jnp.bfloat16), wd_ref[...])

        y_ref[...] = (xs_ref[:, d:d + 1] * mlp(wgua_ref, wda_ref)
                      + xs_ref[:, d + 1:d + 2] * mlp(wgub_ref, wdb_ref))


def _experts(tile_ea, tile_eb, n_active, xs, wgu, wd, *, tm):
    n_pad, width = xs.shape
    d, ff = wd.shape[2], wd.shape[1]
    row_map = lambda j, ea, eb, na: (jnp.minimum(j, na[0] - 1), 0)
    a_map = lambda j, ea, eb, na: (ea[j], 0, 0)
    b_map = lambda j, ea, eb, na: (eb[j], 0, 0)
    return pl.pallas_call(
        functools.partial(_expert_kernel, d=d),
        grid_spec=pltpu.PrefetchScalarGridSpec(
            num_scalar_prefetch=3,
            grid=(n_pad // tm,),
            in_specs=[
                pl.BlockSpec((tm, width), row_map),
                pl.BlockSpec((None, d, 2 * ff), a_map),
                pl.BlockSpec((None, ff, d), a_map),
                pl.BlockSpec((None, d, 2 * ff), b_map),
                pl.BlockSpec((None, ff, d), b_map),
            ],
            out_specs=pl.BlockSpec((tm, d), lambda j, ea, eb, na: (j, 0)),
        ),
        out_shape=jax.ShapeDtypeStruct((n_pad, d), jnp.float32),
        compiler_params=pltpu.CompilerParams(
            dimension_semantics=("arbitrary",), vmem_limit_bytes=VMEM_LIMIT_BYTES),
        name="moe_experts",
    )(tile_ea, tile_eb, n_active, xs, wgu, wd, wgu, wd)


def _combine_kernel(pos_ref, h_ref, y_ref, g_ref, o_ref, ybuf, sems, *, tm, n_tiles):
    i = pl.program_id(0)

    def row_copy(slot, r, p):
        return pltpu.make_async_copy(y_ref.at[pl.ds(p, 1), :], ybuf.at[slot, pl.ds(r, 1), :], sems.at[slot])

    @pl.when(i < n_tiles)
    def _():
        slot = i % 2
        for r in range(tm):
            row_copy(slot, r, pos_ref[i * tm + r]).start(priority=r % 2)

    @pl.when(i > 0)
    def _():
        slot = (i - 1) % 2
        pltpu.make_async_copy(y_ref.at[pl.ds(0, tm), :], ybuf.at[slot], sems.at[slot]).wait()
        o_ref[...] = _rms(h_ref[...] + ybuf[slot], g_ref[...])


def _combine(pos, h, y_sorted, g_final, *, tm):
    n, d = h.shape
    n_tiles = n // tm
    prev_tile = lambda i, pos: (jnp.maximum(i - 1, 0), 0)
    return pl.pallas_call(
        functools.partial(_combine_kernel, tm=tm, n_tiles=n_tiles),
        grid_spec=pltpu.PrefetchScalarGridSpec(
            num_scalar_prefetch=1,
            grid=(n_tiles + 1,),
            in_specs=[
                pl.BlockSpec((tm, d), prev_tile),
                pl.BlockSpec(memory_space=pl.ANY),
                pl.BlockSpec((1, d), lambda i, pos: (0, 0)),
            ],
            out_specs=pl.BlockSpec((tm, d), prev_tile),
            scratch_shapes=[pltpu.VMEM((2, tm, d), jnp.float32), pltpu.SemaphoreType.DMA((2,))],
        ),
        out_shape=jax.ShapeDtypeStruct((n, d), jnp.float32),
        compiler_params=pltpu.CompilerParams(
            dimension_semantics=("arbitrary",), vmem_limit_bytes=VMEM_LIMIT_BYTES),
        name="moe_combine",
    )(pos, h, y_sorted, g_final)


def _tile(n, pref):
    t = min(n, pref)
    assert n % t == 0, (n, t)
    return t


def _sorted_layout(bucket, rank, counts, *, tm_e, n_tiles_e):
    seg_tiles = (counts + tm_e - 1) // tm_e
    seg_end = jnp.cumsum(seg_tiles)
    seg_start = seg_end - seg_tiles
    pos = (seg_start * tm_e)[bucket] + rank
    nonempty = seg_tiles > 0
    last_tiles = jnp.sort(jnp.where(nonempty, seg_end - 1, n_tiles_e))
    n_active = seg_end[-1]
    tile = jnp.minimum(jnp.arange(n_tiles_e, dtype=jnp.int32), n_active - 1)
    tile_bucket = jnp.sum((tile[:, None] >= seg_end[None, :]).astype(jnp.int32), axis=1)
    group, pair = tile_bucket // len(PAIRS), tile_bucket % len(PAIRS)
    pair_a = jnp.array([p[0] for p in PAIRS], jnp.int32)
    pair_b = jnp.array([p[1] for p in PAIRS], jnp.int32)
    tile_ea = group * EXPERTS_PER_GROUP + pair_a[pair]
    tile_eb = group * EXPERTS_PER_GROUP + pair_b[pair]
    i32 = lambda a: a.astype(jnp.int32)
    return i32(pos), i32(last_tiles), i32(jnp.sum(nonempty)), i32(tile_ea), i32(tile_eb), i32(n_active)


def kernel(x, mem, norm_mix, w_in, w_pool, pool_scale, w_out, norm_cross, norm_mem, w_q_mem, w_k_mem,
           w_v_mem, w_o_mem, norm_ffn, w_group, w_expert, w_gate, w_up, w_down, norm_final):
    b, s, d = x.shape
    mem_len = mem.shape[1]
    depth = norm_mix.shape[0]
    pool_width = pool_scale.shape[1]
    attn_width = w_out.shape[1] - pool_width
    bf = jnp.bfloat16
    n = b * s

    assert depth == 1, "single-layer problem: the final RMSNorm is fused into the combine kernel"
    l = 0
    h = x.reshape(n, d)
    u, qkv = _in_proj(h, norm_mix[l][None], w_in[l].astype(bf),
                      pool_width=pool_width, attn_width=attn_width, tm=_tile(n, 1024))
    tq = _tile(s, 2 * MXU_DIM_V7X)
    sub = min(tq, MXU_DIM_V7X)
    attn = _sb_attention(qkv.reshape(-1, b, s, LANES_V7X), tq=tq, sub=sub, win=2 * sub)
    kmem, vmem = _mem_kv(mem.reshape(b * mem_len, d), norm_mem[l][None],
                         w_k_mem[l].astype(bf), w_v_mem[l].astype(bf), tm=mem_len)

    w_router = jnp.concatenate([w_group[l], w_expert[l]], axis=1)
    w_router = jnp.pad(w_router, ((0, 0), (0, ROUTER_LANES - w_router.shape[1])))
    wr_hi = w_router.astype(bf)
    wr_lo = (w_router - wr_hi.astype(jnp.float32)).astype(bf)
    w_pool_bd = jax.scipy.linalg.block_diag(*w_pool[l]).astype(bf)
    h, xrow, bucket, rank, counts = _mix_cross(
        h, u, attn.reshape(-1, n, LANES_V7X), w_pool_bd, pool_scale[l][None], w_out[l].astype(bf),
        norm_cross[l][None], w_q_mem[l].astype(bf), kmem, vmem, w_o_mem[l].astype(bf),
        norm_ffn[l][None], jnp.concatenate([wr_hi, wr_lo], axis=1), tm=_tile(s, 1024), seq=s, mem_len=mem_len)

    tm_e = _tile(n, MXU_DIM_V7X)
    n_tiles_e = -(-(n + N_BUCKETS * (tm_e - 1)) // tm_e)
    pos, last_tiles, n_last, tile_ea, tile_eb, n_active = _sorted_layout(
        bucket.reshape(n), rank.reshape(n), counts[:N_BUCKETS, 0].astype(jnp.int32),
        tm_e=tm_e, n_tiles_e=n_tiles_e)
    xs = _dispatch(pos, last_tiles, n_last[None], n_active[None], xrow, tm=_tile(n, 1024), tm_e=tm_e,
                   n_tiles_e=n_tiles_e)
    w_gate_up = jnp.concatenate([w_gate[l], w_up[l]], axis=-1).astype(bf)
    y_sorted = _experts(tile_ea, tile_eb, n_active[None], xs, w_gate_up, w_down[l].astype(bf), tm=tm_e)
    out = _combine(pos, h, y_sorted, norm_final[None], tm=_tile(n, 512))
    return out.reshape(b, s, d)
```

```python
import functools
import math

import jax
import jax.numpy as jnp
from jax import lax
from jax.experimental import pallas as pl
from jax.experimental.pallas import tpu as pltpu

RMS_EPS = 1e-6
POOL_WINDOWS = (2, 4, 8, 16)
SB_HEAD_DIM = 64
MEM_HEADS = 4
N_GROUPS = 4
EXPERTS_PER_GROUP = 4
N_EXPERTS = N_GROUPS * EXPERTS_PER_GROUP

LANES_V7X = 128
MXU_DIM_V7X = 256
VMEM_LIMIT_BYTES = 56 * 1024 * 1024

LOG2E = 1.4426950408889634
UNDERFLOW_EXPONENT = 104.0
MASKED_LOGIT = -1e30
POOL_HALO = 16

PAIRS = ((0, 1), (0, 2), (1, 2), (1, 3), (2, 3), (0, 3))
assert sorted(PAIRS) == [(a, b) for a in range(EXPERTS_PER_GROUP) for b in range(a + 1, EXPERTS_PER_GROUP)]
N_BUCKETS = N_GROUPS * len(PAIRS)
ROUTER_ROWS = 32
ROUTER_LANES = LANES_V7X
PAYLOAD_LANES = LANES_V7X
assert N_GROUPS + N_EXPERTS <= ROUTER_ROWS and N_BUCKETS <= ROUTER_ROWS


def _rms(x, gain):
    ms = jnp.mean(x * x, axis=-1, keepdims=True)
    return x * lax.rsqrt(ms + RMS_EPS) * gain


def _dot(a, b):
    return jnp.dot(a, b, preferred_element_type=jnp.float32)


def _dot_nt(a, b):
    return lax.dot_general(a, b, (((1,), (1,)), ((), ())), preferred_element_type=jnp.float32)


def _in_proj_kernel(x_ref, g_ref, w_ref, u_ref, qkv_ref, *, pool_width, attn_width, q_scale):
    xn = _rms(x_ref[...], g_ref[...]).astype(jnp.bfloat16)
    proj = _dot(xn, w_ref[...])
    u_ref[...] = proj[:, :pool_width]
    q = proj[:, pool_width:pool_width + attn_width] * q_scale
    qkv = jnp.concatenate([q, proj[:, pool_width + attn_width:]], axis=1).astype(jnp.bfloat16)
    for j in range(qkv_ref.shape[0]):
        qkv_ref[j] = qkv[:, j * LANES_V7X:(j + 1) * LANES_V7X]


def _in_proj(x2, gain, w_in_bf16, *, pool_width, attn_width, tm):
    n, d = x2.shape
    in_width = w_in_bf16.shape[1]
    n_slabs = 3 * attn_width // LANES_V7X
    kern = functools.partial(_in_proj_kernel, pool_width=pool_width, attn_width=attn_width,
                             q_scale=1.0 / math.sqrt(SB_HEAD_DIM))
    return pl.pallas_call(
        kern,
        grid=(n // tm,),
        in_specs=[
            pl.BlockSpec((tm, d), lambda i: (i, 0)),
            pl.BlockSpec((1, d), lambda i: (0, 0)),
            pl.BlockSpec((d, in_width), lambda i: (0, 0)),
        ],
        out_specs=[
            pl.BlockSpec((tm, pool_width), lambda i: (i, 0)),
            pl.BlockSpec((n_slabs, tm, LANES_V7X), lambda i: (0, i, 0)),
        ],
        out_shape=[
            jax.ShapeDtypeStruct((n, pool_width), jnp.float32),
            jax.ShapeDtypeStruct((n_slabs, n, LANES_V7X), jnp.bfloat16),
        ],
        compiler_params=pltpu.CompilerParams(
            dimension_semantics=("parallel",), vmem_limit_bytes=VMEM_LIMIT_BYTES),
        name="in_proj",
    )(x2, gain, w_in_bf16)


def _sb_tile(qh, kb, vb, tri, c, tail_mask):
    z = _dot_nt(qh, kb)
    if tail_mask is not None:
        m = tail_mask.shape[1]
        tail = jnp.where(tail_mask, z[:, -m:], MASKED_LOGIT)
        z = tail if m == z.shape[1] else jnp.concatenate([z[:, :-m], tail], axis=1)
    sp = (jnp.maximum(z, 0.0) + jnp.log(1.0 + jnp.exp2(jnp.abs(z) * (-LOG2E)))).astype(jnp.bfloat16)
    n = tri.shape[0]
    chunks, total = [], None
    for k0 in range(z.shape[1] - n, -1, -n):
        cum_k = _dot(sp[:, k0:k0 + n], tri)
        if total is not None:
            cum_k = cum_k + total
        total = cum_k[:, :1]
        chunks.insert(0, cum_k)
    cum = chunks[0] if len(chunks) == 1 else jnp.concatenate(chunks, axis=1)
    w = jnp.exp(z - cum) if c is None else jnp.exp(jnp.minimum(z - cum, 0.0) - c)
    return (total if c is None else c + total), _dot(w.astype(jnp.bfloat16), vb)


def _sb_attn_kernel(q_ref, k_ref, v_ref, tri_ref, o_ref, *, tq, sub, win):
    qi = pl.program_id(2)
    n_sub = tq // sub
    lane = lax.broadcasted_iota(jnp.int32, (tq, LANES_V7X), 1)
    head0 = lane < SB_HEAD_DIM
    q2 = q_ref[...]
    zero = jnp.zeros_like(q2)
    q_heads = (jnp.where(head0, q2, zero), jnp.where(head0, zero, q2))
    q_st = jnp.concatenate(q_heads, axis=0)

    def kv_block(j):
        start = pl.multiple_of(j * tq, tq)
        return k_ref[pl.ds(start, tq), :], v_ref[pl.ds(start, tq), :]

    def stacked_iota(rows, cols):
        row = lax.broadcasted_iota(jnp.int32, (2 * rows, cols), 0)
        col = lax.broadcasted_iota(jnp.int32, (2 * rows, cols), 1)
        return jnp.where(row >= rows, row - rows, row), col

    def first_block():
        row, col = stacked_iota(tq, tq)
        kb, vb = kv_block(0)
        return _sb_tile(q_st, kb, vb, tri_ref[...], None, col < row)

    def windows():
        row, col = stacked_iota(sub, sub)
        causal = col < row
        tri = tri_ref[...]
        parts = []
        for hf in range(n_sub):
            start = pl.multiple_of((qi * n_sub + hf + 1) * sub - win, sub)
            kw = k_ref[pl.ds(start, win), :]
            vw = v_ref[pl.ds(start, win), :]
            q_sub = jnp.concatenate([qh[hf * sub:(hf + 1) * sub] for qh in q_heads], axis=0)
            parts.append(_sb_tile(q_sub, kw, vw, tri, None, causal))
        return tuple(jnp.concatenate([p[i][h * sub:(h + 1) * sub] for h in range(2) for p in parts], axis=0)
                     for i in range(2))

    carry = lax.cond(qi == 0, first_block, windows)

    def live(carry):
        return jnp.min(carry[0], axis=0, keepdims=True)[0, 0] < UNDERFLOW_EXPONENT

    def add_block(carry, j, tail_mask):
        kb, vb = kv_block(j)
        c, acc = _sb_tile(q_st, kb, vb, tri_ref[...], carry[0], tail_mask)
        return c, carry[1] + acc

    def partial_block(back):
        def fn(carry):
            row, col = stacked_iota(tq, tq)
            fresh = col < (row // sub + 1) * sub - win + back * tq
            return add_block(carry, qi - back, fresh)
        return fn

    for back in ((0, 1) if win < tq else (1,)):
        carry = lax.cond(jnp.logical_and(qi > 0, live(carry)), partial_block(back), lambda cr: cr, carry)

    def cond(state):
        step, alive, _ = state
        return jnp.logical_and(step < qi - 1, alive > 0)

    def body(state):
        step, _, carry = state
        carry = add_block(carry, qi - 2 - step, None)
        return step + 1, live(carry).astype(jnp.int32), carry

    _, _, carry = lax.while_loop(cond, body, (jnp.int32(0), live(carry).astype(jnp.int32), carry))
    acc = carry[1]
    o_ref[...] = jnp.where(head0, acc[:tq], acc[tq:]).astype(o_ref.dtype)


def _sb_attention(qkv, *, tq, sub, win):
    n_slabs, b, s, _ = qkv.shape
    assert tq % sub == 0 and win % sub == 0 and sub <= win <= tq + sub, (tq, sub, win)
    n_pairs = n_slabs // 3
    n = math.gcd(tq, win)
    tri = (lax.broadcasted_iota(jnp.int32, (n, n), 0)
           >= lax.broadcasted_iota(jnp.int32, (n, n), 1)).astype(jnp.bfloat16)

    kern = functools.partial(_sb_attn_kernel, tq=tq, sub=sub, win=win)
    const2 = lambda bi, hp, qi: (0, 0)
    return pl.pallas_call(
        kern,
        grid=(b, n_pairs, s // tq),
        in_specs=[
            pl.BlockSpec((None, None, tq, LANES_V7X), lambda bi, hp, qi: (hp, bi, qi, 0)),
            pl.BlockSpec((None, None, s, LANES_V7X), lambda bi, hp, qi: (n_pairs + hp, bi, 0, 0)),
            pl.BlockSpec((None, None, s, LANES_V7X), lambda bi, hp, qi: (2 * n_pairs + hp, bi, 0, 0)),
            pl.BlockSpec((n, n), const2),
        ],
        out_specs=pl.BlockSpec((None, None, tq, LANES_V7X), lambda bi, hp, qi: (hp, bi, qi, 0)),
        out_shape=jax.ShapeDtypeStruct((n_pairs, b, s, LANES_V7X), jnp.bfloat16),
        compiler_params=pltpu.CompilerParams(
            dimension_semantics=("parallel", "parallel", "parallel"),
            vmem_limit_bytes=VMEM_LIMIT_BYTES),
        name="sb_attention",
    )(qkv, qkv, qkv, tri)


def _mem_kv_kernel(m_ref, g_ref, wk_ref, wv_ref, k_ref, v_ref):
    mn = _rms(m_ref[...], g_ref[...]).astype(jnp.bfloat16)
    k_ref[...] = _dot(mn, wk_ref[...]).astype(jnp.bfloat16)
    v_ref[...] = _dot(mn, wv_ref[...]).astype(jnp.bfloat16)


def _mem_kv(mem2, gain, wk, wv, *, tm):
    n, d = mem2.shape
    row = pl.BlockSpec((tm, d), lambda i: (i, 0))
    full = pl.BlockSpec((d, d), lambda i: (0, 0))
    return pl.pallas_call(
        _mem_kv_kernel,
        grid=(n // tm,),
        in_specs=[row, pl.BlockSpec((1, d), lambda i: (0, 0)), full, full],
        out_specs=[row, row],
        out_shape=[jax.ShapeDtypeStruct((n, d), jnp.bfloat16)] * 2,
        compiler_params=pltpu.CompilerParams(
            dimension_semantics=("parallel",), vmem_limit_bytes=VMEM_LIMIT_BYTES),
        name="mem_kv",
    )(mem2, gain, wk, wv)


def _route_t(logits):
    neg = jnp.float32(-jnp.inf)
    big = jnp.float32(ROUTER_ROWS)
    row = lax.broadcasted_iota(jnp.int32, logits.shape, 0).astype(jnp.float32)

    def col_max(mask):
        return jnp.max(jnp.where(mask, logits, neg), axis=0, keepdims=True)

    def first_argmax(mask, mx):
        return jnp.min(jnp.where(mask & (logits == mx), row, big), axis=0, keepdims=True)

    gmask = row < N_GROUPS
    gmax = col_max(gmask)
    gsum = jnp.sum(jnp.where(gmask, jnp.exp(logits - gmax), 0.0), axis=0, keepdims=True)
    g_gate = 1.0 / gsum
    g_idx = first_argmax(gmask, gmax)

    lo = N_GROUPS + EXPERTS_PER_GROUP * g_idx
    emask = (row >= lo) & (row < lo + EXPERTS_PER_GROUP)
    m1 = col_max(emask)
    i1 = first_argmax(emask, m1)
    mask2 = emask & (row != i1)
    m2 = col_max(mask2)
    i2 = first_argmax(mask2, m2)
    esum = jnp.sum(jnp.where(emask, jnp.exp(logits - m1), 0.0), axis=0, keepdims=True)
    p1 = 1.0 / esum
    p2 = jnp.exp(m2 - m1) / esum
    tot = p1 + p2
    w1 = g_gate * (p1 / tot)
    w2 = g_gate * (p2 / tot)

    first = i1 < i2
    la = jnp.where(first, i1, i2) - lo
    lb = jnp.where(first, i2, i1) - lo
    pair = sum(jnp.where((la == a) & (lb == b), float(p), 0.0) for p, (a, b) in enumerate(PAIRS))
    bucket = g_idx * len(PAIRS) + pair
    return bucket, jnp.where(first, w1, w2), jnp.where(first, w2, w1)


def _mix_cross_kernel(x_ref, u_ref, halo_ref, a_ref, wpbd_ref, ps_ref, wo_ref, gc_ref,
                      wq_ref, km_ref, vm_ref, wom_ref, gf_ref, wr_ref, su_ref,
                      h_ref, xrow_ref, bucket_ref, rank_ref, cnt_ref, run_ref, *, tm, tiles_per_seq):
    i = pl.program_id(0)
    pool_width = u_ref.shape[1]
    gw = pool_width // len(POOL_WINDOWS)
    tile_in_seq = i % tiles_per_seq
    first = tile_in_seq == 0
    pos = tile_in_seq * tm + lax.broadcasted_iota(jnp.int32, (tm, 1), 0)

    halo = jnp.where(first, 0.0, halo_ref[...])
    u = u_ref[...]
    pooled = []
    for g, w in enumerate(POOL_WINDOWS):
        ug = u[:, g * gw:(g + 1) * gw]
        ext = jnp.concatenate([halo[:, g * gw:(g + 1) * gw], ug], axis=0)
        shift = 1
        while shift < w:
            ext = ext + pltpu.roll(ext, shift, 0)
            shift *= 2
        win = ext[POOL_HALO:, :]
        inv_count = 1.0 / jnp.minimum(pos + 1, w).astype(jnp.float32)
        pooled.append((win * inv_count - ug).astype(jnp.bfloat16))

    pool_out = _dot(jnp.concatenate(pooled, axis=1), wpbd_ref[...]) * ps_ref[...]
    mixed = jnp.concatenate([pool_out.astype(jnp.bfloat16)] + [a_ref[j] for j in range(a_ref.shape[0])], axis=1)
    h1 = x_ref[...] + _dot(mixed, wo_ref[...])

    hn = _rms(h1, gc_ref[...]).astype(jnp.bfloat16)
    d = h1.shape[1]
    hd = d // MEM_HEADS
    q = (_dot(hn, wq_ref[...]) * (1.0 / math.sqrt(hd))).astype(jnp.bfloat16)
    outs = []
    for hh in range(MEM_HEADS):
        sl = slice(hh * hd, (hh + 1) * hd)
        s = _dot_nt(q[:, sl], km_ref[:, sl])
        e = jnp.exp(s - jnp.max(s, axis=-1, keepdims=True))
        p = e * (1.0 / jnp.sum(e, axis=-1, keepdims=True))
        outs.append(_dot(p.astype(jnp.bfloat16), vm_ref[:, sl]))
    o = jnp.concatenate(outs, axis=-1).astype(jnp.bfloat16)
    h2 = h1 + _dot(o, wom_ref[...])
    h_ref[...] = h2

    xn = _rms(h2, gf_ref[...])
    xrow_ref[:, :d] = xn
    x_hi = xn.astype(jnp.bfloat16)
    x_lo = (xn - x_hi.astype(jnp.float32)).astype(jnp.bfloat16)
    both = _dot(x_hi, wr_ref[...])
    logits_tm = both[:, :ROUTER_LANES] + both[:, ROUTER_LANES:] + _dot(x_lo, wr_ref[:, :ROUTER_LANES])
    logits = logits_tm.T[:ROUTER_ROWS]
    bucket, w_a, w_b = _route_t(logits)

    @pl.when(i == 0)
    def _():
        run_ref[...] = jnp.zeros_like(run_ref)

    brow = lax.broadcasted_iota(jnp.int32, (ROUTER_ROWS, tm), 0).astype(jnp.float32)
    onehot = (brow == bucket).astype(jnp.float32)
    before = _dot(onehot.astype(jnp.bfloat16), su_ref[...])
    run = run_ref[...]
    rank = jnp.sum(onehot * (before + run[:, :1]), axis=0, keepdims=True)
    run = run + jnp.sum(onehot, axis=1, keepdims=True)
    run_ref[...] = run
    cnt_ref[...] = run
    bucket_ref[...] = bucket.astype(jnp.int32)
    rank_ref[...] = rank.astype(jnp.int32)

    prow = lax.broadcasted_iota(jnp.int32, (PAYLOAD_LANES, tm), 0)
    payload_t = jnp.where(prow == 0, w_a, jnp.where(prow == 1, w_b, 0.0))
    xrow_ref[:, d:] = payload_t.T


def _mix_cross(x2, u, attn, w_pool_bd, pool_scale, w_out, g_cross, wq, kmem, vmem, wom, g_ffn, w_router,
               *, tm, seq, mem_len):
    n, d = x2.shape
    pool_width = u.shape[1]
    tiles_per_seq = seq // tm
    n_tiles = n // tm
    halo_blocks = tm // POOL_HALO
    su = (lax.broadcasted_iota(jnp.int32, (tm, tm), 0)
          < lax.broadcasted_iota(jnp.int32, (tm, tm), 1)).astype(jnp.bfloat16)
    kern = functools.partial(_mix_cross_kernel, tm=tm, tiles_per_seq=tiles_per_seq)
    const2 = lambda i: (0, 0)
    mem_map = lambda i: (i // tiles_per_seq, 0)
    return pl.pallas_call(
        kern,
        grid=(n_tiles,),
        in_specs=[
            pl.BlockSpec((tm, d), lambda i: (i, 0)),
            pl.BlockSpec((tm, pool_width), lambda i: (i, 0)),
            pl.BlockSpec((POOL_HALO, pool_width), lambda i: (jnp.maximum(i * halo_blocks - 1, 0), 0)),
            pl.BlockSpec((attn.shape[0], tm, LANES_V7X), lambda i: (0, i, 0)),
            pl.BlockSpec(w_pool_bd.shape, const2),
            pl.BlockSpec((1, pool_width), const2),
            pl.BlockSpec(w_out.shape, const2),
            pl.BlockSpec((1, d), const2),
            pl.BlockSpec((d, d), const2),
            pl.BlockSpec((mem_len, d), mem_map),
            pl.BlockSpec((mem_len, d), mem_map),
            pl.BlockSpec((d, d), const2),
            pl.BlockSpec((1, d), const2),
            pl.BlockSpec((d, 2 * ROUTER_LANES), const2),
            pl.BlockSpec((tm, tm), const2),
        ],
        out_specs=[
            pl.BlockSpec((tm, d), lambda i: (i, 0)),
            pl.BlockSpec((tm, d + PAYLOAD_LANES), lambda i: (i, 0)),
            pl.BlockSpec((None, 1, tm), lambda i: (i, 0, 0)),
            pl.BlockSpec((None, 1, tm), lambda i: (i, 0, 0)),
            pl.BlockSpec((ROUTER_ROWS, LANES_V7X), const2),
        ],
        out_shape=[
            jax.ShapeDtypeStruct((n, d), jnp.float32),
            jax.ShapeDtypeStruct((n, d + PAYLOAD_LANES), jnp.float32),
            jax.ShapeDtypeStruct((n_tiles, 1, tm), jnp.int32),
            jax.ShapeDtypeStruct((n_tiles, 1, tm), jnp.int32),
            jax.ShapeDtypeStruct((ROUTER_ROWS, LANES_V7X), jnp.float32),
        ],
        scratch_shapes=[pltpu.VMEM((ROUTER_ROWS, LANES_V7X), jnp.float32)],
        compiler_params=pltpu.CompilerParams(
            dimension_semantics=("arbitrary",), vmem_limit_bytes=VMEM_LIMIT_BYTES),
        name="mix_cross",
    )(x2, u, u, attn, w_pool_bd, pool_scale, w_out, g_cross, wq, kmem, vmem, wom, g_ffn, w_router, su)


def _dispatch_kernel(pos_ref, last_ref, n_last_ref, na_ref, x_ref, xs_ref, stage, zeros, sems, fill_sem,
                     *, tm, n_tiles, n_tiles_e):
    i = pl.program_id(0)
    slot = i % 2
    base = i * tm
    tm_e = zeros.shape[0]

    def fill_copy(j):
        return pltpu.make_async_copy(zeros, xs_ref.at[pl.ds(pl.multiple_of(j * tm_e, tm_e), tm_e), :], fill_sem)

    def for_range(lo, hi, fn):
        def body(k, carry):
            fn(k)
            return carry
        lax.fori_loop(lo, hi, body, 0)

    @pl.when(i == 0)
    def _():
        zeros[...] = jnp.zeros_like(zeros)
        for_range(0, n_last_ref[0], lambda k: fill_copy(last_ref[k]).start())
        for_range(na_ref[0], n_tiles_e, lambda j: fill_copy(j).start())
        for_range(0, n_last_ref[0], lambda k: fill_copy(0).wait())
        for_range(na_ref[0], n_tiles_e, lambda j: fill_copy(0).wait())

    def row_copy(slot, r, p):
        return pltpu.make_async_copy(stage.at[slot, pl.ds(r, 1), :], xs_ref.at[pl.ds(p, 1), :], sems.at[slot])

    def wait_slot(slot):
        pltpu.make_async_copy(stage.at[slot], xs_ref.at[pl.ds(0, tm), :], sems.at[slot]).wait()

    stage[slot] = x_ref[...]
    for r in range(tm):
        row_copy(slot, r, pos_ref[base + r]).start(priority=r % 2)

    @pl.when(i > 0)
    def _():
        wait_slot(1 - slot)

    @pl.when(i == n_tiles - 1)
    def _():
        wait_slot(slot)


def _dispatch(pos, last_tiles, n_last, n_active, xrow, *, tm, tm_e, n_tiles_e):
    n, width = xrow.shape
    return pl.pallas_call(
        functools.partial(_dispatch_kernel, tm=tm, n_tiles=n // tm, n_tiles_e=n_tiles_e),
        grid_spec=pltpu.PrefetchScalarGridSpec(
            num_scalar_prefetch=4,
            grid=(n // tm,),
            in_specs=[pl.BlockSpec((tm, width), lambda i, *_: (i, 0))],
            out_specs=pl.BlockSpec(memory_space=pl.ANY),
            scratch_shapes=[pltpu.VMEM((2, tm, width), xrow.dtype), pltpu.VMEM((tm_e, width), xrow.dtype),
                            pltpu.SemaphoreType.DMA((2,)), pltpu.SemaphoreType.DMA],
        ),
        out_shape=jax.ShapeDtypeStruct((n_tiles_e * tm_e, width), xrow.dtype),
        compiler_params=pltpu.CompilerParams(
            dimension_semantics=("arbitrary",), vmem_limit_bytes=VMEM_LIMIT_BYTES),
        name="moe_dispatch",
    )(pos, last_tiles, n_last, n_active, xrow)


def _expert_kernel(ea_ref, eb_ref, na_ref, xs_ref, wgua_ref, wda_ref, wgub_ref, wdb_ref, y_ref, *, d):
    del ea_ref, eb_ref
    active = pl.program_id(0) < na_ref[0]

    @pl.when(jnp.logical_not(active))
    def _():
        y_ref[...] = jnp.zeros_like(y_ref)

    @pl.when(active)
    def _():
        x = xs_ref[:, :d].astype(jnp.bfloat16)

        def mlp(wgu_ref, wd_ref):
            gu = _dot(x, wgu_ref[...])
            ff = gu.shape[1] // 2
            gate, up = gu[:, :ff], gu[:, ff:]
            hmid = (gate * (1.0 / (1.0 + jnp.exp(-gate)))) * up
            return _dot(hmid.astype(jnp.bfloat16), wd_ref[...])

        y_ref[...] = (xs_ref[:, d:d + 1] * mlp(wgua_ref, wda_ref)
                      + xs_ref[:, d + 1:d + 2] * mlp(wgub_ref, wdb_ref))


def _experts(tile_ea, tile_eb, n_active, xs, wgu, wd, *, tm):
    n_pad, width = xs.shape
    d, ff = wd.shape[2], wd.shape[1]
    row_map = lambda j, ea, eb, na: (jnp.minimum(j, na[0] - 1), 0)
    a_map = lambda j, ea, eb, na: (ea[j], 0, 0)
    b_map = lambda j, ea, eb, na: (eb[j], 0, 0)
    return pl.pallas_call(
        functools.partial(_expert_kernel, d=d),
        grid_spec=pltpu.PrefetchScalarGridSpec(
            num_scalar_prefetch=3,
            grid=(n_pad // tm,),
            in_specs=[
                pl.BlockSpec((tm, width), row_map),
                pl.BlockSpec((None, d, 2 * ff), a_map),
                pl.BlockSpec((None, ff, d), a_map),
                pl.BlockSpec((None, d, 2 * ff), b_map),
                pl.BlockSpec((None, ff, d), b_map),
            ],
            out_specs=pl.BlockSpec((tm, d), lambda j, ea, eb, na: (j, 0)),
        ),
        out_shape=jax.ShapeDtypeStruct((n_pad, d), jnp.float32),
        compiler_params=pltpu.CompilerParams(
            dimension_semantics=("arbitrary",), vmem_limit_bytes=VMEM_LIMIT_BYTES),
        name="moe_experts",
    )(tile_ea, tile_eb, n_active, xs, wgu, wd, wgu, wd)


def _combine_kernel(pos_ref, h_ref, y_ref, g_ref, o_ref, ybuf, sems, *, tm, n_tiles):
    i = pl.program_id(0)

    def row_copy(slot, r, p):
        return pltpu.make_async_copy(y_ref.at[pl.ds(p, 1), :], ybuf.at[slot, pl.ds(r, 1), :], sems.at[slot])

    @pl.when(i < n_tiles)
    def _():
        slot = i % 2
        for r in range(tm):
            row_copy(slot, r, pos_ref[i * tm + r]).start(priority=r % 2)

    @pl.when(i > 0)
    def _():
        slot = (i - 1) % 2
        pltpu.make_async_copy(y_ref.at[pl.ds(0, tm), :], ybuf.at[slot], sems.at[slot]).wait()
        o_ref[...] = _rms(h_ref[...] + ybuf[slot], g_ref[...])


def _combine(pos, h, y_sorted, g_final, *, tm):
    n, d = h.shape
    n_tiles = n // tm
    prev_tile = lambda i, pos: (jnp.maximum(i - 1, 0), 0)
    return pl.pallas_call(
        functools.partial(_combine_kernel, tm=tm, n_tiles=n_tiles),
        grid_spec=pltpu.PrefetchScalarGridSpec(
            num_scalar_prefetch=1,
            grid=(n_tiles + 1,),
            in_specs=[
                pl.BlockSpec((tm, d), prev_tile),
                pl.BlockSpec(memory_space=pl.ANY),
                pl.BlockSpec((1, d), lambda i, pos: (0, 0)),
            ],
            out_specs=pl.BlockSpec((tm, d), prev_tile),
            scratch_shapes=[pltpu.VMEM((2, tm, d), jnp.float32), pltpu.SemaphoreType.DMA((2,))],
        ),
        out_shape=jax.ShapeDtypeStruct((n, d), jnp.float32),
        compiler_params=pltpu.CompilerParams(
            dimension_semantics=("arbitrary",), vmem_limit_bytes=VMEM_LIMIT_BYTES),
        name="moe_combine",
    )(pos, h, y_sorted, g_final)


def _tile(n, pref):
    t = min(n, pref)
    assert n % t == 0, (n, t)
    return t


def _sorted_layout(bucket, rank, counts, *, tm_e, n_tiles_e):
    seg_tiles = (counts + tm_e - 1) // tm_e
    seg_end = jnp.cumsum(seg_tiles)
    seg_start = seg_end - seg_tiles
    pos = (seg_start * tm_e)[bucket] + rank
    nonempty = seg_tiles > 0
    last_tiles = jnp.sort(jnp.where(nonempty, seg_end - 1, n_tiles_e))
    n_active = seg_end[-1]
    tile = jnp.minimum(jnp.arange(n_tiles_e, dtype=jnp.int32), n_active - 1)
    tile_bucket = jnp.sum((tile[:, None] >= seg_end[None, :]).astype(jnp.int32), axis=1)
    group, pair = tile_bucket // len(PAIRS), tile_bucket % len(PAIRS)
    pair_a = jnp.array([p[0] for p in PAIRS], jnp.int32)
    pair_b = jnp.array([p[1] for p in PAIRS], jnp.int32)
    tile_ea = group * EXPERTS_PER_GROUP + pair_a[pair]
    tile_eb = group * EXPERTS_PER_GROUP + pair_b[pair]
    i32 = lambda a: a.astype(jnp.int32)
    return i32(pos), i32(last_tiles), i32(jnp.sum(nonempty)), i32(tile_ea), i32(tile_eb), i32(n_active)


def kernel(x, mem, norm_mix, w_in, w_pool, pool_scale, w_out, norm_cross, norm_mem, w_q_mem, w_k_mem,
           w_v_mem, w_o_mem, norm_ffn, w_group, w_expert, w_gate, w_up, w_down, norm_final):
    b, s, d = x.shape
    mem_len = mem.shape[1]
    depth = norm_mix.shape[0]
    pool_width = pool_scale.shape[1]
    attn_width = w_out.shape[1] - pool_width
    bf = jnp.bfloat16
    n = b * s

    assert depth == 1, "single-layer problem: the final RMSNorm is fused into the combine kernel"
    l = 0
    h = x.reshape(n, d)
    u, qkv = _in_proj(h, norm_mix[l][None], w_in[l].astype(bf),
                      pool_width=pool_width, attn_width=attn_width, tm=_tile(n, 1024))
    tq = _tile(s, 2 * MXU_DIM_V7X)
    sub = min(tq, MXU_DIM_V7X)
    attn = _sb_attention(qkv.reshape(-1, b, s, LANES_V7X), tq=tq, sub=sub, win=2 * sub)
    kmem, vmem = _mem_kv(mem.reshape(b * mem_len, d), norm_mem[l][None],
                         w_k_mem[l].astype(bf), w_v_mem[l].astype(bf), tm=mem_len)

    w_router = jnp.concatenate([w_group[l], w_expert[l]], axis=1)
    w_router = jnp.pad(w_router, ((0, 0), (0, ROUTER_LANES - w_router.shape[1])))
    wr_hi = w_router.astype(bf)
    wr_lo = (w_router - wr_hi.astype(jnp.float32)).astype(bf)
    w_pool_bd = jax.scipy.linalg.block_diag(*w_pool[l]).astype(bf)
    h, xrow, bucket, rank, counts = _mix_cross(
        h, u, attn.reshape(-1, n, LANES_V7X), w_pool_bd, pool_scale[l][None], w_out[l].astype(bf),
        norm_cross[l][None], w_q_mem[l].astype(bf), kmem, vmem, w_o_mem[l].astype(bf),
        norm_ffn[l][None], jnp.concatenate([wr_hi, wr_lo], axis=1), tm=_tile(s, 1024), seq=s, mem_len=mem_len)

    tm_e = _tile(n, 2 * MXU_DIM_V7X)
    n_tiles_e = -(-(n + N_BUCKETS * (tm_e - 1)) // tm_e)
    pos, last_tiles, n_last, tile_ea, tile_eb, n_active = _sorted_layout(
        bucket.reshape(n), rank.reshape(n), counts[:N_BUCKETS, 0].astype(jnp.int32),
        tm_e=tm_e, n_tiles_e=n_tiles_e)
    xs = _dispatch(pos, last_tiles, n_last[None], n_active[None], xrow, tm=_tile(n, 1024), tm_e=tm_e,
                   n_tiles_e=n_tiles_e)
    w_gate_up = jnp.concatenate([w_gate[l], w_up[l]], axis=-1).astype(bf)
    y_sorted = _experts(tile_ea, tile_eb, n_active[None], xs, w_gate_up, w_down[l].astype(bf), tm=tm_e)
    out = _combine(pos, h, y_sorted, norm_final[None], tm=_tile(n, 512))
    return out.reshape(b, s, d)
```

```python
import functools
import math

import jax
import jax.numpy as jnp
from jax import lax
from jax.experimental import pallas as pl
from jax.experimental.pallas import tpu as pltpu

RMS_EPS = 1e-6
POOL_WINDOWS = (2, 4, 8, 16)
SB_HEAD_DIM = 64
MEM_HEADS = 4
N_GROUPS = 4
EXPERTS_PER_GROUP = 4
N_EXPERTS = N_GROUPS * EXPERTS_PER_GROUP

LANES_V7X = 128
MXU_DIM_V7X = 256
VMEM_LIMIT_BYTES = 56 * 1024 * 1024

LOG2E = 1.4426950408889634
UNDERFLOW_EXPONENT = 104.0
MASKED_LOGIT = -1e30
POOL_HALO = 16

PAIRS = ((0, 1), (0, 2), (1, 2), (1, 3), (2, 3), (0, 3))
assert sorted(PAIRS) == [(a, b) for a in range(EXPERTS_PER_GROUP) for b in range(a + 1, EXPERTS_PER_GROUP)]
N_BUCKETS = N_GROUPS * len(PAIRS)
ROUTER_ROWS = 32
ROUTER_LANES = LANES_V7X
PAYLOAD_LANES = LANES_V7X
assert N_GROUPS + N_EXPERTS <= ROUTER_ROWS and N_BUCKETS <= ROUTER_ROWS


def _rms(x, gain):
    ms = jnp.mean(x * x, axis=-1, keepdims=True)
    return x * lax.rsqrt(ms + RMS_EPS) * gain


def _dot(a, b):
    return jnp.dot(a, b, preferred_element_type=jnp.float32)


def _dot_nt(a, b):
    return lax.dot_general(a, b, (((1,), (1,)), ((), ())), preferred_element_type=jnp.float32)


def _in_proj_kernel(x_ref, g_ref, w_ref, u_ref, qkv_ref, *, pool_width, attn_width, q_scale):
    xn = _rms(x_ref[...], g_ref[...]).astype(jnp.bfloat16)
    proj = _dot(xn, w_ref[...])
    u_ref[...] = proj[:, :pool_width]
    q = proj[:, pool_width:pool_width + attn_width] * q_scale
    qkv = jnp.concatenate([q, proj[:, pool_width + attn_width:]], axis=1).astype(jnp.bfloat16)
    for j in range(qkv_ref.shape[0]):
        qkv_ref[j] = qkv[:, j * LANES_V7X:(j + 1) * LANES_V7X]


def _in_proj(x2, gain, w_in_bf16, *, pool_width, attn_width, tm):
    n, d = x2.shape
    in_width = w_in_bf16.shape[1]
    n_slabs = 3 * attn_width // LANES_V7X
    kern = functools.partial(_in_proj_kernel, pool_width=pool_width, attn_width=attn_width,
                             q_scale=1.0 / math.sqrt(SB_HEAD_DIM))
    return pl.pallas_call(
        kern,
        grid=(n // tm,),
        in_specs=[
            pl.BlockSpec((tm, d), lambda i: (i, 0)),
            pl.BlockSpec((1, d), lambda i: (0, 0)),
            pl.BlockSpec((d, in_width), lambda i: (0, 0)),
        ],
        out_specs=[
            pl.BlockSpec((tm, pool_width), lambda i: (i, 0)),
            pl.BlockSpec((n_slabs, tm, LANES_V7X), lambda i: (0, i, 0)),
        ],
        out_shape=[
            jax.ShapeDtypeStruct((n, pool_width), jnp.float32),
            jax.ShapeDtypeStruct((n_slabs, n, LANES_V7X), jnp.bfloat16),
        ],
        compiler_params=pltpu.CompilerParams(
            dimension_semantics=("parallel",), vmem_limit_bytes=VMEM_LIMIT_BYTES),
        name="in_proj",
    )(x2, gain, w_in_bf16)


def _sb_tile(qh, kb, vb, tri, c, tail_mask):
    z = _dot_nt(qh, kb)
    if tail_mask is not None:
        m = tail_mask.shape[1]
        tail = jnp.where(tail_mask, z[:, -m:], MASKED_LOGIT)
        z = tail if m == z.shape[1] else jnp.concatenate([z[:, :-m], tail], axis=1)
    sp = (jnp.maximum(z, 0.0) + jnp.log(1.0 + jnp.exp2(jnp.abs(z) * (-LOG2E)))).astype(jnp.bfloat16)
    n = tri.shape[0]
    chunks, total = [], None
    for k0 in range(z.shape[1] - n, -1, -n):
        cum_k = _dot(sp[:, k0:k0 + n], tri)
        if total is not None:
            cum_k = cum_k + total
        total = cum_k[:, :1]
        chunks.insert(0, cum_k)
    cum = chunks[0] if len(chunks) == 1 else jnp.concatenate(chunks, axis=1)
    w = jnp.exp(z - cum) if c is None else jnp.exp(jnp.minimum(z - cum, 0.0) - c)
    return (total if c is None else c + total), _dot(w.astype(jnp.bfloat16), vb)


def _sb_attn_kernel(q_ref, k_ref, v_ref, tri_ref, o_ref, *, tq, sub, win):
    qi = pl.program_id(2)
    n_sub = tq // sub
    lane = lax.broadcasted_iota(jnp.int32, (tq, LANES_V7X), 1)
    head0 = lane < SB_HEAD_DIM
    q2 = q_ref[...]
    zero = jnp.zeros_like(q2)
    q_heads = (jnp.where(head0, q2, zero), jnp.where(head0, zero, q2))
    q_st = jnp.concatenate(q_heads, axis=0)

    def kv_block(j):
        start = pl.multiple_of(j * tq, tq)
        return k_ref[pl.ds(start, tq), :], v_ref[pl.ds(start, tq), :]

    def stacked_iota(rows, cols):
        row = lax.broadcasted_iota(jnp.int32, (2 * rows, cols), 0)
        col = lax.broadcasted_iota(jnp.int32, (2 * rows, cols), 1)
        return jnp.where(row >= rows, row - rows, row), col

    def first_block():
        row, col = stacked_iota(tq, tq)
        kb, vb = kv_block(0)
        return _sb_tile(q_st, kb, vb, tri_ref[...], None, col < row)

    def windows():
        row, col = stacked_iota(sub, sub)
        causal = col < row
        tri = tri_ref[...]
        parts = []
        for hf in range(n_sub):
            start = pl.multiple_of((qi * n_sub + hf + 1) * sub - win, sub)
            kw = k_ref[pl.ds(start, win), :]
            vw = v_ref[pl.ds(start, win), :]
            q_sub = jnp.concatenate([qh[hf * sub:(hf + 1) * sub] for qh in q_heads], axis=0)
            parts.append(_sb_tile(q_sub, kw, vw, tri, None, causal))
        return tuple(jnp.concatenate([p[i][h * sub:(h + 1) * sub] for h in range(2) for p in parts], axis=0)
                     for i in range(2))

    carry = lax.cond(qi == 0, first_block, windows)

    def live(carry):
        return jnp.min(carry[0], axis=0, keepdims=True)[0, 0] < UNDERFLOW_EXPONENT

    def add_block(carry, j, tail_mask):
        kb, vb = kv_block(j)
        c, acc = _sb_tile(q_st, kb, vb, tri_ref[...], carry[0], tail_mask)
        return c, carry[1] + acc

    def partial_block(back):
        def fn(carry):
            row, col = stacked_iota(tq, tq)
            fresh = col < (row // sub + 1) * sub - win + back * tq
            return add_block(carry, qi - back, fresh)
        return fn

    for back in ((0, 1) if win < tq else (1,)):
        carry = lax.cond(jnp.logical_and(qi > 0, live(carry)), partial_block(back), lambda cr: cr, carry)

    def cond(state):
        step, alive, _ = state
        return jnp.logical_and(step < qi - 1, alive > 0)

    def body(state):
        step, _, carry = state
        carry = add_block(carry, qi - 2 - step, None)
        return step + 1, live(carry).astype(jnp.int32), carry

    _, _, carry = lax.while_loop(cond, body, (jnp.int32(0), live(carry).astype(jnp.int32), carry))
    acc = carry[1]
    o_ref[...] = jnp.where(head0, acc[:tq], acc[tq:]).astype(o_ref.dtype)


def _sb_attention(qkv, *, tq, sub, win):
    n_slabs, b, s, _ = qkv.shape
    assert tq % sub == 0 and win % sub == 0 and sub <= win <= tq + sub, (tq, sub, win)
    n_pairs = n_slabs // 3
    n = math.gcd(tq, win)
    tri = (lax.broadcasted_iota(jnp.int32, (n, n), 0)
           >= lax.broadcasted_iota(jnp.int32, (n, n), 1)).astype(jnp.bfloat16)

    kern = functools.partial(_sb_attn_kernel, tq=tq, sub=sub, win=win)
    const2 = lambda bi, hp, qi: (0, 0)
    return pl.pallas_call(
        kern,
        grid=(b, n_pairs, s // tq),
        in_specs=[
            pl.BlockSpec((None, None, tq, LANES_V7X), lambda bi, hp, qi: (hp, bi, qi, 0)),
            pl.BlockSpec((None, None, s, LANES_V7X), lambda bi, hp, qi: (n_pairs + hp, bi, 0, 0)),
            pl.BlockSpec((None, None, s, LANES_V7X), lambda bi, hp, qi: (2 * n_pairs + hp, bi, 0, 0)),
            pl.BlockSpec((n, n), const2),
        ],
        out_specs=pl.BlockSpec((None, None, tq, LANES_V7X), lambda bi, hp, qi: (hp, bi, qi, 0)),
        out_shape=jax.ShapeDtypeStruct((n_pairs, b, s, LANES_V7X), jnp.bfloat16),
        compiler_params=pltpu.CompilerParams(
            dimension_semantics=("parallel", "parallel", "parallel"),
            vmem_limit_bytes=VMEM_LIMIT_BYTES),
        name="sb_attention",
    )(qkv, qkv, qkv, tri)


def _mem_kv_kernel(m_ref, g_ref, wk_ref, wv_ref, k_ref, v_ref):
    mn = _rms(m_ref[...], g_ref[...]).astype(jnp.bfloat16)
    k_ref[...] = _dot(mn, wk_ref[...]).astype(jnp.bfloat16)
    v_ref[...] = _dot(mn, wv_ref[...]).astype(jnp.bfloat16)


def _mem_kv(mem2, gain, wk, wv, *, tm):
    n, d = mem2.shape
    row = pl.BlockSpec((tm, d), lambda i: (i, 0))
    full = pl.BlockSpec((d, d), lambda i: (0, 0))
    return pl.pallas_call(
        _mem_kv_kernel,
        grid=(n // tm,),
        in_specs=[row, pl.BlockSpec((1, d), lambda i: (0, 0)), full, full],
        out_specs=[row, row],
        out_shape=[jax.ShapeDtypeStruct((n, d), jnp.bfloat16)] * 2,
        compiler_params=pltpu.CompilerParams(
            dimension_semantics=("parallel",), vmem_limit_bytes=VMEM_LIMIT_BYTES),
        name="mem_kv",
    )(mem2, gain, wk, wv)


def _route_t(logits):
    neg = jnp.float32(-jnp.inf)
    big = jnp.float32(ROUTER_ROWS)
    row = lax.broadcasted_iota(jnp.int32, logits.shape, 0).astype(jnp.float32)

    def col_max(mask):
        return jnp.max(jnp.where(mask, logits, neg), axis=0, keepdims=True)

    def first_argmax(mask, mx):
        return jnp.min(jnp.where(mask & (logits == mx), row, big), axis=0, keepdims=True)

    gmask = row < N_GROUPS
    gmax = col_max(gmask)
    gsum = jnp.sum(jnp.where(gmask, jnp.exp(logits - gmax), 0.0), axis=0, keepdims=True)
    g_gate = 1.0 / gsum
    g_idx = first_argmax(gmask, gmax)

    lo = N_GROUPS + EXPERTS_PER_GROUP * g_idx
    emask = (row >= lo) & (row < lo + EXPERTS_PER_GROUP)
    m1 = col_max(emask)
    i1 = first_argmax(emask, m1)
    mask2 = emask & (row != i1)
    m2 = col_max(mask2)
    i2 = first_argmax(mask2, m2)
    esum = jnp.sum(jnp.where(emask, jnp.exp(logits - m1), 0.0), axis=0, keepdims=True)
    p1 = 1.0 / esum
    p2 = jnp.exp(m2 - m1) / esum
    tot = p1 + p2
    w1 = g_gate * (p1 / tot)
    w2 = g_gate * (p2 / tot)

    first = i1 < i2
    la = jnp.where(first, i1, i2) - lo
    lb = jnp.where(first, i2, i1) - lo
    pair = sum(jnp.where((la == a) & (lb == b), float(p), 0.0) for p, (a, b) in enumerate(PAIRS))
    bucket = g_idx * len(PAIRS) + pair
    return bucket, jnp.where(first, w1, w2), jnp.where(first, w2, w1)


def _mix_cross_kernel(x_ref, u_ref, halo_ref, a_ref, wpbd_ref, ps_ref, wo_ref, gc_ref,
                      wq_ref, km_ref, vm_ref, wom_ref, gf_ref, wr_ref, su_ref,
                      h_ref, xrow_ref, bucket_ref, rank_ref, cnt_ref, run_ref, *, tm, tiles_per_seq):
    i = pl.program_id(0)
    pool_width = u_ref.shape[1]
    gw = pool_width // len(POOL_WINDOWS)
    tile_in_seq = i % tiles_per_seq
    first = tile_in_seq == 0
    pos = tile_in_seq * tm + lax.broadcasted_iota(jnp.int32, (tm, 1), 0)

    halo = jnp.where(first, 0.0, halo_ref[...])
    u = u_ref[...]
    pooled = []
    for g, w in enumerate(POOL_WINDOWS):
        ug = u[:, g * gw:(g + 1) * gw]
        ext = jnp.concatenate([halo[:, g * gw:(g + 1) * gw], ug], axis=0)
        shift = 1
        while shift < w:
            ext = ext + pltpu.roll(ext, shift, 0)
            shift *= 2
        win = ext[POOL_HALO:, :]
        inv_count = 1.0 / jnp.minimum(pos + 1, w).astype(jnp.float32)
        pooled.append((win * inv_count - ug).astype(jnp.bfloat16))

    pool_out = _dot(jnp.concatenate(pooled, axis=1), wpbd_ref[...]) * ps_ref[...]
    mixed = jnp.concatenate([pool_out.astype(jnp.bfloat16)] + [a_ref[j] for j in range(a_ref.shape[0])], axis=1)
    h1 = x_ref[...] + _dot(mixed, wo_ref[...])

    hn = _rms(h1, gc_ref[...]).astype(jnp.bfloat16)
    d = h1.shape[1]
    hd = d // MEM_HEADS
    q = (_dot(hn, wq_ref[...]) * (1.0 / math.sqrt(hd))).astype(jnp.bfloat16)
    outs = []
    for hh in range(MEM_HEADS):
        sl = slice(hh * hd, (hh + 1) * hd)
        s = _dot_nt(q[:, sl], km_ref[:, sl])
        e = jnp.exp(s - jnp.max(s, axis=-1, keepdims=True))
        p = e * (1.0 / jnp.sum(e, axis=-1, keepdims=True))
        outs.append(_dot(p.astype(jnp.bfloat16), vm_ref[:, sl]))
    o = jnp.concatenate(outs, axis=-1).astype(jnp.bfloat16)
    h2 = h1 + _dot(o, wom_ref[...])
    h_ref[...] = h2

    xn = _rms(h2, gf_ref[...])
    xrow_ref[:, :d] = xn
    x_hi = xn.astype(jnp.bfloat16)
    x_lo = (xn - x_hi.astype(jnp.float32)).astype(jnp.bfloat16)
    both = _dot(x_hi, wr_ref[...])
    logits_tm = both[:, :ROUTER_LANES] + both[:, ROUTER_LANES:] + _dot(x_lo, wr_ref[:, :ROUTER_LANES])
    logits = logits_tm.T[:ROUTER_ROWS]
    bucket, w_a, w_b = _route_t(logits)

    @pl.when(i == 0)
    def _():
        run_ref[...] = jnp.zeros_like(run_ref)

    brow = lax.broadcasted_iota(jnp.int32, (ROUTER_ROWS, tm), 0).astype(jnp.float32)
    onehot = (brow == bucket).astype(jnp.float32)
    before = _dot(onehot.astype(jnp.bfloat16), su_ref[...])
    run = run_ref[...]
    rank = jnp.sum(onehot * (before + run[:, :1]), axis=0, keepdims=True)
    run = run + jnp.sum(onehot, axis=1, keepdims=True)
    run_ref[...] = run
    cnt_ref[...] = run
    bucket_ref[...] = bucket.astype(jnp.int32)
    rank_ref[...] = rank.astype(jnp.int32)

    prow = lax.broadcasted_iota(jnp.int32, (PAYLOAD_LANES, tm), 0)
    payload_t = jnp.where(prow == 0, w_a, jnp.where(prow == 1, w_b, 0.0))
    xrow_ref[:, d:] = payload_t.T


def _mix_cross(x2, u, attn, w_pool_bd, pool_scale, w_out, g_cross, wq, kmem, vmem, wom, g_ffn, w_router,
               *, tm, seq, mem_len):
    n, d = x2.shape
    pool_width = u.shape[1]
    tiles_per_seq = seq // tm
    n_tiles = n // tm
    halo_blocks = tm // POOL_HALO
    su = (lax.broadcasted_iota(jnp.int32, (tm, tm), 0)
          < lax.broadcasted_iota(jnp.int32, (tm, tm), 1)).astype(jnp.bfloat16)
    kern = functools.partial(_mix_cross_kernel, tm=tm, tiles_per_seq=tiles_per_seq)
    const2 = lambda i: (0, 0)
    mem_map = lambda i: (i // tiles_per_seq, 0)
    return pl.pallas_call(
        kern,
        grid=(n_tiles,),
        in_specs=[
            pl.BlockSpec((tm, d), lambda i: (i, 0)),
            pl.BlockSpec((tm, pool_width), lambda i: (i, 0)),
            pl.BlockSpec((POOL_HALO, pool_width), lambda i: (jnp.maximum(i * halo_blocks - 1, 0), 0)),
            pl.BlockSpec((attn.shape[0], tm, LANES_V7X), lambda i: (0, i, 0)),
            pl.BlockSpec(w_pool_bd.shape, const2),
            pl.BlockSpec((1, pool_width), const2),
            pl.BlockSpec(w_out.shape, const2),
            pl.BlockSpec((1, d), const2),
            pl.BlockSpec((d, d), const2),
            pl.BlockSpec((mem_len, d), mem_map),
            pl.BlockSpec((mem_len, d), mem_map),
            pl.BlockSpec((d, d), const2),
            pl.BlockSpec((1, d), const2),
            pl.BlockSpec((d, 2 * ROUTER_LANES), const2),
            pl.BlockSpec((tm, tm), const2),
        ],
        out_specs=[
            pl.BlockSpec((tm, d), lambda i: (i, 0)),
            pl.BlockSpec((tm, d + PAYLOAD_LANES), lambda i: (i, 0)),
            pl.BlockSpec((None, 1, tm), lambda i: (i, 0, 0)),
            pl.BlockSpec((None, 1, tm), lambda i: (i, 0, 0)),
            pl.BlockSpec((ROUTER_ROWS, LANES_V7X), const2),
        ],
        out_shape=[
            jax.ShapeDtypeStruct((n, d), jnp.float32),
            jax.ShapeDtypeStruct((n, d + PAYLOAD_LANES), jnp.float32),
            jax.ShapeDtypeStruct((n_tiles, 1, tm), jnp.int32),
            jax.ShapeDtypeStruct((n_tiles, 1, tm), jnp.int32),
            jax.ShapeDtypeStruct((ROUTER_ROWS, LANES_V7X), jnp.float32),
        ],
        scratch_shapes=[pltpu.VMEM((ROUTER_ROWS, LANES_V7X), jnp.float32)],
        compiler_params=pltpu.CompilerParams(
            dimension_semantics=("arbitrary",), vmem_limit_bytes=VMEM_LIMIT_BYTES),
        name="mix_cross",
    )(x2, u, u, attn, w_pool_bd, pool_scale, w_out, g_cross, wq, kmem, vmem, wom, g_ffn, w_router, su)


def _dispatch_kernel(pos_ref, last_ref, n_last_ref, na_ref, x_ref, xs_ref, stage, zeros, sems, fill_sem,
                     tail_sem, *, tm, n_tiles, n_tiles_e):
    i = pl.program_id(0)
    slot = i % 2
    base = i * tm
    tm_e = zeros.shape[0]

    def fill_copy(j, sem):
        return pltpu.make_async_copy(zeros, xs_ref.at[pl.ds(pl.multiple_of(j * tm_e, tm_e), tm_e), :], sem)

    def for_range(lo, hi, fn):
        def body(k, carry):
            fn(k)
            return carry
        lax.fori_loop(lo, hi, body, 0)

    @pl.when(i == 0)
    def _():
        zeros[...] = jnp.zeros_like(zeros)
        for_range(0, n_last_ref[0], lambda k: fill_copy(last_ref[k], fill_sem).start())
        for_range(na_ref[0], n_tiles_e, lambda j: fill_copy(j, tail_sem).start())
        for_range(0, n_last_ref[0], lambda k: fill_copy(0, fill_sem).wait())

    def row_copy(slot, r, p):
        return pltpu.make_async_copy(stage.at[slot, pl.ds(r, 1), :], xs_ref.at[pl.ds(p, 1), :], sems.at[slot])

    def wait_slot(slot):
        pltpu.make_async_copy(stage.at[slot], xs_ref.at[pl.ds(0, tm), :], sems.at[slot]).wait()

    stage[slot] = x_ref[...]
    for r in range(tm):
        row_copy(slot, r, pos_ref[base + r]).start(priority=r % 2)

    @pl.when(i > 0)
    def _():
        wait_slot(1 - slot)

    @pl.when(i == n_tiles - 1)
    def _():
        wait_slot(slot)
        for_range(na_ref[0], n_tiles_e, lambda j: fill_copy(0, tail_sem).wait())


def _dispatch(pos, last_tiles, n_last, n_active, xrow, *, tm, tm_e, n_tiles_e):
    n, width = xrow.shape
    return pl.pallas_call(
        functools.partial(_dispatch_kernel, tm=tm, n_tiles=n // tm, n_tiles_e=n_tiles_e),
        grid_spec=pltpu.PrefetchScalarGridSpec(
            num_scalar_prefetch=4,
            grid=(n // tm,),
            in_specs=[pl.BlockSpec((tm, width), lambda i, *_: (i, 0))],
            out_specs=pl.BlockSpec(memory_space=pl.ANY),
            scratch_shapes=[pltpu.VMEM((2, tm, width), xrow.dtype), pltpu.VMEM((tm_e, width), xrow.dtype),
                            pltpu.SemaphoreType.DMA((2,)), pltpu.SemaphoreType.DMA, pltpu.SemaphoreType.DMA],
        ),
        out_shape=jax.ShapeDtypeStruct((n_tiles_e * tm_e, width), xrow.dtype),
        compiler_params=pltpu.CompilerParams(
            dimension_semantics=("arbitrary",), vmem_limit_bytes=VMEM_LIMIT_BYTES),
        name="moe_dispatch",
    )(pos, last_tiles, n_last, n_active, xrow)


def _expert_kernel(ea_ref, eb_ref, na_ref, xs_ref, wgua_ref, wda_ref, wgub_ref, wdb_ref, y_ref, *, d):
    del ea_ref, eb_ref
    active = pl.program_id(0) < na_ref[0]

    @pl.when(jnp.logical_not(active))
    def _():
        y_ref[...] = jnp.zeros_like(y_ref)

    @pl.when(active)
    def _():
        x = xs_ref[:, :d].astype(jnp.bfloat16)

        def mlp(wgu_ref, wd_ref):
            gu = _dot(x, wgu_ref[...])
            ff = gu.shape[1] // 2
            gate, up = gu[:, :ff], gu[:, ff:]
            hmid = (gate * (1.0 / (1.0 + jnp.exp(-gate)))) * up
            return _dot(hmid.astype(jnp.bfloat16), wd_ref[...])

        y_ref[...] = (xs_ref[:, d:d + 1] * mlp(wgua_ref, wda_ref)
                      + xs_ref[:, d + 1:d + 2] * mlp(wgub_ref, wdb_ref))


def _experts(tile_ea, tile_eb, n_active, xs, wgu, wd, *, tm):
    n_pad, width = xs.shape
    d, ff = wd.shape[2], wd.shape[1]
    row_map = lambda j, ea, eb, na: (jnp.minimum(j, na[0] - 1), 0)
    a_map = lambda j, ea, eb, na: (ea[j], 0, 0)
    b_map = lambda j, ea, eb, na: (eb[j], 0, 0)
    return pl.pallas_call(
        functools.partial(_expert_kernel, d=d),
        grid_spec=pltpu.PrefetchScalarGridSpec(
            num_scalar_prefetch=3,
            grid=(n_pad // tm,),
            in_specs=[
                pl.BlockSpec((tm, width), row_map),
                pl.BlockSpec((None, d, 2 * ff), a_map),
                pl.BlockSpec((None, ff, d), a_map),
                pl.BlockSpec((None, d, 2 * ff), b_map),
                pl.BlockSpec((None, ff, d), b_map),
            ],
            out_specs=pl.BlockSpec((tm, d), lambda j, ea, eb, na: (j, 0)),
        ),
        out_shape=jax.ShapeDtypeStruct((n_pad, d), jnp.float32),
        compiler_params=pltpu.CompilerParams(
            dimension_semantics=("arbitrary",), vmem_limit_bytes=VMEM_LIMIT_BYTES),
        name="moe_experts",
    )(tile_ea, tile_eb, n_active, xs, wgu, wd, wgu, wd)


def _combine_kernel(pos_ref, h_ref, y_ref, g_ref, o_ref, ybuf, sems, *, tm, n_tiles):
    i = pl.program_id(0)

    def row_copy(slot, r, p):
        return pltpu.make_async_copy(y_ref.at[pl.ds(p, 1), :], ybuf.at[slot, pl.ds(r, 1), :], sems.at[slot])

    @pl.when(i < n_tiles)
    def _():
        slot = i % 2
        for r in range(tm):
            row_copy(slot, r, pos_ref[i * tm + r]).start(priority=r % 2)

    @pl.when(i > 0)
    def _():
        slot = (i - 1) % 2
        pltpu.make_async_copy(y_ref.at[pl.ds(0, tm), :], ybuf.at[slot], sems.at[slot]).wait()
        o_ref[...] = _rms(h_ref[...] + ybuf[slot], g_ref[...])


def _combine(pos, h, y_sorted, g_final, *, tm):
    n, d = h.shape
    n_tiles = n // tm
    prev_tile = lambda i, pos: (jnp.maximum(i - 1, 0), 0)
    return pl.pallas_call(
        functools.partial(_combine_kernel, tm=tm, n_tiles=n_tiles),
        grid_spec=pltpu.PrefetchScalarGridSpec(
            num_scalar_prefetch=1,
            grid=(n_tiles + 1,),
            in_specs=[
                pl.BlockSpec((tm, d), prev_tile),
                pl.BlockSpec(memory_space=pl.ANY),
                pl.BlockSpec((1, d), lambda i, pos: (0, 0)),
            ],
            out_specs=pl.BlockSpec((tm, d), prev_tile),
            scratch_shapes=[pltpu.VMEM((2, tm, d), jnp.float32), pltpu.SemaphoreType.DMA((2,))],
        ),
        out_shape=jax.ShapeDtypeStruct((n, d), jnp.float32),
        compiler_params=pltpu.CompilerParams(
            dimension_semantics=("arbitrary",), vmem_limit_bytes=VMEM_LIMIT_BYTES),
        name="moe_combine",
    )(pos, h, y_sorted, g_final)


def _tile(n, pref):
    t = min(n, pref)
    assert n % t == 0, (n, t)
    return t


def _sorted_layout(bucket, rank, counts, *, tm_e, n_tiles_e):
    seg_tiles = (counts + tm_e - 1) // tm_e
    seg_end = jnp.cumsum(seg_tiles)
    seg_start = seg_end - seg_tiles
    pos = (seg_start * tm_e)[bucket] + rank
    nonempty = seg_tiles > 0
    last_tiles = jnp.sort(jnp.where(nonempty, seg_end - 1, n_tiles_e))
    n_active = seg_end[-1]
    tile = jnp.minimum(jnp.arange(n_tiles_e, dtype=jnp.int32), n_active - 1)
    tile_bucket = jnp.sum((tile[:, None] >= seg_end[None, :]).astype(jnp.int32), axis=1)
    group, pair = tile_bucket // len(PAIRS), tile_bucket % len(PAIRS)
    pair_a = jnp.array([p[0] for p in PAIRS], jnp.int32)
    pair_b = jnp.array([p[1] for p in PAIRS], jnp.int32)
    tile_ea = group * EXPERTS_PER_GROUP + pair_a[pair]
    tile_eb = group * EXPERTS_PER_GROUP + pair_b[pair]
    i32 = lambda a: a.astype(jnp.int32)
    return i32(pos), i32(last_tiles), i32(jnp.sum(nonempty)), i32(tile_ea), i32(tile_eb), i32(n_active)


def kernel(x, mem, norm_mix, w_in, w_pool, pool_scale, w_out, norm_cross, norm_mem, w_q_mem, w_k_mem,
           w_v_mem, w_o_mem, norm_ffn, w_group, w_expert, w_gate, w_up, w_down, norm_final):
    b, s, d = x.shape
    mem_len = mem.shape[1]
    depth = norm_mix.shape[0]
    pool_width = pool_scale.shape[1]
    attn_width = w_out.shape[1] - pool_width
    bf = jnp.bfloat16
    n = b * s

    assert depth == 1, "single-layer problem: the final RMSNorm is fused into the combine kernel"
    l = 0
    h = x.reshape(n, d)
    u, qkv = _in_proj(h, norm_mix[l][None], w_in[l].astype(bf),
                      pool_width=pool_width, attn_width=attn_width, tm=_tile(n, 1024))
    tq = _tile(s, 2 * MXU_DIM_V7X)
    sub = min(tq, MXU_DIM_V7X)
    attn = _sb_attention(qkv.reshape(-1, b, s, LANES_V7X), tq=tq, sub=sub, win=2 * sub)
    kmem, vmem = _mem_kv(mem.reshape(b * mem_len, d), norm_mem[l][None],
                         w_k_mem[l].astype(bf), w_v_mem[l].astype(bf), tm=mem_len)

    w_router = jnp.concatenate([w_group[l], w_expert[l]], axis=1)
    w_router = jnp.pad(w_router, ((0, 0), (0, ROUTER_LANES - w_router.shape[1])))
    wr_hi = w_router.astype(bf)
    wr_lo = (w_router - wr_hi.astype(jnp.float32)).astype(bf)
    w_pool_bd = jax.scipy.linalg.block_diag(*w_pool[l]).astype(bf)
    h, xrow, bucket, rank, counts = _mix_cross(
        h, u, attn.reshape(-1, n, LANES_V7X), w_pool_bd, pool_scale[l][None], w_out[l].astype(bf),
        norm_cross[l][None], w_q_mem[l].astype(bf), kmem, vmem, w_o_mem[l].astype(bf),
        norm_ffn[l][None], jnp.concatenate([wr_hi, wr_lo], axis=1), tm=_tile(s, 1024), seq=s, mem_len=mem_len)

    tm_e = _tile(n, 2 * MXU_DIM_V7X)
    n_tiles_e = -(-(n + N_BUCKETS * (tm_e - 1)) // tm_e)
    pos, last_tiles, n_last, tile_ea, tile_eb, n_active = _sorted_layout(
        bucket.reshape(n), rank.reshape(n), counts[:N_BUCKETS, 0].astype(jnp.int32),
        tm_e=tm_e, n_tiles_e=n_tiles_e)
    xs = _dispatch(pos, last_tiles, n_last[None], n_active[None], xrow, tm=_tile(n, 1024), tm_e=tm_e,
                   n_tiles_e=n_tiles_e)
    w_gate_up = jnp.concatenate([w_gate[l], w_up[l]], axis=-1).astype(bf)
    y_sorted = _experts(tile_ea, tile_eb, n_active[None], xs, w_gate_up, w_down[l].astype(bf), tm=tm_e)
    out = _combine(pos, h, y_sorted, norm_final[None], tm=_tile(n, 256))
    return out.reshape(b, s, d)
```

```python
import functools
import math

import jax
import jax.numpy as jnp
from jax import lax
from jax.experimental import pallas as pl
from jax.experimental.pallas import tpu as pltpu

RMS_EPS = 1e-6
POOL_WINDOWS = (2, 4, 8, 16)
SB_HEAD_DIM = 64
MEM_HEADS = 4
N_GROUPS = 4
EXPERTS_PER_GROUP = 4
N_EXPERTS = N_GROUPS * EXPERTS_PER_GROUP

LANES_V7X = 128
MXU_DIM_V7X = 256
VMEM_LIMIT_BYTES = 56 * 1024 * 1024

LOG2E = 1.4426950408889634
UNDERFLOW_EXPONENT = 104.0
MASKED_LOGIT = -1e30
POOL_HALO = 16

PAIRS = ((0, 1), (0, 2), (1, 2), (1, 3), (2, 3), (0, 3))
assert sorted(PAIRS) == [(a, b) for a in range(EXPERTS_PER_GROUP) for b in range(a + 1, EXPERTS_PER_GROUP)]
N_BUCKETS = N_GROUPS * len(PAIRS)
ROUTER_ROWS = 32
ROUTER_LANES = LANES_V7X
PAYLOAD_LANES = LANES_V7X
assert N_GROUPS + N_EXPERTS <= ROUTER_ROWS and N_BUCKETS <= ROUTER_ROWS


def _rms(x, gain):
    ms = jnp.mean(x * x, axis=-1, keepdims=True)
    return x * lax.rsqrt(ms + RMS_EPS) * gain


def _dot(a, b):
    return jnp.dot(a, b, preferred_element_type=jnp.float32)


def _dot_nt(a, b):
    return lax.dot_general(a, b, (((1,), (1,)), ((), ())), preferred_element_type=jnp.float32)


def _in_proj_kernel(x_ref, g_ref, w_ref, u_ref, qkv_ref, *, pool_width, attn_width, q_scale):
    xn = _rms(x_ref[...], g_ref[...]).astype(jnp.bfloat16)
    proj = _dot(xn, w_ref[...])
    u_ref[...] = proj[:, :pool_width]
    q = proj[:, pool_width:pool_width + attn_width] * q_scale
    qkv = jnp.concatenate([q, proj[:, pool_width + attn_width:]], axis=1).astype(jnp.bfloat16)
    for j in range(qkv_ref.shape[0]):
        qkv_ref[j] = qkv[:, j * LANES_V7X:(j + 1) * LANES_V7X]


def _in_proj(x2, gain, w_in_bf16, *, pool_width, attn_width, tm):
    n, d = x2.shape
    in_width = w_in_bf16.shape[1]
    n_slabs = 3 * attn_width // LANES_V7X
    kern = functools.partial(_in_proj_kernel, pool_width=pool_width, attn_width=attn_width,
                             q_scale=1.0 / math.sqrt(SB_HEAD_DIM))
    return pl.pallas_call(
        kern,
        grid=(n // tm,),
        in_specs=[
            pl.BlockSpec((tm, d), lambda i: (i, 0)),
            pl.BlockSpec((1, d), lambda i: (0, 0)),
            pl.BlockSpec((d, in_width), lambda i: (0, 0)),
        ],
        out_specs=[
            pl.BlockSpec((tm, pool_width), lambda i: (i, 0)),
            pl.BlockSpec((n_slabs, tm, LANES_V7X), lambda i: (0, i, 0)),
        ],
        out_shape=[
            jax.ShapeDtypeStruct((n, pool_width), jnp.float32),
            jax.ShapeDtypeStruct((n_slabs, n, LANES_V7X), jnp.bfloat16),
        ],
        compiler_params=pltpu.CompilerParams(
            dimension_semantics=("parallel",), vmem_limit_bytes=VMEM_LIMIT_BYTES),
        name="in_proj",
    )(x2, gain, w_in_bf16)


def _sb_tile(qh, kb, vb, tri, c, tail_mask):
    z = _dot_nt(qh, kb)
    if tail_mask is not None:
        m = tail_mask.shape[1]
        tail = jnp.where(tail_mask, z[:, -m:], MASKED_LOGIT)
        z = tail if m == z.shape[1] else jnp.concatenate([z[:, :-m], tail], axis=1)
    sp = (jnp.maximum(z, 0.0) + jnp.log(1.0 + jnp.exp2(jnp.abs(z) * (-LOG2E)))).astype(jnp.bfloat16)
    n = tri.shape[0]
    chunks, total = [], None
    for k0 in range(z.shape[1] - n, -1, -n):
        cum_k = _dot(sp[:, k0:k0 + n], tri)
        if total is not None:
            cum_k = cum_k + total
        total = cum_k[:, :1]
        chunks.insert(0, cum_k)
    cum = chunks[0] if len(chunks) == 1 else jnp.concatenate(chunks, axis=1)
    w = jnp.exp(z - cum) if c is None else jnp.exp(jnp.minimum(z - cum, 0.0) - c)
    return (total if c is None else c + total), _dot(w.astype(jnp.bfloat16), vb)


def _sb_attn_kernel(q_ref, k_ref, v_ref, tri_ref, o_ref, *, tq, sub, win):
    qi = pl.program_id(2)
    n_sub = tq // sub
    lane = lax.broadcasted_iota(jnp.int32, (tq, LANES_V7X), 1)
    head0 = lane < SB_HEAD_DIM
    q2 = q_ref[...]
    zero = jnp.zeros_like(q2)
    q_heads = (jnp.where(head0, q2, zero), jnp.where(head0, zero, q2))
    q_st = jnp.concatenate(q_heads, axis=0)

    def kv_block(j):
        start = pl.multiple_of(j * tq, tq)
        return k_ref[pl.ds(start, tq), :], v_ref[pl.ds(start, tq), :]

    def stacked_iota(rows, cols):
        row = lax.broadcasted_iota(jnp.int32, (2 * rows, cols), 0)
        col = lax.broadcasted_iota(jnp.int32, (2 * rows, cols), 1)
        return jnp.where(row >= rows, row - rows, row), col

    def first_block():
        row, col = stacked_iota(tq, tq)
        kb, vb = kv_block(0)
        return _sb_tile(q_st, kb, vb, tri_ref[...], None, col < row)

    def windows():
        row, col = stacked_iota(sub, sub)
        causal = col < row
        tri = tri_ref[...]
        parts = []
        for hf in range(n_sub):
            start = pl.multiple_of((qi * n_sub + hf + 1) * sub - win, sub)
            kw = k_ref[pl.ds(start, win), :]
            vw = v_ref[pl.ds(start, win), :]
            q_sub = jnp.concatenate([qh[hf * sub:(hf + 1) * sub] for qh in q_heads], axis=0)
            parts.append(_sb_tile(q_sub, kw, vw, tri, None, causal))
        return tuple(jnp.concatenate([p[i][h * sub:(h + 1) * sub] for h in range(2) for p in parts], axis=0)
                     for i in range(2))

    carry = lax.cond(qi == 0, first_block, windows)

    def live(carry):
        return jnp.min(carry[0], axis=0, keepdims=True)[0, 0] < UNDERFLOW_EXPONENT

    def add_block(carry, j, tail_mask):
        kb, vb = kv_block(j)
        c, acc = _sb_tile(q_st, kb, vb, tri_ref[...], carry[0], tail_mask)
        return c, carry[1] + acc

    def partial_block(back):
        def fn(carry):
            row, col = stacked_iota(tq, tq)
            fresh = col < (row // sub + 1) * sub - win + back * tq
            return add_block(carry, qi - back, fresh)
        return fn

    for back in ((0, 1) if win < tq else (1,)):
        carry = lax.cond(jnp.logical_and(qi > 0, live(carry)), partial_block(back), lambda cr: cr, carry)

    def cond(state):
        step, alive, _ = state
        return jnp.logical_and(step < qi - 1, alive > 0)

    def body(state):
        step, _, carry = state
        carry = add_block(carry, qi - 2 - step, None)
        return step + 1, live(carry).astype(jnp.int32), carry

    _, _, carry = lax.while_loop(cond, body, (jnp.int32(0), live(carry).astype(jnp.int32), carry))
    acc = carry[1]
    o_ref[...] = jnp.where(head0, acc[:tq], acc[tq:]).astype(o_ref.dtype)


def _sb_attention(qkv, *, tq, sub, win):
    n_slabs, b, s, _ = qkv.shape
    assert tq % sub == 0 and win % sub == 0 and sub <= win <= tq + sub, (tq, sub, win)
    n_pairs = n_slabs // 3
    n = math.gcd(tq, win)
    tri = (lax.broadcasted_iota(jnp.int32, (n, n), 0)
           >= lax.broadcasted_iota(jnp.int32, (n, n), 1)).astype(jnp.bfloat16)

    kern = functools.partial(_sb_attn_kernel, tq=tq, sub=sub, win=win)
    const2 = lambda bi, hp, qi: (0, 0)
    return pl.pallas_call(
        kern,
        grid=(b, n_pairs, s // tq),
        in_specs=[
            pl.BlockSpec((None, None, tq, LANES_V7X), lambda bi, hp, qi: (hp, bi, qi, 0)),
            pl.BlockSpec((None, None, s, LANES_V7X), lambda bi, hp, qi: (n_pairs + hp, bi, 0, 0)),
            pl.BlockSpec((None, None, s, LANES_V7X), lambda bi, hp, qi: (2 * n_pairs + hp, bi, 0, 0)),
            pl.BlockSpec((n, n), const2),
        ],
        out_specs=pl.BlockSpec((None, None, tq, LANES_V7X), lambda bi, hp, qi: (hp, bi, qi, 0)),
        out_shape=jax.ShapeDtypeStruct((n_pairs, b, s, LANES_V7X), jnp.bfloat16),
        compiler_params=pltpu.CompilerParams(
            dimension_semantics=("parallel", "parallel", "parallel"),
            vmem_limit_bytes=VMEM_LIMIT_BYTES),
        name="sb_attention",
    )(qkv, qkv, qkv, tri)


def _mem_kv_kernel(m_ref, g_ref, wk_ref, wv_ref, k_ref, v_ref):
    mn = _rms(m_ref[...], g_ref[...]).astype(jnp.bfloat16)
    k_ref[...] = _dot(mn, wk_ref[...]).astype(jnp.bfloat16)
    v_ref[...] = _dot(mn, wv_ref[...]).astype(jnp.bfloat16)


def _mem_kv(mem2, gain, wk, wv, *, tm):
    n, d = mem2.shape
    row = pl.BlockSpec((tm, d), lambda i: (i, 0))
    full = pl.BlockSpec((d, d), lambda i: (0, 0))
    return pl.pallas_call(
        _mem_kv_kernel,
        grid=(n // tm,),
        in_specs=[row, pl.BlockSpec((1, d), lambda i: (0, 0)), full, full],
        out_specs=[row, row],
        out_shape=[jax.ShapeDtypeStruct((n, d), jnp.bfloat16)] * 2,
        compiler_params=pltpu.CompilerParams(
            dimension_semantics=("parallel",), vmem_limit_bytes=VMEM_LIMIT_BYTES),
        name="mem_kv",
    )(mem2, gain, wk, wv)


def _route_t(logits):
    neg = jnp.float32(-jnp.inf)
    big = jnp.float32(ROUTER_ROWS)
    row = lax.broadcasted_iota(jnp.int32, logits.shape, 0).astype(jnp.float32)

    def col_max(mask):
        return jnp.max(jnp.where(mask, logits, neg), axis=0, keepdims=True)

    def first_argmax(mask, mx):
        return jnp.min(jnp.where(mask & (logits == mx), row, big), axis=0, keepdims=True)

    gmask = row < N_GROUPS
    gmax = col_max(gmask)
    gsum = jnp.sum(jnp.where(gmask, jnp.exp(logits - gmax), 0.0), axis=0, keepdims=True)
    g_gate = 1.0 / gsum
    g_idx = first_argmax(gmask, gmax)

    lo = N_GROUPS + EXPERTS_PER_GROUP * g_idx
    emask = (row >= lo) & (row < lo + EXPERTS_PER_GROUP)
    m1 = col_max(emask)
    i1 = first_argmax(emask, m1)
    mask2 = emask & (row != i1)
    m2 = col_max(mask2)
    i2 = first_argmax(mask2, m2)
    esum = jnp.sum(jnp.where(emask, jnp.exp(logits - m1), 0.0), axis=0, keepdims=True)
    p1 = 1.0 / esum
    p2 = jnp.exp(m2 - m1) / esum
    tot = p1 + p2
    w1 = g_gate * (p1 / tot)
    w2 = g_gate * (p2 / tot)

    first = i1 < i2
    la = jnp.where(first, i1, i2) - lo
    lb = jnp.where(first, i2, i1) - lo
    pair = sum(jnp.where((la == a) & (lb == b), float(p), 0.0) for p, (a, b) in enumerate(PAIRS))
    bucket = g_idx * len(PAIRS) + pair
    return bucket, jnp.where(first, w1, w2), jnp.where(first, w2, w1)


def _mix_cross_kernel(x_ref, u_ref, halo_ref, a_ref, wpbd_ref, ps_ref, wo_ref, gc_ref,
                      wq_ref, km_ref, vm_ref, wom_ref, gf_ref, wr_ref, su_ref,
                      h_ref, xrow_ref, bucket_ref, rank_ref, cnt_ref, run_ref, *, tm, tiles_per_seq):
    i = pl.program_id(0)
    pool_width = u_ref.shape[1]
    gw = pool_width // len(POOL_WINDOWS)
    tile_in_seq = i % tiles_per_seq
    first = tile_in_seq == 0
    pos = tile_in_seq * tm + lax.broadcasted_iota(jnp.int32, (tm, 1), 0)

    halo = jnp.where(first, 0.0, halo_ref[...])
    u = u_ref[...]
    pooled = []
    for g, w in enumerate(POOL_WINDOWS):
        ug = u[:, g * gw:(g + 1) * gw]
        ext = jnp.concatenate([halo[:, g * gw:(g + 1) * gw], ug], axis=0)
        shift = 1
        while shift < w:
            ext = ext + pltpu.roll(ext, shift, 0)
            shift *= 2
        win = ext[POOL_HALO:, :]
        inv_count = 1.0 / jnp.minimum(pos + 1, w).astype(jnp.float32)
        pooled.append((win * inv_count - ug).astype(jnp.bfloat16))

    pool_out = _dot(jnp.concatenate(pooled, axis=1), wpbd_ref[...]) * ps_ref[...]
    mixed = jnp.concatenate([pool_out.astype(jnp.bfloat16)] + [a_ref[j] for j in range(a_ref.shape[0])], axis=1)
    h1 = x_ref[...] + _dot(mixed, wo_ref[...])

    hn = _rms(h1, gc_ref[...]).astype(jnp.bfloat16)
    d = h1.shape[1]
    hd = d // MEM_HEADS
    q = (_dot(hn, wq_ref[...]) * (1.0 / math.sqrt(hd))).astype(jnp.bfloat16)
    outs = []
    for hh in range(MEM_HEADS):
        sl = slice(hh * hd, (hh + 1) * hd)
        s = _dot_nt(q[:, sl], km_ref[:, sl])
        e = jnp.exp(s - jnp.max(s, axis=-1, keepdims=True))
        p = e * (1.0 / jnp.sum(e, axis=-1, keepdims=True))
        outs.append(_dot(p.astype(jnp.bfloat16), vm_ref[:, sl]))
    o = jnp.concatenate(outs, axis=-1).astype(jnp.bfloat16)
    h2 = h1 + _dot(o, wom_ref[...])
    h_ref[...] = h2

    xn = _rms(h2, gf_ref[...])
    xrow_ref[:, :d] = xn
    x_hi = xn.astype(jnp.bfloat16)
    x_lo = (xn - x_hi.astype(jnp.float32)).astype(jnp.bfloat16)
    both = _dot(x_hi, wr_ref[...])
    logits_tm = both[:, :ROUTER_LANES] + both[:, ROUTER_LANES:] + _dot(x_lo, wr_ref[:, :ROUTER_LANES])
    logits = logits_tm.T[:ROUTER_ROWS]
    bucket, w_a, w_b = _route_t(logits)

    @pl.when(i == 0)
    def _():
        run_ref[...] = jnp.zeros_like(run_ref)

    brow = lax.broadcasted_iota(jnp.int32, (ROUTER_ROWS, tm), 0).astype(jnp.float32)
    onehot = (brow == bucket).astype(jnp.float32)
    before = _dot(onehot.astype(jnp.bfloat16), su_ref[...])
    run = run_ref[...]
    rank = jnp.sum(onehot * (before + run[:, :1]), axis=0, keepdims=True)
    run = run + jnp.sum(onehot, axis=1, keepdims=True)
    run_ref[...] = run
    cnt_ref[...] = run
    bucket_ref[...] = bucket.astype(jnp.int32)
    rank_ref[...] = rank.astype(jnp.int32)

    prow = lax.broadcasted_iota(jnp.int32, (PAYLOAD_LANES, tm), 0)
    payload_t = jnp.where(prow == 0, w_a, jnp.where(prow == 1, w_b, 0.0))
    xrow_ref[:, d:] = payload_t.T


def _mix_cross(x2, u, attn, w_pool_bd, pool_scale, w_out, g_cross, wq, kmem, vmem, wom, g_ffn, w_router,
               *, tm, seq, mem_len):
    n, d = x2.shape
    pool_width = u.shape[1]
    tiles_per_seq = seq // tm
    n_tiles = n // tm
    halo_blocks = tm // POOL_HALO
    su = (lax.broadcasted_iota(jnp.int32, (tm, tm), 0)
          < lax.broadcasted_iota(jnp.int32, (tm, tm), 1)).astype(jnp.bfloat16)
    kern = functools.partial(_mix_cross_kernel, tm=tm, tiles_per_seq=tiles_per_seq)
    const2 = lambda i: (0, 0)
    mem_map = lambda i: (i // tiles_per_seq, 0)
    return pl.pallas_call(
        kern,
        grid=(n_tiles,),
        in_specs=[
            pl.BlockSpec((tm, d), lambda i: (i, 0)),
            pl.BlockSpec((tm, pool_width), lambda i: (i, 0)),
            pl.BlockSpec((POOL_HALO, pool_width), lambda i: (jnp.maximum(i * halo_blocks - 1, 0), 0)),
            pl.BlockSpec((attn.shape[0], tm, LANES_V7X), lambda i: (0, i, 0)),
            pl.BlockSpec(w_pool_bd.shape, const2),
            pl.BlockSpec((1, pool_width), const2),
            pl.BlockSpec(w_out.shape, const2),
            pl.BlockSpec((1, d), const2),
            pl.BlockSpec((d, d), const2),
            pl.BlockSpec((mem_len, d), mem_map),
            pl.BlockSpec((mem_len, d), mem_map),
            pl.BlockSpec((d, d), const2),
            pl.BlockSpec((1, d), const2),
            pl.BlockSpec((d, 2 * ROUTER_LANES), const2),
            pl.BlockSpec((tm, tm), const2),
        ],
        out_specs=[
            pl.BlockSpec((tm, d), lambda i: (i, 0)),
            pl.BlockSpec((tm, d + PAYLOAD_LANES), lambda i: (i, 0)),
            pl.BlockSpec((None, 1, tm), lambda i: (i, 0, 0)),
            pl.BlockSpec((None, 1, tm), lambda i: (i, 0, 0)),
            pl.BlockSpec((ROUTER_ROWS, LANES_V7X), const2),
        ],
        out_shape=[
            jax.ShapeDtypeStruct((n, d), jnp.float32),
            jax.ShapeDtypeStruct((n, d + PAYLOAD_LANES), jnp.float32),
            jax.ShapeDtypeStruct((n_tiles, 1, tm), jnp.int32),
            jax.ShapeDtypeStruct((n_tiles, 1, tm), jnp.int32),
            jax.ShapeDtypeStruct((ROUTER_ROWS, LANES_V7X), jnp.float32),
        ],
        scratch_shapes=[pltpu.VMEM((ROUTER_ROWS, LANES_V7X), jnp.float32)],
        compiler_params=pltpu.CompilerParams(
            dimension_semantics=("arbitrary",), vmem_limit_bytes=VMEM_LIMIT_BYTES),
        name="mix_cross",
    )(x2, u, u, attn, w_pool_bd, pool_scale, w_out, g_cross, wq, kmem, vmem, wom, g_ffn, w_router, su)


def _dispatch_kernel(pos_ref, last_ref, n_last_ref, na_ref, x_ref, xs_ref, stage, zeros, sems, fill_sem,
                     tail_sem, *, tm, n_tiles, n_tiles_e):
    i = pl.program_id(0)
    slot = i % 2
    base = i * tm
    tm_e = zeros.shape[0]

    def fill_copy(j, sem):
        return pltpu.make_async_copy(zeros, xs_ref.at[pl.ds(pl.multiple_of(j * tm_e, tm_e), tm_e), :], sem)

    def for_range(lo, hi, fn):
        def body(k, carry):
            fn(k)
            return carry
        lax.fori_loop(lo, hi, body, 0)

    @pl.when(i == 0)
    def _():
        zeros[...] = jnp.zeros_like(zeros)
        for_range(0, n_last_ref[0], lambda k: fill_copy(last_ref[k], fill_sem).start())
        for_range(na_ref[0], n_tiles_e, lambda j: fill_copy(j, tail_sem).start())
        for_range(0, n_last_ref[0], lambda k: fill_copy(0, fill_sem).wait())

    def row_copy(slot, r, p):
        return pltpu.make_async_copy(stage.at[slot, pl.ds(r, 1), :], xs_ref.at[pl.ds(p, 1), :], sems.at[slot])

    def wait_slot(slot):
        pltpu.make_async_copy(stage.at[slot], xs_ref.at[pl.ds(0, tm), :], sems.at[slot]).wait()

    stage[slot] = x_ref[...]
    for r in range(tm):
        row_copy(slot, r, pos_ref[base + r]).start(priority=r % 2)

    @pl.when(i > 0)
    def _():
        wait_slot(1 - slot)

    @pl.when(i == n_tiles - 1)
    def _():
        wait_slot(slot)
        for_range(na_ref[0], n_tiles_e, lambda j: fill_copy(0, tail_sem).wait())


def _dispatch(pos, last_tiles, n_last, n_active, xrow, *, tm, tm_e, n_tiles_e):
    n, width = xrow.shape
    return pl.pallas_call(
        functools.partial(_dispatch_kernel, tm=tm, n_tiles=n // tm, n_tiles_e=n_tiles_e),
        grid_spec=pltpu.PrefetchScalarGridSpec(
            num_scalar_prefetch=4,
            grid=(n // tm,),
            in_specs=[pl.BlockSpec((tm, width), lambda i, *_: (i, 0))],
            out_specs=pl.BlockSpec(memory_space=pl.ANY),
            scratch_shapes=[pltpu.VMEM((2, tm, width), xrow.dtype), pltpu.VMEM((tm_e, width), xrow.dtype),
                            pltpu.SemaphoreType.DMA((2,)), pltpu.SemaphoreType.DMA, pltpu.SemaphoreType.DMA],
        ),
        out_shape=jax.ShapeDtypeStruct((n_tiles_e * tm_e, width), xrow.dtype),
        compiler_params=pltpu.CompilerParams(
            dimension_semantics=("arbitrary",), vmem_limit_bytes=VMEM_LIMIT_BYTES),
        name="moe_dispatch",
    )(pos, last_tiles, n_last, n_active, xrow)


def _expert_kernel(ea_ref, eb_ref, na_ref, xs_ref, wga_ref, wua_ref, wda_ref, wgb_ref, wub_ref, wdb_ref, y_ref,
                   wgua_s, wda_s, wgub_s, wdb_s, *, d):
    j = pl.program_id(0)
    active = j < na_ref[0]

    def refresh(e_ref, wg_ref, wu_ref, wd_ref, wgu_s, wd_s):
        @pl.when(jnp.logical_or(j == 0, e_ref[j] != e_ref[jnp.maximum(j - 1, 0)]))
        def _():
            ff = wg_ref.shape[1]
            wgu_s[:, :ff] = wg_ref[...].astype(jnp.bfloat16)
            wgu_s[:, ff:] = wu_ref[...].astype(jnp.bfloat16)
            wd_s[...] = wd_ref[...].astype(jnp.bfloat16)

    @pl.when(jnp.logical_not(active))
    def _():
        y_ref[...] = jnp.zeros_like(y_ref)

    @pl.when(active)
    def _():
        refresh(ea_ref, wga_ref, wua_ref, wda_ref, wgua_s, wda_s)
        refresh(eb_ref, wgb_ref, wub_ref, wdb_ref, wgub_s, wdb_s)
        x = xs_ref[:, :d].astype(jnp.bfloat16)

        def mlp(wgu_ref, wd_ref):
            gu = _dot(x, wgu_ref[...])
            ff = gu.shape[1] // 2
            gate, up = gu[:, :ff], gu[:, ff:]
            hmid = (gate * (1.0 / (1.0 + jnp.exp(-gate)))) * up
            return _dot(hmid.astype(jnp.bfloat16), wd_ref[...])

        y_ref[...] = (xs_ref[:, d:d + 1] * mlp(wgua_s, wda_s)
                      + xs_ref[:, d + 1:d + 2] * mlp(wgub_s, wdb_s))


def _experts(tile_ea, tile_eb, n_active, xs, wg, wu, wd, *, tm):
    n_pad, width = xs.shape
    d, ff = wd.shape[2], wd.shape[1]
    bf = jnp.bfloat16
    row_map = lambda j, ea, eb, na: (jnp.minimum(j, na[0] - 1), 0)
    a_map = lambda j, ea, eb, na: (ea[j], 0, 0)
    b_map = lambda j, ea, eb, na: (eb[j], 0, 0)
    return pl.pallas_call(
        functools.partial(_expert_kernel, d=d),
        grid_spec=pltpu.PrefetchScalarGridSpec(
            num_scalar_prefetch=3,
            grid=(n_pad // tm,),
            in_specs=[
                pl.BlockSpec((tm, width), row_map),
                pl.BlockSpec((None, d, ff), a_map),
                pl.BlockSpec((None, d, ff), a_map),
                pl.BlockSpec((None, ff, d), a_map),
                pl.BlockSpec((None, d, ff), b_map),
                pl.BlockSpec((None, d, ff), b_map),
                pl.BlockSpec((None, ff, d), b_map),
            ],
            out_specs=pl.BlockSpec((tm, d), lambda j, ea, eb, na: (j, 0)),
            scratch_shapes=[pltpu.VMEM((d, 2 * ff), bf), pltpu.VMEM((ff, d), bf),
                            pltpu.VMEM((d, 2 * ff), bf), pltpu.VMEM((ff, d), bf)],
        ),
        out_shape=jax.ShapeDtypeStruct((n_pad, d), jnp.float32),
        compiler_params=pltpu.CompilerParams(
            dimension_semantics=("arbitrary",), vmem_limit_bytes=VMEM_LIMIT_BYTES),
        name="moe_experts",
    )(tile_ea, tile_eb, n_active, xs, wg, wu, wd, wg, wu, wd)


def _combine_kernel(pos_ref, h_ref, y_ref, g_ref, o_ref, ybuf, sems, *, tm, n_tiles):
    i = pl.program_id(0)

    def row_copy(slot, r, p):
        return pltpu.make_async_copy(y_ref.at[pl.ds(p, 1), :], ybuf.at[slot, pl.ds(r, 1), :], sems.at[slot])

    @pl.when(i < n_tiles)
    def _():
        slot = i % 2
        for r in range(tm):
            row_copy(slot, r, pos_ref[i * tm + r]).start(priority=r % 2)

    @pl.when(i > 0)
    def _():
        slot = (i - 1) % 2
        pltpu.make_async_copy(y_ref.at[pl.ds(0, tm), :], ybuf.at[slot], sems.at[slot]).wait()
        o_ref[...] = _rms(h_ref[...] + ybuf[slot], g_ref[...])


def _combine(pos, h, y_sorted, g_final, *, tm):
    n, d = h.shape
    n_tiles = n // tm
    prev_tile = lambda i, pos: (jnp.maximum(i - 1, 0), 0)
    return pl.pallas_call(
        functools.partial(_combine_kernel, tm=tm, n_tiles=n_tiles),
        grid_spec=pltpu.PrefetchScalarGridSpec(
            num_scalar_prefetch=1,
            grid=(n_tiles + 1,),
            in_specs=[
                pl.BlockSpec((tm, d), prev_tile),
                pl.BlockSpec(memory_space=pl.ANY),
                pl.BlockSpec((1, d), lambda i, pos: (0, 0)),
            ],
            out_specs=pl.BlockSpec((tm, d), prev_tile),
            scratch_shapes=[pltpu.VMEM((2, tm, d), jnp.float32), pltpu.SemaphoreType.DMA((2,))],
        ),
        out_shape=jax.ShapeDtypeStruct((n, d), jnp.float32),
        compiler_params=pltpu.CompilerParams(
            dimension_semantics=("arbitrary",), vmem_limit_bytes=VMEM_LIMIT_BYTES),
        name="moe_combine",
    )(pos, h, y_sorted, g_final)


def _tile(n, pref):
    t = min(n, pref)
    assert n % t == 0, (n, t)
    return t


def _sorted_layout(bucket, rank, counts, *, tm_e, n_tiles_e):
    seg_tiles = (counts + tm_e - 1) // tm_e
    seg_end = jnp.cumsum(seg_tiles)
    seg_start = seg_end - seg_tiles
    pos = (seg_start * tm_e)[bucket] + rank
    nonempty = seg_tiles > 0
    last_tiles = jnp.sort(jnp.where(nonempty, seg_end - 1, n_tiles_e))
    n_active = seg_end[-1]
    tile = jnp.minimum(jnp.arange(n_tiles_e, dtype=jnp.int32), n_active - 1)
    tile_bucket = jnp.sum((tile[:, None] >= seg_end[None, :]).astype(jnp.int32), axis=1)
    group, pair = tile_bucket // len(PAIRS), tile_bucket % len(PAIRS)
    pair_a = jnp.array([p[0] for p in PAIRS], jnp.int32)
    pair_b = jnp.array([p[1] for p in PAIRS], jnp.int32)
    tile_ea = group * EXPERTS_PER_GROUP + pair_a[pair]
    tile_eb = group * EXPERTS_PER_GROUP + pair_b[pair]
    i32 = lambda a: a.astype(jnp.int32)
    return i32(pos), i32(last_tiles), i32(jnp.sum(nonempty)), i32(tile_ea), i32(tile_eb), i32(n_active)


def kernel(x, mem, norm_mix, w_in, w_pool, pool_scale, w_out, norm_cross, norm_mem, w_q_mem, w_k_mem,
           w_v_mem, w_o_mem, norm_ffn, w_group, w_expert, w_gate, w_up, w_down, norm_final):
    b, s, d = x.shape
    mem_len = mem.shape[1]
    depth = norm_mix.shape[0]
    pool_width = pool_scale.shape[1]
    attn_width = w_out.shape[1] - pool_width
    bf = jnp.bfloat16
    n = b * s

    assert depth == 1, "single-layer problem: the final RMSNorm is fused into the combine kernel"
    l = 0
    h = x.reshape(n, d)
    u, qkv = _in_proj(h, norm_mix[l][None], w_in[l].astype(bf),
                      pool_width=pool_width, attn_width=attn_width, tm=_tile(n, 1024))
    tq = _tile(s, 2 * MXU_DIM_V7X)
    sub = min(tq, MXU_DIM_V7X)
    attn = _sb_attention(qkv.reshape(-1, b, s, LANES_V7X), tq=tq, sub=sub, win=2 * sub)
    kmem, vmem = _mem_kv(mem.reshape(b * mem_len, d), norm_mem[l][None],
                         w_k_mem[l].astype(bf), w_v_mem[l].astype(bf), tm=mem_len)

    w_router = jnp.concatenate([w_group[l], w_expert[l]], axis=1)
    w_router = jnp.pad(w_router, ((0, 0), (0, ROUTER_LANES - w_router.shape[1])))
    wr_hi = w_router.astype(bf)
    wr_lo = (w_router - wr_hi.astype(jnp.float32)).astype(bf)
    w_pool_bd = jax.scipy.linalg.block_diag(*w_pool[l]).astype(bf)
    h, xrow, bucket, rank, counts = _mix_cross(
        h, u, attn.reshape(-1, n, LANES_V7X), w_pool_bd, pool_scale[l][None], w_out[l].astype(bf),
        norm_cross[l][None], w_q_mem[l].astype(bf), kmem, vmem, w_o_mem[l].astype(bf),
        norm_ffn[l][None], jnp.concatenate([wr_hi, wr_lo], axis=1), tm=_tile(s, 1024), seq=s, mem_len=mem_len)

    tm_e = _tile(n, 2 * MXU_DIM_V7X)
    n_tiles_e = -(-(n + N_BUCKETS * (tm_e - 1)) // tm_e)
    pos, last_tiles, n_last, tile_ea, tile_eb, n_active = _sorted_layout(
        bucket.reshape(n), rank.reshape(n), counts[:N_BUCKETS, 0].astype(jnp.int32),
        tm_e=tm_e, n_tiles_e=n_tiles_e)
    xs = _dispatch(pos, last_tiles, n_last[None], n_active[None], xrow, tm=_tile(n, 1024), tm_e=tm_e,
                   n_tiles_e=n_tiles_e)
    y_sorted = _experts(tile_ea, tile_eb, n_active[None], xs, w_gate[l], w_up[l], w_down[l], tm=tm_e)
    out = _combine(pos, h, y_sorted, norm_final[None], tm=_tile(n, 512))
    return out.reshape(b, s, d)
```

```python
import functools
import math
from typing import NamedTuple

import jax
import jax.numpy as jnp
from jax import lax
from jax.experimental import pallas as pl
from jax.experimental.pallas import tpu as pltpu

RMS_EPS = 1e-6
POOL_WINDOWS = (2, 4, 8, 16)
SB_HEAD_DIM = 64
MEM_HEADS = 4
N_GROUPS = 4
EXPERTS_PER_GROUP = 4
N_EXPERTS = N_GROUPS * EXPERTS_PER_GROUP

LANES_V7X = 128
MXU_DIM_V7X = 256
VMEM_LIMIT_BYTES = 56 * 1024 * 1024

LOG2E = 1.4426950408889634
UNDERFLOW_EXPONENT = 104.0
MASKED_LOGIT = -1e30
POOL_HALO = 16

PAIRS = ((0, 1), (0, 2), (1, 2), (1, 3), (2, 3), (0, 3))
assert sorted(PAIRS) == [(a, b) for a in range(EXPERTS_PER_GROUP) for b in range(a + 1, EXPERTS_PER_GROUP)]
N_BUCKETS = N_GROUPS * len(PAIRS)
ROUTER_ROWS = 32
ROUTER_LANES = LANES_V7X
PAYLOAD_LANES = LANES_V7X
assert N_GROUPS + N_EXPERTS <= ROUTER_ROWS and N_BUCKETS <= ROUTER_ROWS


def _rms(x, gain):
    ms = jnp.mean(x * x, axis=-1, keepdims=True)
    return x * lax.rsqrt(ms + RMS_EPS) * gain


def _dot(a, b):
    return jnp.dot(a, b, preferred_element_type=jnp.float32)


def _dot_nt(a, b):
    return lax.dot_general(a, b, (((1,), (1,)), ((), ())), preferred_element_type=jnp.float32)


def _in_proj_kernel(x_ref, g_ref, w_ref, u_ref, qkv_ref, *, pool_width, attn_width, q_scale):
    xn = _rms(x_ref[...], g_ref[...]).astype(jnp.bfloat16)
    proj = _dot(xn, w_ref[...])
    u_ref[...] = proj[:, :pool_width]
    q = proj[:, pool_width:pool_width + attn_width] * q_scale
    qkv = jnp.concatenate([q, proj[:, pool_width + attn_width:]], axis=1).astype(jnp.bfloat16)
    for j in range(qkv_ref.shape[0]):
        qkv_ref[j] = qkv[:, j * LANES_V7X:(j + 1) * LANES_V7X]


def _in_proj(x2, gain, w_in_bf16, *, pool_width, attn_width, tm):
    n, d = x2.shape
    in_width = w_in_bf16.shape[1]
    n_slabs = 3 * attn_width // LANES_V7X
    kern = functools.partial(_in_proj_kernel, pool_width=pool_width, attn_width=attn_width,
                             q_scale=1.0 / math.sqrt(SB_HEAD_DIM))
    return pl.pallas_call(
        kern,
        grid=(n // tm,),
        in_specs=[
            pl.BlockSpec((tm, d), lambda i: (i, 0)),
            pl.BlockSpec((1, d), lambda i: (0, 0)),
            pl.BlockSpec((d, in_width), lambda i: (0, 0)),
        ],
        out_specs=[
            pl.BlockSpec((tm, pool_width), lambda i: (i, 0)),
            pl.BlockSpec((n_slabs, tm, LANES_V7X), lambda i: (0, i, 0)),
        ],
        out_shape=[
            jax.ShapeDtypeStruct((n, pool_width), jnp.float32),
            jax.ShapeDtypeStruct((n_slabs, n, LANES_V7X), jnp.bfloat16),
        ],
        compiler_params=pltpu.CompilerParams(
            dimension_semantics=("parallel",), vmem_limit_bytes=VMEM_LIMIT_BYTES),
        name="in_proj",
    )(x2, gain, w_in_bf16)


def _sb_tile(qh, kb, vb, tri, c, tail_mask):
    z = _dot_nt(qh, kb)
    if tail_mask is not None:
        m = tail_mask.shape[1]
        tail = jnp.where(tail_mask, z[:, -m:], MASKED_LOGIT)
        z = tail if m == z.shape[1] else jnp.concatenate([z[:, :-m], tail], axis=1)
    sp = (jnp.maximum(z, 0.0) + jnp.log(1.0 + jnp.exp2(jnp.abs(z) * (-LOG2E)))).astype(jnp.bfloat16)
    n = tri.shape[0]
    chunks, total = [], None
    for k0 in range(z.shape[1] - n, -1, -n):
        cum_k = _dot(sp[:, k0:k0 + n], tri)
        if total is not None:
            cum_k = cum_k + total
        total = cum_k[:, :1]
        chunks.insert(0, cum_k)
    cum = chunks[0] if len(chunks) == 1 else jnp.concatenate(chunks, axis=1)
    w = jnp.exp(z - cum) if c is None else jnp.exp(jnp.minimum(z - cum, 0.0) - c)
    return (total if c is None else c + total), _dot(w.astype(jnp.bfloat16), vb)


def _sb_attn_kernel(q_ref, k_ref, v_ref, tri_ref, o_ref, *, tq, sub, win):
    qi = pl.program_id(2)
    n_sub = tq // sub
    lane = lax.broadcasted_iota(jnp.int32, (tq, LANES_V7X), 1)
    head0 = lane < SB_HEAD_DIM
    q2 = q_ref[...]
    zero = jnp.zeros_like(q2)
    q_heads = (jnp.where(head0, q2, zero), jnp.where(head0, zero, q2))
    q_st = jnp.concatenate(q_heads, axis=0)

    def kv_block(j):
        start = pl.multiple_of(j * tq, tq)
        return k_ref[pl.ds(start, tq), :], v_ref[pl.ds(start, tq), :]

    def stacked_iota(rows, cols):
        row = lax.broadcasted_iota(jnp.int32, (2 * rows, cols), 0)
        col = lax.broadcasted_iota(jnp.int32, (2 * rows, cols), 1)
        return jnp.where(row >= rows, row - rows, row), col

    def first_block():
        row, col = stacked_iota(tq, tq)
        kb, vb = kv_block(0)
        return _sb_tile(q_st, kb, vb, tri_ref[...], None, col < row)

    def windows():
        row, col = stacked_iota(sub, sub)
        causal = col < row
        tri = tri_ref[...]
        parts = []
        for hf in range(n_sub):
            start = pl.multiple_of((qi * n_sub + hf + 1) * sub - win, sub)
            kw = k_ref[pl.ds(start, win), :]
            vw = v_ref[pl.ds(start, win), :]
            q_sub = jnp.concatenate([qh[hf * sub:(hf + 1) * sub] for qh in q_heads], axis=0)
            parts.append(_sb_tile(q_sub, kw, vw, tri, None, causal))
        return tuple(jnp.concatenate([p[i][h * sub:(h + 1) * sub] for h in range(2) for p in parts], axis=0)
                     for i in range(2))

    carry = lax.cond(qi == 0, first_block, windows)

    def live(carry):
        return jnp.min(carry[0], axis=0, keepdims=True)[0, 0] < UNDERFLOW_EXPONENT

    def add_block(carry, j, tail_mask):
        kb, vb = kv_block(j)
        c, acc = _sb_tile(q_st, kb, vb, tri_ref[...], carry[0], tail_mask)
        return c, carry[1] + acc

    def partial_block(back):
        def fn(carry):
            row, col = stacked_iota(tq, tq)
            fresh = col < (row // sub + 1) * sub - win + back * tq
            return add_block(carry, qi - back, fresh)
        return fn

    for back in ((0, 1) if win < tq else (1,)):
        carry = lax.cond(jnp.logical_and(qi > 0, live(carry)), partial_block(back), lambda cr: cr, carry)

    def cond(state):
        step, alive, _ = state
        return jnp.logical_and(step < qi - 1, alive > 0)

    def body(state):
        step, _, carry = state
        carry = add_block(carry, qi - 2 - step, None)
        return step + 1, live(carry).astype(jnp.int32), carry

    _, _, carry = lax.while_loop(cond, body, (jnp.int32(0), live(carry).astype(jnp.int32), carry))
    acc = carry[1]
    o_ref[...] = jnp.where(head0, acc[:tq], acc[tq:]).astype(o_ref.dtype)


def _sb_attention(qkv, *, tq, sub, win):
    n_slabs, b, s, _ = qkv.shape
    assert tq % sub == 0 and win % sub == 0 and sub <= win <= tq + sub, (tq, sub, win)
    n_pairs = n_slabs // 3
    n = math.gcd(tq, win)
    tri = (lax.broadcasted_iota(jnp.int32, (n, n), 0)
           >= lax.broadcasted_iota(jnp.int32, (n, n), 1)).astype(jnp.bfloat16)

    kern = functools.partial(_sb_attn_kernel, tq=tq, sub=sub, win=win)
    const2 = lambda bi, hp, qi: (0, 0)
    return pl.pallas_call(
        kern,
        grid=(b, n_pairs, s // tq),
        in_specs=[
            pl.BlockSpec((None, None, tq, LANES_V7X), lambda bi, hp, qi: (hp, bi, qi, 0)),
            pl.BlockSpec((None, None, s, LANES_V7X), lambda bi, hp, qi: (n_pairs + hp, bi, 0, 0)),
            pl.BlockSpec((None, None, s, LANES_V7X), lambda bi, hp, qi: (2 * n_pairs + hp, bi, 0, 0)),
            pl.BlockSpec((n, n), const2),
        ],
        out_specs=pl.BlockSpec((None, None, tq, LANES_V7X), lambda bi, hp, qi: (hp, bi, qi, 0)),
        out_shape=jax.ShapeDtypeStruct((n_pairs, b, s, LANES_V7X), jnp.bfloat16),
        compiler_params=pltpu.CompilerParams(
            dimension_semantics=("parallel", "parallel", "parallel"),
            vmem_limit_bytes=VMEM_LIMIT_BYTES),
        name="sb_attention",
    )(qkv, qkv, qkv, tri)


def _mem_kv_kernel(m_ref, g_ref, wk_ref, wv_ref, k_ref, v_ref):
    mn = _rms(m_ref[...], g_ref[...]).astype(jnp.bfloat16)
    k_ref[...] = _dot(mn, wk_ref[...]).astype(jnp.bfloat16)
    v_ref[...] = _dot(mn, wv_ref[...]).astype(jnp.bfloat16)


def _mem_kv(mem2, gain, wk, wv, *, tm):
    n, d = mem2.shape
    row = pl.BlockSpec((tm, d), lambda i: (i, 0))
    full = pl.BlockSpec((d, d), lambda i: (0, 0))
    return pl.pallas_call(
        _mem_kv_kernel,
        grid=(n // tm,),
        in_specs=[row, pl.BlockSpec((1, d), lambda i: (0, 0)), full, full],
        out_specs=[row, row],
        out_shape=[jax.ShapeDtypeStruct((n, d), jnp.bfloat16)] * 2,
        compiler_params=pltpu.CompilerParams(
            dimension_semantics=("parallel",), vmem_limit_bytes=VMEM_LIMIT_BYTES),
        name="mem_kv",
    )(mem2, gain, wk, wv)


def _route_t(logits):
    neg = jnp.float32(-jnp.inf)
    big = jnp.float32(ROUTER_ROWS)
    row = lax.broadcasted_iota(jnp.int32, logits.shape, 0).astype(jnp.float32)

    def col_max(mask):
        return jnp.max(jnp.where(mask, logits, neg), axis=0, keepdims=True)

    def first_argmax(mask, mx):
        return jnp.min(jnp.where(mask & (logits == mx), row, big), axis=0, keepdims=True)

    gmask = row < N_GROUPS
    gmax = col_max(gmask)
    gsum = jnp.sum(jnp.where(gmask, jnp.exp(logits - gmax), 0.0), axis=0, keepdims=True)
    g_gate = 1.0 / gsum
    g_idx = first_argmax(gmask, gmax)

    lo = N_GROUPS + EXPERTS_PER_GROUP * g_idx
    emask = (row >= lo) & (row < lo + EXPERTS_PER_GROUP)
    m1 = col_max(emask)
    i1 = first_argmax(emask, m1)
    mask2 = emask & (row != i1)
    m2 = col_max(mask2)
    i2 = first_argmax(mask2, m2)
    esum = jnp.sum(jnp.where(emask, jnp.exp(logits - m1), 0.0), axis=0, keepdims=True)
    p1 = 1.0 / esum
    p2 = jnp.exp(m2 - m1) / esum
    tot = p1 + p2
    w1 = g_gate * (p1 / tot)
    w2 = g_gate * (p2 / tot)

    first = i1 < i2
    la = jnp.where(first, i1, i2) - lo
    lb = jnp.where(first, i2, i1) - lo
    pair = sum(jnp.where((la == a) & (lb == b), float(p), 0.0) for p, (a, b) in enumerate(PAIRS))
    bucket = g_idx * len(PAIRS) + pair
    return bucket, jnp.where(first, w1, w2), jnp.where(first, w2, w1)


def _mix_cross_kernel(x_ref, u_ref, halo_ref, a_ref, wpbd_ref, ps_ref, wo_ref, gc_ref,
                      wq_ref, km_ref, vm_ref, wom_ref, gf_ref, wr_ref, su_ref,
                      h_ref, xrow_ref, bucket_ref, rank_ref, cnt_ref, run_ref, *, tm, tiles_per_seq):
    i = pl.program_id(0)
    pool_width = u_ref.shape[1]
    gw = pool_width // len(POOL_WINDOWS)
    tile_in_seq = i % tiles_per_seq
    first = tile_in_seq == 0
    pos = tile_in_seq * tm + lax.broadcasted_iota(jnp.int32, (tm, 1), 0)

    halo = jnp.where(first, 0.0, halo_ref[...])
    u = u_ref[...]
    pooled = []
    for g, w in enumerate(POOL_WINDOWS):
        ug = u[:, g * gw:(g + 1) * gw]
        ext = jnp.concatenate([halo[:, g * gw:(g + 1) * gw], ug], axis=0)
        shift = 1
        while shift < w:
            ext = ext + pltpu.roll(ext, shift, 0)
            shift *= 2
        win = ext[POOL_HALO:, :]
        inv_count = 1.0 / jnp.minimum(pos + 1, w).astype(jnp.float32)
        pooled.append((win * inv_count - ug).astype(jnp.bfloat16))

    pool_out = _dot(jnp.concatenate(pooled, axis=1), wpbd_ref[...]) * ps_ref[...]
    mixed = jnp.concatenate([pool_out.astype(jnp.bfloat16)] + [a_ref[j] for j in range(a_ref.shape[0])], axis=1)
    h1 = x_ref[...] + _dot(mixed, wo_ref[...])

    hn = _rms(h1, gc_ref[...]).astype(jnp.bfloat16)
    d = h1.shape[1]
    hd = d // MEM_HEADS
    q = (_dot(hn, wq_ref[...]) * (1.0 / math.sqrt(hd))).astype(jnp.bfloat16)
    outs = []
    for hh in range(MEM_HEADS):
        sl = slice(hh * hd, (hh + 1) * hd)
        s = _dot_nt(q[:, sl], km_ref[:, sl])
        e = jnp.exp(s - jnp.max(s, axis=-1, keepdims=True))
        p = e * (1.0 / jnp.sum(e, axis=-1, keepdims=True))
        outs.append(_dot(p.astype(jnp.bfloat16), vm_ref[:, sl]))
    o = jnp.concatenate(outs, axis=-1).astype(jnp.bfloat16)
    h2 = h1 + _dot(o, wom_ref[...])
    h_ref[...] = h2

    xn = _rms(h2, gf_ref[...])
    xrow_ref[:, :d] = xn
    x_hi = xn.astype(jnp.bfloat16)
    x_lo = (xn - x_hi.astype(jnp.float32)).astype(jnp.bfloat16)
    both = _dot(x_hi, wr_ref[...])
    logits_tm = both[:, :ROUTER_LANES] + both[:, ROUTER_LANES:] + _dot(x_lo, wr_ref[:, :ROUTER_LANES])
    logits = logits_tm.T[:ROUTER_ROWS]
    bucket, w_a, w_b = _route_t(logits)

    @pl.when(i == 0)
    def _():
        run_ref[...] = jnp.zeros_like(run_ref)

    brow = lax.broadcasted_iota(jnp.int32, (ROUTER_ROWS, tm), 0).astype(jnp.float32)
    onehot = (brow == bucket).astype(jnp.float32)
    before = _dot(onehot.astype(jnp.bfloat16), su_ref[...])
    run = run_ref[...]
    rank = jnp.sum(onehot * (before + run[:, :1]), axis=0, keepdims=True)
    run = run + jnp.sum(onehot, axis=1, keepdims=True)
    run_ref[...] = run
    cnt_ref[...] = run
    bucket_ref[...] = bucket.astype(jnp.int32)
    rank_ref[...] = rank.astype(jnp.int32)

    prow = lax.broadcasted_iota(jnp.int32, (PAYLOAD_LANES, tm), 0)
    payload_t = jnp.where(prow == 0, w_a, jnp.where(prow == 1, w_b, 0.0))
    xrow_ref[:, d:] = payload_t.T


def _mix_cross(x2, u, attn, w_pool_bd, pool_scale, w_out, g_cross, wq, kmem, vmem, wom, g_ffn, w_router,
               *, tm, seq, mem_len):
    n, d = x2.shape
    pool_width = u.shape[1]
    tiles_per_seq = seq // tm
    n_tiles = n // tm
    halo_blocks = tm // POOL_HALO
    su = (lax.broadcasted_iota(jnp.int32, (tm, tm), 0)
          < lax.broadcasted_iota(jnp.int32, (tm, tm), 1)).astype(jnp.bfloat16)
    kern = functools.partial(_mix_cross_kernel, tm=tm, tiles_per_seq=tiles_per_seq)
    const2 = lambda i: (0, 0)
    mem_map = lambda i: (i // tiles_per_seq, 0)
    return pl.pallas_call(
        kern,
        grid=(n_tiles,),
        in_specs=[
            pl.BlockSpec((tm, d), lambda i: (i, 0)),
            pl.BlockSpec((tm, pool_width), lambda i: (i, 0)),
            pl.BlockSpec((POOL_HALO, pool_width), lambda i: (jnp.maximum(i * halo_blocks - 1, 0), 0)),
            pl.BlockSpec((attn.shape[0], tm, LANES_V7X), lambda i: (0, i, 0)),
            pl.BlockSpec(w_pool_bd.shape, const2),
            pl.BlockSpec((1, pool_width), const2),
            pl.BlockSpec(w_out.shape, const2),
            pl.BlockSpec((1, d), const2),
            pl.BlockSpec((d, d), const2),
            pl.BlockSpec((mem_len, d), mem_map),
            pl.BlockSpec((mem_len, d), mem_map),
            pl.BlockSpec((d, d), const2),
            pl.BlockSpec((1, d), const2),
            pl.BlockSpec((d, 2 * ROUTER_LANES), const2),
            pl.BlockSpec((tm, tm), const2),
        ],
        out_specs=[
            pl.BlockSpec((tm, d), lambda i: (i, 0)),
            pl.BlockSpec((tm, d + PAYLOAD_LANES), lambda i: (i, 0)),
            pl.BlockSpec((None, 1, tm), lambda i: (i, 0, 0)),
            pl.BlockSpec((None, 1, tm), lambda i: (i, 0, 0)),
            pl.BlockSpec((ROUTER_ROWS, LANES_V7X), const2),
        ],
        out_shape=[
            jax.ShapeDtypeStruct((n, d), jnp.float32),
            jax.ShapeDtypeStruct((n, d + PAYLOAD_LANES), jnp.float32),
            jax.ShapeDtypeStruct((n_tiles, 1, tm), jnp.int32),
            jax.ShapeDtypeStruct((n_tiles, 1, tm), jnp.int32),
            jax.ShapeDtypeStruct((ROUTER_ROWS, LANES_V7X), jnp.float32),
        ],
        scratch_shapes=[pltpu.VMEM((ROUTER_ROWS, LANES_V7X), jnp.float32)],
        compiler_params=pltpu.CompilerParams(
            dimension_semantics=("arbitrary",), vmem_limit_bytes=VMEM_LIMIT_BYTES),
        name="mix_cross",
    )(x2, u, u, attn, w_pool_bd, pool_scale, w_out, g_cross, wq, kmem, vmem, wom, g_ffn, w_router, su)


def _dispatch_kernel(pos_ref, last_ref, n_last_ref, na_ref, x_ref, xs_ref, stage, zeros, sems, fill_sem,
                     tail_sem, *, tm, n_tiles, n_tiles_e):
    i = pl.program_id(0)
    slot = i % 2
    base = i * tm
    tm_e = zeros.shape[0]

    def fill_copy(j, sem):
        return pltpu.make_async_copy(zeros, xs_ref.at[pl.ds(pl.multiple_of(j * tm_e, tm_e), tm_e), :], sem)

    def for_range(lo, hi, fn):
        def body(k, carry):
            fn(k)
            return carry
        lax.fori_loop(lo, hi, body, 0)

    @pl.when(i == 0)
    def _():
        zeros[...] = jnp.zeros_like(zeros)
        for_range(0, n_last_ref[0], lambda k: fill_copy(last_ref[k], fill_sem).start())
        for_range(na_ref[0], n_tiles_e, lambda j: fill_copy(j, tail_sem).start())
        for_range(0, n_last_ref[0], lambda k: fill_copy(0, fill_sem).wait())

    def row_copy(slot, r, p):
        return pltpu.make_async_copy(stage.at[slot, pl.ds(r, 1), :], xs_ref.at[pl.ds(p, 1), :], sems.at[slot])

    def wait_slot(slot):
        pltpu.make_async_copy(stage.at[slot], xs_ref.at[pl.ds(0, tm), :], sems.at[slot]).wait()

    stage[slot] = x_ref[...]
    for r in range(tm):
        row_copy(slot, r, pos_ref[base + r]).start(priority=r % 2)

    @pl.when(i > 0)
    def _():
        wait_slot(1 - slot)

    @pl.when(i == n_tiles - 1)
    def _():
        wait_slot(slot)
        for_range(na_ref[0], n_tiles_e, lambda j: fill_copy(0, tail_sem).wait())


def _dispatch(pos, last_tiles, n_last, n_active, xrow, *, tm, tm_e, n_tiles_e):
    n, width = xrow.shape
    return pl.pallas_call(
        functools.partial(_dispatch_kernel, tm=tm, n_tiles=n // tm, n_tiles_e=n_tiles_e),
        grid_spec=pltpu.PrefetchScalarGridSpec(
            num_scalar_prefetch=4,
            grid=(n // tm,),
            in_specs=[pl.BlockSpec((tm, width), lambda i, *_: (i, 0))],
            out_specs=pl.BlockSpec(memory_space=pl.ANY),
            scratch_shapes=[pltpu.VMEM((2, tm, width), xrow.dtype), pltpu.VMEM((tm_e, width), xrow.dtype),
                            pltpu.SemaphoreType.DMA((2,)), pltpu.SemaphoreType.DMA, pltpu.SemaphoreType.DMA],
        ),
        out_shape=jax.ShapeDtypeStruct((n_tiles_e * tm_e, width), xrow.dtype),
        compiler_params=pltpu.CompilerParams(
            dimension_semantics=("arbitrary",), vmem_limit_bytes=VMEM_LIMIT_BYTES),
        name="moe_dispatch",
    )(pos, last_tiles, n_last, n_active, xrow)


def _expert_kernel(ea_ref, eb_ref, na_ref, xs_ref, wga_ref, wua_ref, wda_ref, wgb_ref, wub_ref, wdb_ref, y_ref,
                   wgua_s, wda_s, wgub_s, wdb_s, *, d):
    j = pl.program_id(0)
    active = j < na_ref[0]

    def refresh(e_ref, wg_ref, wu_ref, wd_ref, wgu_s, wd_s):
        @pl.when(jnp.logical_or(j == 0, e_ref[j] != e_ref[jnp.maximum(j - 1, 0)]))
        def _():
            ff = wg_ref.shape[1]
            wgu_s[:, :ff] = wg_ref[...].astype(jnp.bfloat16)
            wgu_s[:, ff:] = wu_ref[...].astype(jnp.bfloat16)
            wd_s[...] = wd_ref[...].astype(jnp.bfloat16)

    @pl.when(jnp.logical_not(active))
    def _():
        y_ref[...] = jnp.zeros_like(y_ref)

    @pl.when(active)
    def _():
        refresh(ea_ref, wga_ref, wua_ref, wda_ref, wgua_s, wda_s)
        refresh(eb_ref, wgb_ref, wub_ref, wdb_ref, wgub_s, wdb_s)
        x = xs_ref[:, :d].astype(jnp.bfloat16)

        def mlp(wgu_ref, wd_ref):
            gu = _dot(x, wgu_ref[...])
            ff = gu.shape[1] // 2
            gate, up = gu[:, :ff], gu[:, ff:]
            hmid = (gate * (1.0 / (1.0 + jnp.exp(-gate)))) * up
            return _dot(hmid.astype(jnp.bfloat16), wd_ref[...])

        y_ref[...] = (xs_ref[:, d:d + 1] * mlp(wgua_s, wda_s)
                      + xs_ref[:, d + 1:d + 2] * mlp(wgub_s, wdb_s))


def _experts(tile_ea, tile_eb, n_active, xs, wg, wu, wd, *, tm):
    n_pad, width = xs.shape
    d, ff = wd.shape[2], wd.shape[1]
    bf = jnp.bfloat16
    row_map = lambda j, ea, eb, na: (jnp.minimum(j, na[0] - 1), 0)
    a_map = lambda j, ea, eb, na: (ea[j], 0, 0)
    b_map = lambda j, ea, eb, na: (eb[j], 0, 0)
    return pl.pallas_call(
        functools.partial(_expert_kernel, d=d),
        grid_spec=pltpu.PrefetchScalarGridSpec(
            num_scalar_prefetch=3,
            grid=(n_pad // tm,),
            in_specs=[
                pl.BlockSpec((tm, width), row_map),
                pl.BlockSpec((None, d, ff), a_map),
                pl.BlockSpec((None, d, ff), a_map),
                pl.BlockSpec((None, ff, d), a_map),
                pl.BlockSpec((None, d, ff), b_map),
                pl.BlockSpec((None, d, ff), b_map),
                pl.BlockSpec((None, ff, d), b_map),
            ],
            out_specs=pl.BlockSpec((tm, d), lambda j, ea, eb, na: (j, 0)),
            scratch_shapes=[pltpu.VMEM((d, 2 * ff), bf), pltpu.VMEM((ff, d), bf),
                            pltpu.VMEM((d, 2 * ff), bf), pltpu.VMEM((ff, d), bf)],
        ),
        out_shape=jax.ShapeDtypeStruct((n_pad, d), jnp.float32),
        compiler_params=pltpu.CompilerParams(
            dimension_semantics=("arbitrary",), vmem_limit_bytes=VMEM_LIMIT_BYTES),
        name="moe_experts",
    )(tile_ea, tile_eb, n_active, xs, wg, wu, wd, wg, wu, wd)


def _combine_kernel(pos_ref, h_ref, y_ref, g_ref, o_ref, ybuf, sems, *, tm, n_tiles):
    i = pl.program_id(0)

    def row_copy(slot, r, p):
        return pltpu.make_async_copy(y_ref.at[pl.ds(p, 1), :], ybuf.at[slot, pl.ds(r, 1), :], sems.at[slot])

    @pl.when(i < n_tiles)
    def _():
        slot = i % 2
        for r in range(tm):
            row_copy(slot, r, pos_ref[i * tm + r]).start(priority=r % 2)

    @pl.when(i > 0)
    def _():
        slot = (i - 1) % 2
        pltpu.make_async_copy(y_ref.at[pl.ds(0, tm), :], ybuf.at[slot], sems.at[slot]).wait()
        o_ref[...] = _rms(h_ref[...] + ybuf[slot], g_ref[...])


def _combine(pos, h, y_sorted, g_final, *, tm):
    n, d = h.shape
    n_tiles = n // tm
    prev_tile = lambda i, pos: (jnp.maximum(i - 1, 0), 0)
    return pl.pallas_call(
        functools.partial(_combine_kernel, tm=tm, n_tiles=n_tiles),
        grid_spec=pltpu.PrefetchScalarGridSpec(
            num_scalar_prefetch=1,
            grid=(n_tiles + 1,),
            in_specs=[
                pl.BlockSpec((tm, d), prev_tile),
                pl.BlockSpec(memory_space=pl.ANY),
                pl.BlockSpec((1, d), lambda i, pos: (0, 0)),
            ],
            out_specs=pl.BlockSpec((tm, d), prev_tile),
            scratch_shapes=[pltpu.VMEM((2, tm, d), jnp.float32), pltpu.SemaphoreType.DMA((2,))],
        ),
        out_shape=jax.ShapeDtypeStruct((n, d), jnp.float32),
        compiler_params=pltpu.CompilerParams(
            dimension_semantics=("arbitrary",), vmem_limit_bytes=VMEM_LIMIT_BYTES),
        name="moe_combine",
    )(pos, h, y_sorted, g_final)


def _tile(n, pref):
    t = min(n, pref)
    assert n % t == 0, (n, t)
    return t


class _Tiles(NamedTuple):
    in_proj: int
    attn_q: int
    attn_sub: int
    mix_cross: int
    dispatch: int
    experts: int
    combine: int


def _tiles(n, s):
    m = MXU_DIM_V7X
    attn_q = _tile(s, 2 * m)
    return _Tiles(in_proj=_tile(n, 4 * m), attn_q=attn_q, attn_sub=min(attn_q, m), mix_cross=_tile(s, 4 * m),
                  dispatch=_tile(n, 4 * m), experts=_tile(n, 2 * m), combine=_tile(n, 2 * m))


def _sorted_layout(bucket, rank, counts, *, tm_e, n_tiles_e):
    seg_tiles = (counts + tm_e - 1) // tm_e
    seg_end = jnp.cumsum(seg_tiles)
    seg_start = seg_end - seg_tiles
    pos = (seg_start * tm_e)[bucket] + rank
    nonempty = seg_tiles > 0
    last_tiles = jnp.sort(jnp.where(nonempty, seg_end - 1, n_tiles_e))
    n_active = seg_end[-1]
    tile = jnp.minimum(jnp.arange(n_tiles_e, dtype=jnp.int32), n_active - 1)
    tile_bucket = jnp.sum((tile[:, None] >= seg_end[None, :]).astype(jnp.int32), axis=1)
    group, pair = tile_bucket // len(PAIRS), tile_bucket % len(PAIRS)
    pair_a = jnp.array([p[0] for p in PAIRS], jnp.int32)
    pair_b = jnp.array([p[1] for p in PAIRS], jnp.int32)
    tile_ea = group * EXPERTS_PER_GROUP + pair_a[pair]
    tile_eb = group * EXPERTS_PER_GROUP + pair_b[pair]
    i32 = lambda a: a.astype(jnp.int32)
    return i32(pos), i32(last_tiles), i32(jnp.sum(nonempty)), i32(tile_ea), i32(tile_eb), i32(n_active)


def kernel(x, mem, norm_mix, w_in, w_pool, pool_scale, w_out, norm_cross, norm_mem, w_q_mem, w_k_mem,
           w_v_mem, w_o_mem, norm_ffn, w_group, w_expert, w_gate, w_up, w_down, norm_final):
    b, s, d = x.shape
    mem_len = mem.shape[1]
    depth = norm_mix.shape[0]
    pool_width = pool_scale.shape[1]
    attn_width = w_out.shape[1] - pool_width
    bf = jnp.bfloat16
    n = b * s

    assert depth == 1, "single-layer problem: the final RMSNorm is fused into the combine kernel"
    l = 0
    t = _tiles(n, s)
    h = x.reshape(n, d)
    u, qkv = _in_proj(h, norm_mix[l][None], w_in[l].astype(bf),
                      pool_width=pool_width, attn_width=attn_width, tm=t.in_proj)
    attn = _sb_attention(qkv.reshape(-1, b, s, LANES_V7X), tq=t.attn_q, sub=t.attn_sub, win=2 * t.attn_sub)
    kmem, vmem = _mem_kv(mem.reshape(b * mem_len, d), norm_mem[l][None],
                         w_k_mem[l].astype(bf), w_v_mem[l].astype(bf), tm=b * mem_len)

    w_router = jnp.concatenate([w_group[l], w_expert[l]], axis=1)
    w_router = jnp.pad(w_router, ((0, 0), (0, ROUTER_LANES - w_router.shape[1])))
    wr_hi = w_router.astype(bf)
    wr_lo = (w_router - wr_hi.astype(jnp.float32)).astype(bf)
    w_pool_bd = jax.scipy.linalg.block_diag(*w_pool[l]).astype(bf)
    h, xrow, bucket, rank, counts = _mix_cross(
        h, u, attn.reshape(-1, n, LANES_V7X), w_pool_bd, pool_scale[l][None], w_out[l].astype(bf),
        norm_cross[l][None], w_q_mem[l].astype(bf), kmem, vmem, w_o_mem[l].astype(bf),
        norm_ffn[l][None], jnp.concatenate([wr_hi, wr_lo], axis=1), tm=t.mix_cross, seq=s, mem_len=mem_len)

    tm_e = t.experts
    n_tiles_e = -(-(n + N_BUCKETS * (tm_e - 1)) // tm_e)
    pos, last_tiles, n_last, tile_ea, tile_eb, n_active = _sorted_layout(
        bucket.reshape(n), rank.reshape(n), counts[:N_BUCKETS, 0].astype(jnp.int32),
        tm_e=tm_e, n_tiles_e=n_tiles_e)
    xs = _dispatch(pos, last_tiles, n_last[None], n_active[None], xrow, tm=t.dispatch, tm_e=tm_e,
                   n_tiles_e=n_tiles_e)
    y_sorted = _experts(tile_ea, tile_eb, n_active[None], xs, w_gate[l], w_up[l], w_down[l], tm=tm_e)
    out = _combine(pos, h, y_sorted, norm_final[None], tm=t.combine)
    return out.reshape(b, s, d)
```
